```python
import jax, jax.numpy as jnp
from jax import lax
import numpy as np

D_MODEL = 1024
BATCH = 8
SEQ = 16384
DEPTH = 4

MIX_WIDTH = D_MODEL
POOL_WIDTH = D_MODEL // 4
SCONV_WIDTH = 3 * D_MODEL // 8
CCONV_WIDTH = MIX_WIDTH - POOL_WIDTH - SCONV_WIDTH
HEAD_DIM = 64
POOL_WINDOWS = (2, 4, 8, 16)
N_POOL_GROUPS = len(POOL_WINDOWS)
POOL_GROUP = POOL_WIDTH // N_POOL_GROUPS
SCONV_K = 3
CCONV_K = 31
IN_COLS = POOL_WIDTH + 3 * SCONV_WIDTH + 2 * CCONV_WIDTH
D_FF = 2816
LN_EPS = 1e-5
DEEPNORM_ALPHA = (2.0 * DEPTH) ** 0.25
DEEPNORM_BETA = (8.0 * DEPTH) ** -0.25

kernel_name = "hybrid_pool_conv_conformer_encoder"


def layer_norm(x, g, b):
    xf = x.astype(jnp.float32)
    mu = jnp.mean(xf, axis=-1, keepdims=True)
    var = jnp.mean(jnp.square(xf - mu), axis=-1, keepdims=True)
    y = (xf - mu) * lax.rsqrt(var + LN_EPS)
    return (y * g.astype(jnp.float32) + b.astype(jnp.float32)).astype(x.dtype)


def swiglu_ffn(x, w_gate, w_up, w_down):
    return (jax.nn.silu(x @ w_gate) * (x @ w_up)) @ w_down


def depthwise_conv(u, w):
    k = w.shape[0]
    return lax.conv_general_dilated(
        u, w[:, None, :], window_strides=(1,), padding=[(k // 2, k // 2)],
        dimension_numbers=("NWC", "WIO", "NWC"), feature_group_count=u.shape[-1])


def centred_pool_minus_self(u, window):
    seq = u.shape[1]
    uf = u.astype(jnp.float32)
    csum = jnp.pad(jnp.cumsum(uf, axis=1), ((0, 0), (1, 0), (0, 0)))
    t = jnp.arange(seq)
    left = window // 2
    lo = jnp.clip(t - left, 0, seq)
    hi = jnp.clip(t - left + window, 0, seq)
    total = jnp.take(csum, hi, axis=1) - jnp.take(csum, lo, axis=1)
    count = (hi - lo).astype(jnp.float32)
    return (total / count[None, :, None] - uf).astype(u.dtype)


def hybrid_mixer(h, w_in, pool_w, pool_scale, sconv_w, cconv_w, cconv_b, cnorm_g, cnorm_b, w_out):
    bsz, seq = h.shape[0], h.shape[1]
    proj = h @ w_in
    cuts = [POOL_WIDTH,
            POOL_WIDTH + SCONV_WIDTH,
            POOL_WIDTH + 2 * SCONV_WIDTH,
            POOL_WIDTH + 3 * SCONV_WIDTH,
            POOL_WIDTH + 3 * SCONV_WIDTH + CCONV_WIDTH]
    u_pool, gate_b, gate_c, v, c_val, c_gate = jnp.split(proj, cuts, axis=-1)

    pooled = jnp.stack(
        [centred_pool_minus_self(u_pool[..., g * POOL_GROUP:(g + 1) * POOL_GROUP], w)
         for g, w in enumerate(POOL_WINDOWS)], axis=2)
    y_a = jnp.einsum("bsgc,gcd->bsgd", pooled, pool_w).reshape(bsz, seq, POOL_WIDTH) * pool_scale

    y_b = gate_b * depthwise_conv(gate_c * v, sconv_w)

    a = c_val * jax.nn.sigmoid(c_gate)
    a = depthwise_conv(a, cconv_w) + cconv_b
    y_c = jax.nn.silu(layer_norm(a, cnorm_g, cnorm_b))

    return jnp.concatenate([y_a, y_b, y_c], axis=-1) @ w_out


def _fwd_setup_inputs(seed: int = 0) -> dict:
    key = jax.random.key(seed)
    ks = jax.random.split(key, 22)

    def nrm(k, shape, scale):
        return jax.random.normal(k, shape, jnp.float32) * scale

    d, f = D_MODEL, D_FF
    return {
        "x": nrm(ks[0], (BATCH, SEQ, d), 1.0),
        "ln1_g": 1.0 + nrm(ks[1], (DEPTH, d), 0.05),
        "ln1_b": nrm(ks[2], (DEPTH, d), 0.02),
        "ffn1_w_gate": nrm(ks[3], (DEPTH, d, f), d ** -0.5),
        "ffn1_w_up": nrm(ks[4], (DEPTH, d, f), d ** -0.5),
        "ffn1_w_down": nrm(ks[5], (DEPTH, f, d), DEEPNORM_BETA * f ** -0.5),
        "mix_w_in": nrm(ks[6], (DEPTH, d, IN_COLS), d ** -0.5),
        "pool_w": nrm(ks[7], (DEPTH, N_POOL_GROUPS, POOL_GROUP, POOL_GROUP), POOL_GROUP ** -0.5),
        "pool_scale": 1.0 + nrm(ks[8], (DEPTH, POOL_WIDTH), 0.05),
        "sconv_w": nrm(ks[9], (DEPTH, SCONV_K, SCONV_WIDTH), SCONV_K ** -0.5),
        "cconv_w": nrm(ks[10], (DEPTH, CCONV_K, CCONV_WIDTH), CCONV_K ** -0.5),
        "cconv_b": nrm(ks[11], (DEPTH, CCONV_WIDTH), 0.02),
        "cnorm_g": 1.0 + nrm(ks[12], (DEPTH, CCONV_WIDTH), 0.05),
        "cnorm_b": nrm(ks[13], (DEPTH, CCONV_WIDTH), 0.02),
        "mix_w_out": nrm(ks[14], (DEPTH, MIX_WIDTH, d), DEEPNORM_BETA * MIX_WIDTH ** -0.5),
        "ln2_g": 1.0 + nrm(ks[15], (DEPTH, d), 0.05),
        "ln2_b": nrm(ks[16], (DEPTH, d), 0.02),
        "ffn2_w_gate": nrm(ks[17], (DEPTH, d, f), d ** -0.5),
        "ffn2_w_up": nrm(ks[18], (DEPTH, d, f), d ** -0.5),
        "ffn2_w_down": nrm(ks[19], (DEPTH, f, d), DEEPNORM_BETA * f ** -0.5),
        "ln3_g": 1.0 + nrm(ks[20], (DEPTH, d), 0.05),
        "ln3_b": nrm(ks[21], (DEPTH, d), 0.02),
    }


def _fwd_reference(x, ln1_g, ln1_b, ffn1_w_gate, ffn1_w_up, ffn1_w_down, mix_w_in, pool_w,
              pool_scale, sconv_w, cconv_w, cconv_b, cnorm_g, cnorm_b, mix_w_out,
              ln2_g, ln2_b, ffn2_w_gate, ffn2_w_up, ffn2_w_down, ln3_g, ln3_b):
    for l in range(DEPTH):
        x = layer_norm(DEEPNORM_ALPHA * x
                       + 0.5 * swiglu_ffn(x, ffn1_w_gate[l], ffn1_w_up[l], ffn1_w_down[l]),
                       ln1_g[l], ln1_b[l])
        x = layer_norm(DEEPNORM_ALPHA * x
                       + hybrid_mixer(x, mix_w_in[l], pool_w[l], pool_scale[l], sconv_w[l],
                                      cconv_w[l], cconv_b[l], cnorm_g[l], cnorm_b[l], mix_w_out[l]),
                       ln2_g[l], ln2_b[l])
        x = layer_norm(DEEPNORM_ALPHA * x
                       + 0.5 * swiglu_ffn(x, ffn2_w_gate[l], ffn2_w_up[l], ffn2_w_down[l]),
                       ln3_g[l], ln3_b[l])
    return x


import jax as _jax
import jax.numpy as _jnp

TWIN_FORMAT = 'train_step'
FWD_PARAMS = ['x', 'ln1_g', 'ln1_b', 'ffn1_w_gate', 'ffn1_w_up', 'ffn1_w_down', 'mix_w_in', 'pool_w', 'pool_scale', 'sconv_w', 'cconv_w', 'cconv_b', 'cnorm_g', 'cnorm_b', 'mix_w_out', 'ln2_g', 'ln2_b', 'ffn2_w_gate', 'ffn2_w_up', 'ffn2_w_down', 'ln3_g', 'ln3_b']
TWIN_WEIGHTS = ['ln1_g', 'ln1_b', 'ffn1_w_gate', 'ffn1_w_up', 'ffn1_w_down', 'mix_w_in', 'pool_w', 'pool_scale', 'sconv_w', 'cconv_w', 'cconv_b', 'cnorm_g', 'cnorm_b', 'mix_w_out', 'ln2_g', 'ln2_b', 'ffn2_w_gate', 'ffn2_w_up', 'ffn2_w_down', 'ln3_g', 'ln3_b']
TWIN_DIFF_INPUT = 'x'
TWIN_INPUTS = ['x', 'ln1_g', 'ln1_b', 'ffn1_w_gate', 'ffn1_w_up', 'ffn1_w_down', 'mix_w_in', 'pool_w', 'pool_scale', 'sconv_w', 'cconv_w', 'cconv_b', 'cnorm_g', 'cnorm_b', 'mix_w_out', 'ln2_g', 'ln2_b', 'ffn2_w_gate', 'ffn2_w_up', 'ffn2_w_down', 'ln3_g', 'ln3_b', 'loss_target', 'm_ln1_g', 'm_ln1_b', 'm_ffn1_w_gate', 'm_ffn1_w_up', 'm_ffn1_w_down', 'm_mix_w_in', 'm_pool_w', 'm_pool_scale', 'm_sconv_w', 'm_cconv_w', 'm_cconv_b', 'm_cnorm_g', 'm_cnorm_b', 'm_mix_w_out', 'm_ln2_g', 'm_ln2_b', 'm_ffn2_w_gate', 'm_ffn2_w_up', 'm_ffn2_w_down', 'm_ln3_g', 'm_ln3_b', 'v_ln1_g', 'v_ln1_b', 'v_ffn1_w_gate', 'v_ffn1_w_up', 'v_ffn1_w_down', 'v_mix_w_in', 'v_pool_w', 'v_pool_scale', 'v_sconv_w', 'v_cconv_w', 'v_cconv_b', 'v_cnorm_g', 'v_cnorm_b', 'v_mix_w_out', 'v_ln2_g', 'v_ln2_b', 'v_ffn2_w_gate', 'v_ffn2_w_up', 'v_ffn2_w_down', 'v_ln3_g', 'v_ln3_b']
TWIN_OUTPUTS = ['loss', 'grad_x', 'grad_ln1_g', 'grad_ln1_b', 'grad_ffn1_w_gate', 'grad_ffn1_w_up', 'grad_ffn1_w_down', 'grad_mix_w_in', 'grad_pool_w', 'grad_pool_scale', 'grad_sconv_w', 'grad_cconv_w', 'grad_cconv_b', 'grad_cnorm_g', 'grad_cnorm_b', 'grad_mix_w_out', 'grad_ln2_g', 'grad_ln2_b', 'grad_ffn2_w_gate', 'grad_ffn2_w_up', 'grad_ffn2_w_down', 'grad_ln3_g', 'grad_ln3_b', 'delta_ln1_g', 'delta_ln1_b', 'delta_ffn1_w_gate', 'delta_ffn1_w_up', 'delta_ffn1_w_down', 'delta_mix_w_in', 'delta_pool_w', 'delta_pool_scale', 'delta_sconv_w', 'delta_cconv_w', 'delta_cconv_b', 'delta_cnorm_g', 'delta_cnorm_b', 'delta_mix_w_out', 'delta_ln2_g', 'delta_ln2_b', 'delta_ffn2_w_gate', 'delta_ffn2_w_up', 'delta_ffn2_w_down', 'delta_ln3_g', 'delta_ln3_b', 'new_m_ln1_g', 'new_m_ln1_b', 'new_m_ffn1_w_gate', 'new_m_ffn1_w_up', 'new_m_ffn1_w_down', 'new_m_mix_w_in', 'new_m_pool_w', 'new_m_pool_scale', 'new_m_sconv_w', 'new_m_cconv_w', 'new_m_cconv_b', 'new_m_cnorm_g', 'new_m_cnorm_b', 'new_m_mix_w_out', 'new_m_ln2_g', 'new_m_ln2_b', 'new_m_ffn2_w_gate', 'new_m_ffn2_w_up', 'new_m_ffn2_w_down', 'new_m_ln3_g', 'new_m_ln3_b', 'new_v_ln1_g', 'new_v_ln1_b', 'new_v_ffn1_w_gate', 'new_v_ffn1_w_up', 'new_v_ffn1_w_down', 'new_v_mix_w_in', 'new_v_pool_w', 'new_v_pool_scale', 'new_v_sconv_w', 'new_v_cconv_w', 'new_v_cconv_b', 'new_v_cnorm_g', 'new_v_cnorm_b', 'new_v_mix_w_out', 'new_v_ln2_g', 'new_v_ln2_b', 'new_v_ffn2_w_gate', 'new_v_ffn2_w_up', 'new_v_ffn2_w_down', 'new_v_ln3_g', 'new_v_ln3_b']
TWIN_LEAF_KINDS = {'loss': 'loss', 'grad_x': 'grad_x', 'grad_ln1_g': 'grad_w', 'grad_ln1_b': 'grad_w', 'grad_ffn1_w_gate': 'grad_w', 'grad_ffn1_w_up': 'grad_w', 'grad_ffn1_w_down': 'grad_w', 'grad_mix_w_in': 'grad_w', 'grad_pool_w': 'grad_w', 'grad_pool_scale': 'grad_w', 'grad_sconv_w': 'grad_w', 'grad_cconv_w': 'grad_w', 'grad_cconv_b': 'grad_w', 'grad_cnorm_g': 'grad_w', 'grad_cnorm_b': 'grad_w', 'grad_mix_w_out': 'grad_w', 'grad_ln2_g': 'grad_w', 'grad_ln2_b': 'grad_w', 'grad_ffn2_w_gate': 'grad_w', 'grad_ffn2_w_up': 'grad_w', 'grad_ffn2_w_down': 'grad_w', 'grad_ln3_g': 'grad_w', 'grad_ln3_b': 'grad_w', 'delta_ln1_g': 'delta_w', 'delta_ln1_b': 'delta_w', 'delta_ffn1_w_gate': 'delta_w', 'delta_ffn1_w_up': 'delta_w', 'delta_ffn1_w_down': 'delta_w', 'delta_mix_w_in': 'delta_w', 'delta_pool_w': 'delta_w', 'delta_pool_scale': 'delta_w', 'delta_sconv_w': 'delta_w', 'delta_cconv_w': 'delta_w', 'delta_cconv_b': 'delta_w', 'delta_cnorm_g': 'delta_w', 'delta_cnorm_b': 'delta_w', 'delta_mix_w_out': 'delta_w', 'delta_ln2_g': 'delta_w', 'delta_ln2_b': 'delta_w', 'delta_ffn2_w_gate': 'delta_w', 'delta_ffn2_w_up': 'delta_w', 'delta_ffn2_w_down': 'delta_w', 'delta_ln3_g': 'delta_w', 'delta_ln3_b': 'delta_w', 'new_m_ln1_g': 'new_m', 'new_m_ln1_b': 'new_m', 'new_m_ffn1_w_gate': 'new_m', 'new_m_ffn1_w_up': 'new_m', 'new_m_ffn1_w_down': 'new_m', 'new_m_mix_w_in': 'new_m', 'new_m_pool_w': 'new_m', 'new_m_pool_scale': 'new_m', 'new_m_sconv_w': 'new_m', 'new_m_cconv_w': 'new_m', 'new_m_cconv_b': 'new_m', 'new_m_cnorm_g': 'new_m', 'new_m_cnorm_b': 'new_m', 'new_m_mix_w_out': 'new_m', 'new_m_ln2_g': 'new_m', 'new_m_ln2_b': 'new_m', 'new_m_ffn2_w_gate': 'new_m', 'new_m_ffn2_w_up': 'new_m', 'new_m_ffn2_w_down': 'new_m', 'new_m_ln3_g': 'new_m', 'new_m_ln3_b': 'new_m', 'new_v_ln1_g': 'new_v', 'new_v_ln1_b': 'new_v', 'new_v_ffn1_w_gate': 'new_v', 'new_v_ffn1_w_up': 'new_v', 'new_v_ffn1_w_down': 'new_v', 'new_v_mix_w_in': 'new_v', 'new_v_pool_w': 'new_v', 'new_v_pool_scale': 'new_v', 'new_v_sconv_w': 'new_v', 'new_v_cconv_w': 'new_v', 'new_v_cconv_b': 'new_v', 'new_v_cnorm_g': 'new_v', 'new_v_cnorm_b': 'new_v', 'new_v_mix_w_out': 'new_v', 'new_v_ln2_g': 'new_v', 'new_v_ln2_b': 'new_v', 'new_v_ffn2_w_gate': 'new_v', 'new_v_ffn2_w_up': 'new_v', 'new_v_ffn2_w_down': 'new_v', 'new_v_ln3_g': 'new_v', 'new_v_ln3_b': 'new_v'}


def _forward(args):
    return _fwd_reference(*[args[k] for k in FWD_PARAMS])


def _output_shape():
    def fwd():
        inp = _fwd_setup_inputs(0)
        return _fwd_reference(*[inp[k] for k in FWD_PARAMS])
    out = _jax.eval_shape(fwd)
    return out.shape, out.dtype

N_MICROBATCH = 1
ADAM_LR = 0.001
ADAM_B1 = 0.9
ADAM_B2 = 0.999
ADAM_EPS = 1e-08
ADAM_WD = 0.01
ADAM_STEP = 10
PER_EXAMPLE_BATCH_AXIS = {'x': 0, 'loss_target': 0}
SHARED_INPUTS = []
_WEIGHT_DTYPES = {'ln1_g': _jnp.float32, 'ln1_b': _jnp.float32, 'ffn1_w_gate': _jnp.float32, 'ffn1_w_up': _jnp.float32, 'ffn1_w_down': _jnp.float32, 'mix_w_in': _jnp.float32, 'pool_w': _jnp.float32, 'pool_scale': _jnp.float32, 'sconv_w': _jnp.float32, 'cconv_w': _jnp.float32, 'cconv_b': _jnp.float32, 'cnorm_g': _jnp.float32, 'cnorm_b': _jnp.float32, 'mix_w_out': _jnp.float32, 'ln2_g': _jnp.float32, 'ln2_b': _jnp.float32, 'ffn2_w_gate': _jnp.float32, 'ffn2_w_up': _jnp.float32, 'ffn2_w_down': _jnp.float32, 'ln3_g': _jnp.float32, 'ln3_b': _jnp.float32}
MOMENT_SCALE = {'ln1_g': 1.155990e+01, 'ln1_b': 1.909008e+00, 'ffn1_w_gate': 1.779552e-02, 'ffn1_w_up': 1.749477e-02, 'ffn1_w_down': 6.887661e-02, 'mix_w_in': 7.806248e-02, 'pool_w': 8.553342e-02, 'pool_scale': 8.660794e-02, 'sconv_w': 9.125371e-02, 'cconv_w': 5.943698e-02, 'cconv_b': 3.746788e-01, 'cnorm_g': 1.328291e-01, 'cnorm_b': 2.098835e-01, 'mix_w_out': 2.076653e-01, 'ln2_g': 1.203644e+01, 'ln2_b': 1.978008e+00, 'ffn2_w_gate': 1.728818e-02, 'ffn2_w_up': 1.704140e-02, 'ffn2_w_down': 6.722260e-02, 'ln3_g': 6.773973e+01, 'ln3_b': 5.665244e+00}


def _to_microbatches(a, axis):
    t = _jnp.moveaxis(a, axis, 0)
    t = t.reshape((N_MICROBATCH, t.shape[0] // N_MICROBATCH) + t.shape[1:])
    return _jnp.moveaxis(t, 1, axis + 1)


def setup_inputs(seed: int = 0) -> dict:
    inp = _fwd_setup_inputs(seed)
    key = _jax.random.fold_in(_jax.random.key(seed), 7919)
    shape, _ = _output_shape()
    out = dict(inp)
    out["loss_target"] = _jax.random.normal(_jax.random.fold_in(key, 0), shape, _jnp.float32)
    for i, name in enumerate(TWIN_WEIGHTS):
        w = inp[name].astype(_jnp.float32)
        if MOMENT_SCALE is None:
            s = _jnp.sqrt(_jnp.mean(_jnp.square(w)) + 1e-30)
        else:
            s = MOMENT_SCALE[name]
        km, kv = _jax.random.split(_jax.random.fold_in(key, i + 1))
        out[name] = w
        out["m_" + name] = s * _jax.random.normal(km, w.shape, _jnp.float32)
        out["v_" + name] = (s * s) * _jax.random.uniform(kv, w.shape, _jnp.float32, 0.5, 1.5)
    if N_MICROBATCH > 1:
        for name, axis in PER_EXAMPLE_BATCH_AXIS.items():
            out[name] = _to_microbatches(out[name], axis)
    return {'x': out['x'], 'ln1_g': out['ln1_g'], 'ln1_b': out['ln1_b'], 'ffn1_w_gate': out['ffn1_w_gate'], 'ffn1_w_up': out['ffn1_w_up'], 'ffn1_w_down': out['ffn1_w_down'], 'mix_w_in': out['mix_w_in'], 'pool_w': out['pool_w'], 'pool_scale': out['pool_scale'], 'sconv_w': out['sconv_w'], 'cconv_w': out['cconv_w'], 'cconv_b': out['cconv_b'], 'cnorm_g': out['cnorm_g'], 'cnorm_b': out['cnorm_b'], 'mix_w_out': out['mix_w_out'], 'ln2_g': out['ln2_g'], 'ln2_b': out['ln2_b'], 'ffn2_w_gate': out['ffn2_w_gate'], 'ffn2_w_up': out['ffn2_w_up'], 'ffn2_w_down': out['ffn2_w_down'], 'ln3_g': out['ln3_g'], 'ln3_b': out['ln3_b'], 'loss_target': out['loss_target'], 'm_ln1_g': out['m_ln1_g'], 'm_ln1_b': out['m_ln1_b'], 'm_ffn1_w_gate': out['m_ffn1_w_gate'], 'm_ffn1_w_up': out['m_ffn1_w_up'], 'm_ffn1_w_down': out['m_ffn1_w_down'], 'm_mix_w_in': out['m_mix_w_in'], 'm_pool_w': out['m_pool_w'], 'm_pool_scale': out['m_pool_scale'], 'm_sconv_w': out['m_sconv_w'], 'm_cconv_w': out['m_cconv_w'], 'm_cconv_b': out['m_cconv_b'], 'm_cnorm_g': out['m_cnorm_g'], 'm_cnorm_b': out['m_cnorm_b'], 'm_mix_w_out': out['m_mix_w_out'], 'm_ln2_g': out['m_ln2_g'], 'm_ln2_b': out['m_ln2_b'], 'm_ffn2_w_gate': out['m_ffn2_w_gate'], 'm_ffn2_w_up': out['m_ffn2_w_up'], 'm_ffn2_w_down': out['m_ffn2_w_down'], 'm_ln3_g': out['m_ln3_g'], 'm_ln3_b': out['m_ln3_b'], 'v_ln1_g': out['v_ln1_g'], 'v_ln1_b': out['v_ln1_b'], 'v_ffn1_w_gate': out['v_ffn1_w_gate'], 'v_ffn1_w_up': out['v_ffn1_w_up'], 'v_ffn1_w_down': out['v_ffn1_w_down'], 'v_mix_w_in': out['v_mix_w_in'], 'v_pool_w': out['v_pool_w'], 'v_pool_scale': out['v_pool_scale'], 'v_sconv_w': out['v_sconv_w'], 'v_cconv_w': out['v_cconv_w'], 'v_cconv_b': out['v_cconv_b'], 'v_cnorm_g': out['v_cnorm_g'], 'v_cnorm_b': out['v_cnorm_b'], 'v_mix_w_out': out['v_mix_w_out'], 'v_ln2_g': out['v_ln2_g'], 'v_ln2_b': out['v_ln2_b'], 'v_ffn2_w_gate': out['v_ffn2_w_gate'], 'v_ffn2_w_up': out['v_ffn2_w_up'], 'v_ffn2_w_down': out['v_ffn2_w_down'], 'v_ln3_g': out['v_ln3_g'], 'v_ln3_b': out['v_ln3_b']}


def _loss(weights, diff, rest, loss_target):
    with _jax.named_scope("forward"):
        args = {**rest, TWIN_DIFF_INPUT: diff, **{k: w.astype(_WEIGHT_DTYPES[k]) for k, w in weights.items()}}
        y = _forward(args)
    with _jax.named_scope("loss_head"):
        err = _jnp.square(y.astype(_jnp.float32) - loss_target)
        return 0.5 * _jnp.sum(_jnp.mean(err, axis=-1)) if err.ndim else 0.5 * err


def _adamw(w, g, m, v):
    m = ADAM_B1 * m + (1.0 - ADAM_B1) * g
    v = ADAM_B2 * v + (1.0 - ADAM_B2) * _jnp.square(g)
    m_hat = m / (1.0 - ADAM_B1 ** ADAM_STEP)
    v_hat = v / (1.0 - ADAM_B2 ** ADAM_STEP)
    delta = -ADAM_LR * (m_hat / (_jnp.sqrt(v_hat) + ADAM_EPS) + ADAM_WD * w)
    return delta, m, v


def reference(x, ln1_g, ln1_b, ffn1_w_gate, ffn1_w_up, ffn1_w_down, mix_w_in, pool_w, pool_scale, sconv_w, cconv_w, cconv_b, cnorm_g, cnorm_b, mix_w_out, ln2_g, ln2_b, ffn2_w_gate, ffn2_w_up, ffn2_w_down, ln3_g, ln3_b, loss_target, m_ln1_g, m_ln1_b, m_ffn1_w_gate, m_ffn1_w_up, m_ffn1_w_down, m_mix_w_in, m_pool_w, m_pool_scale, m_sconv_w, m_cconv_w, m_cconv_b, m_cnorm_g, m_cnorm_b, m_mix_w_out, m_ln2_g, m_ln2_b, m_ffn2_w_gate, m_ffn2_w_up, m_ffn2_w_down, m_ln3_g, m_ln3_b, v_ln1_g, v_ln1_b, v_ffn1_w_gate, v_ffn1_w_up, v_ffn1_w_down, v_mix_w_in, v_pool_w, v_pool_scale, v_sconv_w, v_cconv_w, v_cconv_b, v_cnorm_g, v_cnorm_b, v_mix_w_out, v_ln2_g, v_ln2_b, v_ffn2_w_gate, v_ffn2_w_up, v_ffn2_w_down, v_ln3_g, v_ln3_b):
    given = dict(x=x, ln1_g=ln1_g, ln1_b=ln1_b, ffn1_w_gate=ffn1_w_gate, ffn1_w_up=ffn1_w_up, ffn1_w_down=ffn1_w_down, mix_w_in=mix_w_in, pool_w=pool_w, pool_scale=pool_scale, sconv_w=sconv_w, cconv_w=cconv_w, cconv_b=cconv_b, cnorm_g=cnorm_g, cnorm_b=cnorm_b, mix_w_out=mix_w_out, ln2_g=ln2_g, ln2_b=ln2_b, ffn2_w_gate=ffn2_w_gate, ffn2_w_up=ffn2_w_up, ffn2_w_down=ffn2_w_down, ln3_g=ln3_g, ln3_b=ln3_b, loss_target=loss_target, m_ln1_g=m_ln1_g, m_ln1_b=m_ln1_b, m_ffn1_w_gate=m_ffn1_w_gate, m_ffn1_w_up=m_ffn1_w_up, m_ffn1_w_down=m_ffn1_w_down, m_mix_w_in=m_mix_w_in, m_pool_w=m_pool_w, m_pool_scale=m_pool_scale, m_sconv_w=m_sconv_w, m_cconv_w=m_cconv_w, m_cconv_b=m_cconv_b, m_cnorm_g=m_cnorm_g, m_cnorm_b=m_cnorm_b, m_mix_w_out=m_mix_w_out, m_ln2_g=m_ln2_g, m_ln2_b=m_ln2_b, m_ffn2_w_gate=m_ffn2_w_gate, m_ffn2_w_up=m_ffn2_w_up, m_ffn2_w_down=m_ffn2_w_down, m_ln3_g=m_ln3_g, m_ln3_b=m_ln3_b, v_ln1_g=v_ln1_g, v_ln1_b=v_ln1_b, v_ffn1_w_gate=v_ffn1_w_gate, v_ffn1_w_up=v_ffn1_w_up, v_ffn1_w_down=v_ffn1_w_down, v_mix_w_in=v_mix_w_in, v_pool_w=v_pool_w, v_pool_scale=v_pool_scale, v_sconv_w=v_sconv_w, v_cconv_w=v_cconv_w, v_cconv_b=v_cconv_b, v_cnorm_g=v_cnorm_g, v_cnorm_b=v_cnorm_b, v_mix_w_out=v_mix_w_out, v_ln2_g=v_ln2_g, v_ln2_b=v_ln2_b, v_ffn2_w_gate=v_ffn2_w_gate, v_ffn2_w_up=v_ffn2_w_up, v_ffn2_w_down=v_ffn2_w_down, v_ln3_g=v_ln3_g, v_ln3_b=v_ln3_b)
    weights = {n: given[n] for n in TWIN_WEIGHTS}
    shared = {n: given[n] for n in SHARED_INPUTS}
    per_example = {n: given[n] for n in ['x']}
    grad_fn = _jax.value_and_grad(_loss, argnums=(0, 1))

    def one_microbatch(ex, loss_target):
        ex = dict(ex)
        diff = ex.pop(TWIN_DIFF_INPUT)
        return grad_fn(weights, diff, {**shared, **ex}, loss_target)

    if N_MICROBATCH == 1:
        loss, (grad_w, grad_x) = one_microbatch(per_example, given["loss_target"])
    else:
        def body(carry, xs):
            loss_sum, grad_sum = carry
            l_k, (gw_k, gx_k) = one_microbatch(xs[0], xs[1])
            with _jax.named_scope("update"):
                return (loss_sum + l_k, _jax.tree.map(_jnp.add, grad_sum, gw_k)), gx_k

        init = (_jnp.zeros((), _jnp.float32), _jax.tree.map(_jnp.zeros_like, weights))
        (loss, grad_w), grad_x = _jax.lax.scan(body, init, (per_example, given["loss_target"]))
    with _jax.named_scope("update"):
        delta_w, new_m, new_v = {}, {}, {}
        for n in TWIN_WEIGHTS:
            delta_w[n], new_m[n], new_v[n] = _adamw(weights[n], grad_w[n], given["m_" + n], given["v_" + n])
    return (loss, grad_x, *[grad_w[n] for n in TWIN_WEIGHTS], *[delta_w[n] for n in TWIN_WEIGHTS],
            *[new_m[n] for n in TWIN_WEIGHTS], *[new_v[n] for n in TWIN_WEIGHTS])
```

```python
import jax
import jax.numpy as jnp
from jax import lax
from jax.experimental import pallas as pl
from jax.experimental.pallas import tpu as pltpu

F32 = jnp.float32
BF16 = jnp.bfloat16

DEPTH = 4
ALPHA = (2.0 * DEPTH) ** 0.25
LN_EPS = 1e-5
POOL_W = 256
CONV_W = 384
SCONV_K = 3
CCONV_K = 31
C_POOL = (0, 256)
C_GB = (256, 640)
C_GC = (640, 1024)
C_V = (1024, 1408)
C_CV = (1408, 1792)
C_CG = (1792, 2176)

ADAM_LR = 0.001
ADAM_B1 = 0.9
ADAM_B2 = 0.999
ADAM_EPS = 1e-08
ADAM_WD = 0.01
ADAM_STEP = 10

N_CHIPS = 4
N_DEV = 8
MESH = pl.DeviceIdType.MESH

TM_FWD = 512
TM_BWD = 256
TM_MIX = 256
HALO = 32
SMALL_ROWS = 256
VMEM_LIMIT = 56 * 1024 * 1024


def _cparams():
    return pltpu.CompilerParams(vmem_limit_bytes=VMEM_LIMIT)


def _sigmoid(v):
    return 1.0 / (1.0 + jnp.exp(-v))


def _dot(a, b):
    return jnp.dot(a, b, preferred_element_type=F32)


def _dot_tn(a, b):
    return lax.dot_general(a, b, (((0,), (0,)), ((), ())), preferred_element_type=F32)


def _ln_fwd(z, g, b):
    mu = jnp.mean(z, axis=-1, keepdims=True)
    zc = z - mu
    var = jnp.mean(zc * zc, axis=-1, keepdims=True)
    return zc * lax.rsqrt(var + LN_EPS) * g + b


def _ln_bwd(dy, z, g):
    mu = jnp.mean(z, axis=-1, keepdims=True)
    zc = z - mu
    var = jnp.mean(zc * zc, axis=-1, keepdims=True)
    rstd = lax.rsqrt(var + LN_EPS)
    xhat = zc * rstd
    dxh = dy * g
    m1 = jnp.mean(dxh, axis=-1, keepdims=True)
    m2 = jnp.mean(dxh * xhat, axis=-1, keepdims=True)
    return rstd * (dxh - m1 - xhat * m2), xhat


def _tile(t, tm):
    tm = min(tm, t)
    assert t % tm == 0, (t, tm)
    return tm


def ffn_fwd(x, wg, wu, wd, ln_g, ln_b):
    t, d = x.shape
    s_n, _, fs = wg.shape
    tm = _tile(t, TM_FWD)

    def body(x_ref, wg_ref, wu_ref, wd_ref, g_ref, b_ref, y_ref, z_ref, acc_ref):
        s = pl.program_id(1)
        xb = x_ref[...].astype(BF16)
        g = _dot(xb, wg_ref[0])
        u = _dot(xb, wu_ref[0])
        a = (g * _sigmoid(g) * u).astype(BF16)
        part = _dot(a, wd_ref[0])

        @pl.when(s == 0)
        def _():
            acc_ref[...] = part

        @pl.when(s > 0)
        def _():
            acc_ref[...] += part

        @pl.when(s == s_n - 1)
        def _():
            z = ALPHA * x_ref[...] + 0.5 * acc_ref[...]
            z_ref[...] = z
            y_ref[...] = _ln_fwd(z, g_ref[...], b_ref[...])

    return pl.pallas_call(
        body, name="ffn_fwd",
        grid=(t // tm, s_n),
        in_specs=[
            pl.BlockSpec((tm, d), lambda i, s: (i, 0)),
            pl.BlockSpec((1, d, fs), lambda i, s: (s, 0, 0)),
            pl.BlockSpec((1, d, fs), lambda i, s: (s, 0, 0)),
            pl.BlockSpec((1, fs, d), lambda i, s: (s, 0, 0)),
            pl.BlockSpec((1, d), lambda i, s: (0, 0)),
            pl.BlockSpec((1, d), lambda i, s: (0, 0)),
        ],
        out_specs=[pl.BlockSpec((tm, d), lambda i, s: (i, 0)),
                   pl.BlockSpec((tm, d), lambda i, s: (i, 0))],
        out_shape=[jax.ShapeDtypeStruct((t, d), F32), jax.ShapeDtypeStruct((t, d), F32)],
        scratch_shapes=[pltpu.VMEM((tm, d), F32)],
        compiler_params=_cparams(),
    )(x, wg, wu, wd, ln_g, ln_b)


def ffn_bwd(x, dzb, wg, wu, wd_t, wg_t, wu_t):
    t, d = x.shape
    s_n, _, fs = wg.shape
    tm = _tile(t, TM_BWD)
    n_i = t // tm

    def body(x_ref, dzb_ref, wg_ref, wu_ref, wdt_ref, wgt_ref, wut_ref,
             dx_ref, dwg_ref, dwu_ref, dwd_ref, accg, accu, accd):
        i = pl.program_id(1)
        xb = x_ref[...].astype(BF16)
        g = _dot(xb, wg_ref[0])
        u = _dot(xb, wu_ref[0])
        sg = _sigmoid(g)
        si = g * sg
        a = (si * u).astype(BF16)
        dfb = dzb_ref[...] * 0.5
        da = _dot(dfb, wdt_ref[0])
        dgate = (da * u * (sg * (1.0 + g * (1.0 - sg)))).astype(BF16)
        dup = (da * si).astype(BF16)
        dx_ref[0] = _dot(dgate, wgt_ref[0]) + _dot(dup, wut_ref[0])
        cg = _dot_tn(xb, dgate)
        cu = _dot_tn(xb, dup)
        cd = _dot_tn(a, dfb)

        @pl.when(i == 0)
        def _():
            accg[...] = cg
            accu[...] = cu
            accd[...] = cd

        @pl.when(i > 0)
        def _():
            accg[...] += cg
            accu[...] += cu
            accd[...] += cd

        @pl.when(i == n_i - 1)
        def _():
            dwg_ref[0] = accg[...].astype(BF16)
            dwu_ref[0] = accu[...].astype(BF16)
            dwd_ref[0] = accd[...].astype(BF16)

    tok = lambda s, i: (i, 0)
    shard = lambda s, i: (s, 0, 0)
    return pl.pallas_call(
        body, name="ffn_bwd",
        grid=(s_n, n_i),
        in_specs=[
            pl.BlockSpec((tm, d), tok), pl.BlockSpec((tm, d), tok),
            pl.BlockSpec((1, d, fs), shard), pl.BlockSpec((1, d, fs), shard), pl.BlockSpec((1, d, fs), shard),
            pl.BlockSpec((1, fs, d), shard), pl.BlockSpec((1, fs, d), shard),
        ],
        out_specs=[pl.BlockSpec((1, tm, d), lambda s, i: (s, i, 0)),
                   pl.BlockSpec((1, d, fs), shard), pl.BlockSpec((1, d, fs), shard),
                   pl.BlockSpec((1, fs, d), shard)],
        out_shape=[jax.ShapeDtypeStruct((s_n, t, d), F32),
                   jax.ShapeDtypeStruct((s_n, d, fs), BF16), jax.ShapeDtypeStruct((s_n, d, fs), BF16),
                   jax.ShapeDtypeStruct((s_n, fs, d), BF16)],
        scratch_shapes=[pltpu.VMEM((d, fs), F32), pltpu.VMEM((d, fs), F32), pltpu.VMEM((fs, d), F32)],
        compiler_params=_cparams(),
    )(x, dzb, wg, wu, wd_t, wg_t, wu_t)


def _sum_parts(base_ref, parts_ref):
    v = base_ref[...]
    if parts_ref is not None:
        for p in range(parts_ref.shape[0]):
            v = v + parts_ref[p]
    return v


def ln_bwd(dy, parts, z, ln_g):
    t, d = dy.shape
    tm = _tile(t, TM_FWD)
    n_p = 0 if parts is None else parts.shape[0]

    def body(*refs):
        if parts is None:
            dy_ref, z_ref, g_ref, dzb_ref, dxr_ref, dg_ref, db_ref = refs
            parts_ref = None
        else:
            dy_ref, parts_ref, z_ref, g_ref, dzb_ref, dxr_ref, dg_ref, db_ref = refs
        i = pl.program_id(0)
        dy_v = _sum_parts(dy_ref, parts_ref)
        dz, xhat = _ln_bwd(dy_v, z_ref[...], g_ref[...])
        dzb_ref[...] = dz.astype(BF16)
        dxr_ref[...] = ALPHA * dz

        @pl.when(i == 0)
        def _():
            dg_ref[...] = jnp.zeros_like(dg_ref)
            db_ref[...] = jnp.zeros_like(db_ref)

        dg_ref[...] += jnp.sum(dy_v * xhat, axis=0, keepdims=True)
        db_ref[...] += jnp.sum(dy_v, axis=0, keepdims=True)

    tok = lambda i: (i, 0)
    one = lambda i: (0, 0)
    in_specs = [pl.BlockSpec((tm, d), tok)]
    args = [dy]
    if parts is not None:
        in_specs.append(pl.BlockSpec((n_p, tm, d), lambda i: (0, i, 0)))
        args.append(parts)
    return pl.pallas_call(
        body, name="ln_bwd" if parts is None else "ln_bwd_parts",
        grid=(t // tm,),
        in_specs=in_specs + [pl.BlockSpec((tm, d), tok), pl.BlockSpec((1, d), one)],
        out_specs=[pl.BlockSpec((tm, d), tok), pl.BlockSpec((tm, d), tok),
                   pl.BlockSpec((1, d), one), pl.BlockSpec((1, d), one)],
        out_shape=[jax.ShapeDtypeStruct((t, d), BF16), jax.ShapeDtypeStruct((t, d), F32),
                   jax.ShapeDtypeStruct((1, d), F32), jax.ShapeDtypeStruct((1, d), F32)],
        compiler_params=_cparams(),
    )(*args, z, ln_g)


def add_parts(base, parts):
    t, d = base.shape
    tm = _tile(t, TM_FWD)

    def body(b_ref, p_ref, o_ref):
        o_ref[...] = _sum_parts(b_ref, p_ref)

    tok = lambda i: (i, 0)
    return pl.pallas_call(
        body, name="add_parts",
        grid=(t // tm,),
        in_specs=[pl.BlockSpec((tm, d), tok), pl.BlockSpec((parts.shape[0], tm, d), lambda i: (0, i, 0))],
        out_specs=pl.BlockSpec((tm, d), tok),
        out_shape=jax.ShapeDtypeStruct((t, d), F32),
        compiler_params=_cparams(),
    )(base, parts)


def loss_and_grad(y, target):
    t, d = y.shape
    tm = _tile(t, TM_FWD)

    def body(y_ref, t_ref, dy_ref, l_ref):
        i = pl.program_id(0)
        e = y_ref[...] - t_ref[...]
        dy_ref[...] = e * (1.0 / d)

        @pl.when(i == 0)
        def _():
            l_ref[...] = jnp.zeros_like(l_ref)

        l_ref[...] += (0.5 / d) * jnp.sum(e * e)

    tok = lambda i: (i, 0)
    return pl.pallas_call(
        body, name="loss",
        grid=(t // tm,),
        in_specs=[pl.BlockSpec((tm, d), tok), pl.BlockSpec((tm, d), tok)],
        out_specs=[pl.BlockSpec((tm, d), tok), pl.BlockSpec((8, 128), lambda i: (0, 0))],
        out_shape=[jax.ShapeDtypeStruct((t, d), F32), jax.ShapeDtypeStruct((8, 128), F32)],
        compiler_params=_cparams(),
    )(y, target)


def matmul(a, w, add=None, out_dtype=F32, name="matmul"):
    t, k = a.shape
    n = w.shape[1]
    tm = _tile(t, TM_FWD)

    def body(*refs):
        if add is None:
            a_ref, w_ref, o_ref = refs
        else:
            a_ref, w_ref, add_ref, o_ref = refs
        o = _dot(a_ref[...].astype(BF16), w_ref[...])
        if add is not None:
            o = o + add_ref[...]
        o_ref[...] = o.astype(out_dtype)

    tok = lambda i: (i, 0)
    in_specs = [pl.BlockSpec((tm, k), tok), pl.BlockSpec((k, n), lambda i: (0, 0))]
    args = [a, w]
    if add is not None:
        in_specs.append(pl.BlockSpec((tm, n), tok))
        args.append(add)
    return pl.pallas_call(
        body, name=name,
        grid=(t // tm,),
        in_specs=in_specs,
        out_specs=pl.BlockSpec((tm, n), tok),
        out_shape=jax.ShapeDtypeStruct((t, n), out_dtype),
        compiler_params=_cparams(),
    )(*args)


def matmul_tn(a, b, name="matmul_tn"):
    t, k = a.shape
    n = b.shape[1]
    tm = _tile(t, TM_FWD)

    def body(a_ref, b_ref, o_ref):
        i = pl.program_id(0)

        @pl.when(i == 0)
        def _():
            o_ref[...] = jnp.zeros_like(o_ref)

        o_ref[...] += _dot_tn(a_ref[...].astype(BF16), b_ref[...].astype(BF16))

    tok = lambda i: (i, 0)
    return pl.pallas_call(
        body, name=name,
        grid=(t // tm,),
        in_specs=[pl.BlockSpec((tm, k), tok), pl.BlockSpec((tm, n), tok)],
        out_specs=pl.BlockSpec((k, n), lambda i: (0, 0)),
        out_shape=jax.ShapeDtypeStruct((k, n), F32),
        compiler_params=_cparams(),
    )(a, b)


def out_proj_ln(ycat, w_out, h, ln_g, ln_b):
    t, d = h.shape
    k = ycat.shape[1]
    tm = _tile(t, TM_FWD)

    def body(yc_ref, w_ref, h_ref, g_ref, b_ref, y_ref, z_ref):
        z = ALPHA * h_ref[...] + _dot(yc_ref[...], w_ref[...])
        z_ref[...] = z
        y_ref[...] = _ln_fwd(z, g_ref[...], b_ref[...])

    tok = lambda i: (i, 0)
    one = lambda i: (0, 0)
    return pl.pallas_call(
        body, name="out_proj_ln",
        grid=(t // tm,),
        in_specs=[pl.BlockSpec((tm, k), tok), pl.BlockSpec((k, d), one), pl.BlockSpec((tm, d), tok),
                  pl.BlockSpec((1, d), one), pl.BlockSpec((1, d), one)],
        out_specs=[pl.BlockSpec((tm, d), tok), pl.BlockSpec((tm, d), tok)],
        out_shape=[jax.ShapeDtypeStruct((t, d), F32), jax.ShapeDtypeStruct((t, d), F32)],
        compiler_params=_cparams(),
    )(ycat, w_out, h, ln_g, ln_b)


def _halo_specs(tm, cols, n_rows):
    r = tm // HALO
    last = n_rows // HALO - 1
    return [pl.BlockSpec((HALO, cols), lambda i: (jnp.maximum(i * r - 1, 0), 0)),
            pl.BlockSpec((tm, cols), lambda i: (i, 0)),
            pl.BlockSpec((HALO, cols), lambda i: (jnp.minimum((i + 1) * r, last), 0))]


def _fill_ext(dst, prev_ref, main_ref, next_ref, i, n_i):
    tm = main_ref.shape[0]
    dst[0:HALO, :] = jnp.where(i > 0, prev_ref[...], 0.0)
    dst[HALO:HALO + tm, :] = main_ref[...]
    dst[HALO + tm:HALO + tm + HALO, :] = jnp.where(i < n_i - 1, next_ref[...], 0.0)


def _pool_lane_half():
    lane = lax.broadcasted_iota(jnp.int32, (1, POOL_W), 1)
    return jnp.left_shift(1, lane // 64)


def _pool_inv_count(t0, rows, seq):
    half = _pool_lane_half()
    tpos = t0 + lax.broadcasted_iota(jnp.int32, (rows, 1), 0)
    lo = jnp.maximum(tpos - half, 0)
    hi = jnp.minimum(tpos + half, seq)
    cnt = jnp.maximum(hi - lo, 1)
    return 1.0 / cnt.astype(F32)


def _pool_forward(p_ext, tm, t0, seq):
    half = _pool_lane_half()
    total = jnp.zeros((tm, POOL_W), F32)
    for o in range(-8, 8):
        m = ((o >= -half) & (o < half)).astype(F32)
        total = total + m * p_ext[HALO + o:HALO + o + tm, C_POOL[0]:C_POOL[1]]
    u = p_ext[HALO:HALO + tm, C_POOL[0]:C_POOL[1]]
    return total * _pool_inv_count(t0, tm, seq) - u


def mix_fwd(proj, pw_bd, pool_scale, sconv_w, cconv_w, cconv_b, cnorm_g, cnorm_b):
    t, pc = proj.shape
    tm = _tile(t, TM_MIX)
    n_i = t // tm
    e = tm + 2 * HALO

    def body(pp_ref, pm_ref, pn_ref, pw_ref, ps_ref, sw_ref, cw_ref, cb_ref, cg_ref, cbb_ref,
             yc_ref, a1_ref, p_ext, q_ext, a0_ext):
        i = pl.program_id(0)
        _fill_ext(p_ext, pp_ref, pm_ref, pn_ref, i, n_i)
        pooled = _pool_forward(p_ext, tm, i * tm, t)
        y_a = _dot(pooled.astype(BF16), pw_ref[...]) * ps_ref[...]
        yc_ref[:, 0:256] = y_a.astype(BF16)
        q_ext[...] = p_ext[:, C_GC[0]:C_GC[1]] * p_ext[:, C_V[0]:C_V[1]]
        conv = jnp.zeros((tm, CONV_W), F32)
        for k in range(SCONV_K):
            conv = conv + sw_ref[k:k + 1, :] * q_ext[HALO + k - 1:HALO + k - 1 + tm, :]
        y_b = p_ext[HALO:HALO + tm, C_GB[0]:C_GB[1]] * conv
        yc_ref[:, 256:640] = y_b.astype(BF16)
        a0_ext[...] = p_ext[:, C_CV[0]:C_CV[1]] * _sigmoid(p_ext[:, C_CG[0]:C_CG[1]])
        a1 = jnp.zeros((tm, CONV_W), F32) + cb_ref[...]
        for k in range(CCONV_K):
            a1 = a1 + cw_ref[k:k + 1, :] * a0_ext[HALO + k - 15:HALO + k - 15 + tm, :]
        a1_ref[...] = a1
        ln = _ln_fwd(a1, cg_ref[...], cbb_ref[...])
        yc_ref[:, 640:1024] = (ln * _sigmoid(ln)).astype(BF16)

    one = lambda i: (0, 0)
    return pl.pallas_call(
        body, name="mix_fwd",
        grid=(n_i,),
        in_specs=_halo_specs(tm, pc, t) + [
            pl.BlockSpec((POOL_W, POOL_W), one), pl.BlockSpec((1, POOL_W), one),
            pl.BlockSpec((8, CONV_W), one), pl.BlockSpec((32, CONV_W), one),
            pl.BlockSpec((1, CONV_W), one), pl.BlockSpec((1, CONV_W), one), pl.BlockSpec((1, CONV_W), one)],
        out_specs=[pl.BlockSpec((tm, 1024), lambda i: (i, 0)), pl.BlockSpec((tm, CONV_W), lambda i: (i, 0))],
        out_shape=[jax.ShapeDtypeStruct((t, 1024), BF16), jax.ShapeDtypeStruct((t, CONV_W), F32)],
        scratch_shapes=[pltpu.VMEM((e, pc), F32), pltpu.VMEM((e, CONV_W), F32), pltpu.VMEM((e, CONV_W), F32)],
        compiler_params=_cparams(),
    )(proj, proj, proj, pw_bd, pool_scale, sconv_w, cconv_w, cconv_b, cnorm_g, cnorm_b)


def mix_bwd(proj, dycat, a1, pw_bd, pw_bd_t, pool_scale, sconv_w, cconv_w, cnorm_g, cnorm_b):
    t, pc = proj.shape
    tm = _tile(t, TM_MIX)
    n_i = t // tm
    e = tm + 2 * HALO

    def body(pp_ref, pm_ref, pn_ref, dp_ref, dm_ref, dn_ref, ap_ref, am_ref, an_ref,
             pw_ref, pwt_ref, ps_ref, sw_ref, cw_ref, cg_ref, cbb_ref,
             dproj_ref, dpw_ref, dps_ref, dsw_ref, dcw_ref, dcb_ref, dcg_ref, dcbb_ref,
             p_ext, dy_ext, a1_ext, a0_ext, da1_ext, q_ext, dc_ext, dpn_ext):
        i = pl.program_id(0)
        main = slice(HALO, HALO + tm)

        @pl.when(i == 0)
        def _():
            for r in (dpw_ref, dps_ref, dsw_ref, dcw_ref, dcb_ref, dcg_ref, dcbb_ref):
                r[...] = jnp.zeros_like(r)

        _fill_ext(p_ext, pp_ref, pm_ref, pn_ref, i, n_i)
        _fill_ext(dy_ext, dp_ref, dm_ref, dn_ref, i, n_i)
        _fill_ext(a1_ext, ap_ref, am_ref, an_ref, i, n_i)

        sig_cg = _sigmoid(p_ext[:, C_CG[0]:C_CG[1]])
        a0_ext[...] = p_ext[:, C_CV[0]:C_CV[1]] * sig_cg
        a1_v = a1_ext[...]
        mu = jnp.mean(a1_v, axis=-1, keepdims=True)
        zc = a1_v - mu
        var = jnp.mean(zc * zc, axis=-1, keepdims=True)
        rstd = lax.rsqrt(var + LN_EPS)
        xhat = zc * rstd
        ln = xhat * cg_ref[...] + cbb_ref[...]
        sl = _sigmoid(ln)
        dln = dy_ext[:, 640:1024] * (sl * (1.0 + ln * (1.0 - sl)))
        dcg_ref[...] += jnp.sum((dln * xhat)[main], axis=0, keepdims=True)
        dcbb_ref[...] += jnp.sum(dln[main], axis=0, keepdims=True)
        dxh = dln * cg_ref[...]
        m1 = jnp.mean(dxh, axis=-1, keepdims=True)
        m2 = jnp.mean(dxh * xhat, axis=-1, keepdims=True)
        da1 = rstd * (dxh - m1 - xhat * m2)
        da1_ext[...] = da1
        da1_m = da1[main]
        dcb_ref[...] += jnp.sum(da1_m, axis=0, keepdims=True)
        da0 = jnp.zeros((tm, CONV_W), F32)
        for k in range(CCONV_K):
            da0 = da0 + cw_ref[k:k + 1, :] * da1_ext[HALO - k + 15:HALO - k + 15 + tm, :]
            dcw_ref[k:k + 1, :] += jnp.sum(da1_m * a0_ext[HALO + k - 15:HALO + k - 15 + tm, :],
                                           axis=0, keepdims=True)
        sig_m = sig_cg[main]
        cv_m = p_ext[main, C_CV[0]:C_CV[1]]
        dproj_ref[:, C_CV[0]:C_CV[1]] = (da0 * sig_m).astype(BF16)
        dproj_ref[:, C_CG[0]:C_CG[1]] = (da0 * cv_m * sig_m * (1.0 - sig_m)).astype(BF16)

        q_ext[...] = p_ext[:, C_GC[0]:C_GC[1]] * p_ext[:, C_V[0]:C_V[1]]
        dc_ext[...] = dy_ext[:, 256:640] * p_ext[:, C_GB[0]:C_GB[1]]
        dc_m = dc_ext[main, :]
        conv = jnp.zeros((tm, CONV_W), F32)
        dq = jnp.zeros((tm, CONV_W), F32)
        for k in range(SCONV_K):
            q_k = q_ext[HALO + k - 1:HALO + k - 1 + tm, :]
            conv = conv + sw_ref[k:k + 1, :] * q_k
            dq = dq + sw_ref[k:k + 1, :] * dc_ext[HALO - k + 1:HALO - k + 1 + tm, :]
            dsw_ref[k:k + 1, :] += jnp.sum(dc_m * q_k, axis=0, keepdims=True)
        dproj_ref[:, C_GB[0]:C_GB[1]] = (dy_ext[main, 256:640] * conv).astype(BF16)
        dproj_ref[:, C_GC[0]:C_GC[1]] = (dq * p_ext[main, C_V[0]:C_V[1]]).astype(BF16)
        dproj_ref[:, C_V[0]:C_V[1]] = (dq * p_ext[main, C_GC[0]:C_GC[1]]).astype(BF16)

        t0 = i * tm
        dya = dy_ext[:, 0:256] * ps_ref[...]
        dpooled = _dot(dya.astype(BF16), pwt_ref[...])
        dpn_ext[...] = dpooled * _pool_inv_count(t0 - HALO, e, t)
        half = _pool_lane_half()
        du = jnp.zeros((tm, POOL_W), F32)
        for o in range(-7, 9):
            m = ((o > -half) & (o <= half)).astype(F32)
            du = du + m * dpn_ext[HALO + o:HALO + o + tm, :]
        dproj_ref[:, C_POOL[0]:C_POOL[1]] = (du - dpooled[main]).astype(BF16)
        pooled = _pool_forward(p_ext, tm, t0, t)
        pooled_b = pooled.astype(BF16)
        ya_pre = _dot(pooled_b, pw_ref[...])
        dps_ref[...] += jnp.sum(dy_ext[main, 0:256] * ya_pre, axis=0, keepdims=True)
        dpw_ref[...] += _dot_tn(pooled_b, dya[main].astype(BF16))

    one = lambda i: (0, 0)
    small = [((POOL_W, POOL_W), F32), ((1, POOL_W), F32), ((8, CONV_W), F32), ((32, CONV_W), F32),
             ((1, CONV_W), F32), ((1, CONV_W), F32), ((1, CONV_W), F32)]
    return pl.pallas_call(
        body, name="mix_bwd",
        grid=(n_i,),
        in_specs=_halo_specs(tm, pc, t) + _halo_specs(tm, 1024, t) + _halo_specs(tm, CONV_W, t) + [
            pl.BlockSpec((POOL_W, POOL_W), one), pl.BlockSpec((POOL_W, POOL_W), one), pl.BlockSpec((1, POOL_W), one),
            pl.BlockSpec((8, CONV_W), one), pl.BlockSpec((32, CONV_W), one),
            pl.BlockSpec((1, CONV_W), one), pl.BlockSpec((1, CONV_W), one)],
        out_specs=[pl.BlockSpec((tm, pc), lambda i: (i, 0))] + [pl.BlockSpec(s, one) for s, _ in small],
        out_shape=[jax.ShapeDtypeStruct((t, pc), BF16)] + [jax.ShapeDtypeStruct(s, dt) for s, dt in small],
        scratch_shapes=[pltpu.VMEM((e, pc), F32), pltpu.VMEM((e, 1024), F32), pltpu.VMEM((e, CONV_W), F32),
                        pltpu.VMEM((e, CONV_W), F32), pltpu.VMEM((e, CONV_W), F32), pltpu.VMEM((e, CONV_W), F32),
                        pltpu.VMEM((e, CONV_W), F32), pltpu.VMEM((e, POOL_W), F32)],
        compiler_params=_cparams(),
    )(proj, proj, proj, dycat, dycat, dycat, a1, a1, a1,
      pw_bd, pw_bd_t, pool_scale, sconv_w, cconv_w, cnorm_g, cnorm_b)


def _mesh_pos():
    return lax.axis_index("x"), lax.axis_index("y"), lax.axis_index("c")


def _flip(v, f):
    return 1 - v if f else v


def gather_chips(arrs):
    n = len(arrs)
    flips = [(1, 0), (0, 1), (1, 1)]

    def body(*refs):
        ins, outs = refs[:n], refs[n:2 * n]
        send_sems, recv_sems, loc_sems = refs[2 * n:]
        x, y, c = _mesh_pos()
        me = 2 * x + y
        local = []
        remote = []
        for a in range(n):
            cp = pltpu.make_async_copy(ins[a], outs[a].at[me], loc_sems.at[a])
            cp.start()
            local.append(cp)
            for k, (fx, fy) in enumerate(flips):
                px, py = _flip(x, fx), _flip(y, fy)
                rc = pltpu.make_async_remote_copy(
                    src_ref=ins[a], dst_ref=outs[a].at[me],
                    send_sem=send_sems.at[a * 3 + k], recv_sem=recv_sems.at[a * 3 + k],
                    device_id=(px, py, c), device_id_type=MESH)
                rc.start()
                remote.append(pltpu.make_async_remote_copy(
                    src_ref=ins[a], dst_ref=outs[a].at[2 * px + py],
                    send_sem=send_sems.at[a * 3 + k], recv_sem=recv_sems.at[a * 3 + k],
                    device_id=(px, py, c), device_id_type=MESH))
        for cp in local:
            cp.wait()
        for rc in remote:
            rc.wait()

    hbm = pl.BlockSpec(memory_space=pl.ANY)
    return pl.pallas_call(
        body, name="gather_chips",
        in_specs=[hbm] * n, out_specs=[hbm] * n,
        out_shape=[jax.ShapeDtypeStruct((N_CHIPS,) + a.shape, a.dtype) for a in arrs],
        scratch_shapes=[pltpu.SemaphoreType.DMA((3 * n,)), pltpu.SemaphoreType.DMA((3 * n,)),
                        pltpu.SemaphoreType.DMA((n,))],
    )(*arrs)


def exchange(arrs, per_chip):
    n = len(arrs)
    flips = [(fx, fy, fc) for fx in (0, 1) for fy in (0, 1) for fc in (0, 1)][1:]

    def body(*refs):
        ins, outs = refs[:n], refs[n:2 * n]
        send_sems, recv_sems, loc_sems = refs[2 * n:]
        x, y, c = _mesh_pos()
        me = 4 * x + 2 * y + c
        local = []
        remote = []
        for a in range(n):
            def part(px, py):
                return ins[a].at[2 * px + py] if per_chip else ins[a]

            cp = pltpu.make_async_copy(part(x, y), outs[a].at[me], loc_sems.at[a])
            cp.start()
            local.append(cp)
            for k, (fx, fy, fc) in enumerate(flips):
                px, py, pc = _flip(x, fx), _flip(y, fy), _flip(c, fc)
                sem = a * 7 + k
                rc = pltpu.make_async_remote_copy(
                    src_ref=part(px, py), dst_ref=outs[a].at[me],
                    send_sem=send_sems.at[sem], recv_sem=recv_sems.at[sem],
                    device_id=(px, py, pc), device_id_type=MESH)
                rc.start()
                remote.append(pltpu.make_async_remote_copy(
                    src_ref=part(px, py), dst_ref=outs[a].at[4 * px + 2 * py + pc],
                    send_sem=send_sems.at[sem], recv_sem=recv_sems.at[sem],
                    device_id=(px, py, pc), device_id_type=MESH))
        for cp in local:
            cp.wait()
        for rc in remote:
            rc.wait()

    hbm = pl.BlockSpec(memory_space=pl.ANY)
    shapes = [a.shape[1:] if per_chip else a.shape for a in arrs]
    return pl.pallas_call(
        body, name="exchange_per_chip" if per_chip else "exchange_all",
        in_specs=[hbm] * n, out_specs=[hbm] * n,
        out_shape=[jax.ShapeDtypeStruct((N_DEV,) + s, a.dtype) for s, a in zip(shapes, arrs)],
        scratch_shapes=[pltpu.SemaphoreType.DMA((7 * n,)), pltpu.SemaphoreType.DMA((7 * n,)),
                        pltpu.SemaphoreType.DMA((n,))],
    )(*arrs)


def adamw(parts, w, m, v, name):
    k_n, r, c = parts.shape
    tr = r
    for cand in (512, 256, 128, 64, 32, 16, 8):
        if r % cand == 0:
            tr = cand
            break

    def body(p_ref, w_ref, m_ref, v_ref, g_ref, d_ref, mo_ref, vo_ref):
        g = p_ref[0].astype(F32)
        for k in range(1, k_n):
            g = g + p_ref[k].astype(F32)
        m_new = ADAM_B1 * m_ref[...] + (1.0 - ADAM_B1) * g
        v_new = ADAM_B2 * v_ref[...] + (1.0 - ADAM_B2) * (g * g)
        m_hat = m_new / (1.0 - ADAM_B1 ** ADAM_STEP)
        v_hat = v_new / (1.0 - ADAM_B2 ** ADAM_STEP)
        g_ref[...] = g
        d_ref[...] = -ADAM_LR * (m_hat / (jnp.sqrt(v_hat) + ADAM_EPS) + ADAM_WD * w_ref[...])
        mo_ref[...] = m_new
        vo_ref[...] = v_new

    blk = pl.BlockSpec((tr, c), lambda i: (i, 0))
    return pl.pallas_call(
        body, name=name,
        grid=(r // tr,),
        in_specs=[pl.BlockSpec((k_n, tr, c), lambda i: (0, i, 0)), blk, blk, blk],
        out_specs=[blk, blk, blk, blk],
        out_shape=[jax.ShapeDtypeStruct((r, c), F32)] * 4,
        compiler_params=_cparams(),
    )(parts, w, m, v)


def _block_diag(pool_w):
    out = jnp.zeros((POOL_W, POOL_W), pool_w.dtype)
    for g in range(4):
        out = lax.dynamic_update_slice(out, pool_w[g], (64 * g, 64 * g))
    return out


def _pad_rows(a, rows):
    return jnp.pad(a, ((0, rows - a.shape[0]), (0, 0)))


def kernel(x, ln1_g, ln1_b, ffn1_w_gate, ffn1_w_up, ffn1_w_down, mix_w_in, pool_w, pool_scale, sconv_w, cconv_w, cconv_b, cnorm_g, cnorm_b, mix_w_out, ln2_g, ln2_b, ffn2_w_gate, ffn2_w_up, ffn2_w_down, ln3_g, ln3_b, loss_target, m_ln1_g, m_ln1_b, m_ffn1_w_gate, m_ffn1_w_up, m_ffn1_w_down, m_mix_w_in, m_pool_w, m_pool_scale, m_sconv_w, m_cconv_w, m_cconv_b, m_cnorm_g, m_cnorm_b, m_mix_w_out, m_ln2_g, m_ln2_b, m_ffn2_w_gate, m_ffn2_w_up, m_ffn2_w_down, m_ln3_g, m_ln3_b, v_ln1_g, v_ln1_b, v_ffn1_w_gate, v_ffn1_w_up, v_ffn1_w_down, v_mix_w_in, v_pool_w, v_pool_scale, v_sconv_w, v_cconv_w, v_cconv_b, v_cnorm_g, v_cnorm_b, v_mix_w_out, v_ln2_g, v_ln2_b, v_ffn2_w_gate, v_ffn2_w_up, v_ffn2_w_down, v_ln3_g, v_ln3_b):
    names = ['ln1_g', 'ln1_b', 'ffn1_w_gate', 'ffn1_w_up', 'ffn1_w_down', 'mix_w_in', 'pool_w', 'pool_scale',
             'sconv_w', 'cconv_w', 'cconv_b', 'cnorm_g', 'cnorm_b', 'mix_w_out', 'ln2_g', 'ln2_b',
             'ffn2_w_gate', 'ffn2_w_up', 'ffn2_w_down', 'ln3_g', 'ln3_b']
    w = dict(zip(names, (ln1_g, ln1_b, ffn1_w_gate, ffn1_w_up, ffn1_w_down, mix_w_in, pool_w, pool_scale, sconv_w,
                         cconv_w, cconv_b, cnorm_g, cnorm_b, mix_w_out, ln2_g, ln2_b, ffn2_w_gate, ffn2_w_up,
                         ffn2_w_down, ln3_g, ln3_b)))
    mom_m = dict(zip(names, (m_ln1_g, m_ln1_b, m_ffn1_w_gate, m_ffn1_w_up, m_ffn1_w_down, m_mix_w_in, m_pool_w,
                             m_pool_scale, m_sconv_w, m_cconv_w, m_cconv_b, m_cnorm_g, m_cnorm_b, m_mix_w_out,
                             m_ln2_g, m_ln2_b, m_ffn2_w_gate, m_ffn2_w_up, m_ffn2_w_down, m_ln3_g, m_ln3_b)))
    mom_v = dict(zip(names, (v_ln1_g, v_ln1_b, v_ffn1_w_gate, v_ffn1_w_up, v_ffn1_w_down, v_mix_w_in, v_pool_w,
                             v_pool_scale, v_sconv_w, v_cconv_w, v_cconv_b, v_cnorm_g, v_cnorm_b, v_mix_w_out,
                             v_ln2_g, v_ln2_b, v_ffn2_w_gate, v_ffn2_w_up, v_ffn2_w_down, v_ln3_g, v_ln3_b)))
    big = ['ffn1_w_gate', 'ffn1_w_up', 'ffn1_w_down', 'mix_w_in', 'mix_w_out',
           'ffn2_w_gate', 'ffn2_w_up', 'ffn2_w_down']
    n_l = ln1_g.shape[0]
    d = x.shape[-1]
    fs = ffn1_w_gate.shape[-1]
    ws_in = mix_w_in.shape[-1]
    cs = sconv_w.shape[-1]
    chip = 2 * lax.axis_index("x") + lax.axis_index("y")

    conv_loc = jnp.concatenate([sconv_w, cconv_w], axis=1)
    gathered = gather_chips([w[k].astype(BF16) for k in big] + [conv_loc])
    gw = dict(zip(big, gathered[:-1]))
    conv_all = jnp.transpose(gathered[-1], (1, 2, 0, 3)).reshape(n_l, SCONV_K + CCONV_K, N_CHIPS * cs)
    sconv_full = conv_all[:, :SCONV_K]
    cconv_full = conv_all[:, SCONV_K:]

    def cols(a):
        return jnp.transpose(a, (1, 0, 2)).reshape(a.shape[1], -1)

    h = x[0]
    target = loss_target[0]
    saved = []
    layer_w = []
    for l in range(n_l):
        lw = dict(
            g1=gw['ffn1_w_gate'][:, l], u1=gw['ffn1_w_up'][:, l], d1=gw['ffn1_w_down'][:, l],
            g2=gw['ffn2_w_gate'][:, l], u2=gw['ffn2_w_up'][:, l], d2=gw['ffn2_w_down'][:, l],
            w_in=cols(gw['mix_w_in'][:, l]),
            w_out=gw['mix_w_out'][:, l].reshape(-1, d),
            pw=_block_diag(pool_w[l]).astype(BF16),
            ps=pool_scale[l][None], sw=_pad_rows(sconv_full[l], 8), cw=_pad_rows(cconv_full[l], 32),
            cb=cconv_b[l][None], cg=cnorm_g[l][None], cbb=cnorm_b[l][None],
        )
        layer_w.append(lw)
        y1, z1 = ffn_fwd(h, lw['g1'], lw['u1'], lw['d1'], ln1_g[l][None], ln1_b[l][None])
        proj = matmul(y1, lw['w_in'], name="proj")
        ycat, a1 = mix_fwd(proj, lw['pw'], lw['ps'], lw['sw'], lw['cw'], lw['cb'], lw['cg'], lw['cbb'])
        y2, z2 = out_proj_ln(ycat, lw['w_out'], y1, ln2_g[l][None], ln2_b[l][None])
        y3, z3 = ffn_fwd(y2, lw['g2'], lw['u2'], lw['d2'], ln3_g[l][None], ln3_b[l][None])
        saved.append(dict(x0=h, z1=z1, y1=y1, proj=proj, ycat=ycat, a1=a1, z2=z2, y2=y2, z3=z3))
        h = y3

    dy, loss_blk = loss_and_grad(h, target)
    loss = lax.psum(loss_blk[0, 0], ("x", "y", "c"))

    g_loc = {k: [None] * n_l for k in names}
    sw_t = lambda a: jnp.swapaxes(a, 1, 2)
    parts = None
    for l in reversed(range(n_l)):
        lw, sv = layer_w[l], saved[l]
        dzb, dxr, g_loc['ln3_g'][l], g_loc['ln3_b'][l] = ln_bwd(dy, parts, sv['z3'], ln3_g[l][None])
        parts, g_loc['ffn2_w_gate'][l], g_loc['ffn2_w_up'][l], g_loc['ffn2_w_down'][l] = ffn_bwd(
            sv['y2'], dzb, lw['g2'], lw['u2'], sw_t(lw['d2']), sw_t(lw['g2']), sw_t(lw['u2']))
        dzb, dxr, g_loc['ln2_g'][l], g_loc['ln2_b'][l] = ln_bwd(dxr, parts, sv['z2'], ln2_g[l][None])
        dycat = matmul(dzb, lw['w_out'].T, name="dycat")
        g_loc['mix_w_out'][l] = matmul_tn(sv['ycat'], dzb, name="dw_out")
        (dproj, dpw, g_loc['pool_scale'][l], dsw, dcw, g_loc['cconv_b'][l], g_loc['cnorm_g'][l],
         g_loc['cnorm_b'][l]) = mix_bwd(sv['proj'], dycat, sv['a1'], lw['pw'], lw['pw'].T, lw['ps'],
                                        lw['sw'], lw['cw'], lw['cg'], lw['cbb'])
        g_loc['pool_w'][l] = jnp.stack([dpw[64 * g:64 * g + 64, 64 * g:64 * g + 64] for g in range(4)])
        g_loc['sconv_w'][l] = dsw[:SCONV_K]
        g_loc['cconv_w'][l] = dcw[:CCONV_K]
        dy = matmul(dproj, lw['w_in'].T, add=dxr, name="dmix_in")
        g_loc['mix_w_in'][l] = matmul_tn(sv['y1'], dproj, name="dw_in")
        dzb, dxr, g_loc['ln1_g'][l], g_loc['ln1_b'][l] = ln_bwd(dy, None, sv['z1'], ln1_g[l][None])
        parts, g_loc['ffn1_w_gate'][l], g_loc['ffn1_w_up'][l], g_loc['ffn1_w_down'][l] = ffn_bwd(
            sv['x0'], dzb, lw['g1'], lw['u1'], sw_t(lw['d1']), sw_t(lw['g1']), sw_t(lw['u1']))
        dy = dxr
    grad_x = add_parts(dy, parts)[None]

    def chip_major(k):
        if k in ('mix_w_in',):
            a = jnp.stack(g_loc[k])
            return jnp.transpose(a.reshape(n_l, d, N_CHIPS, ws_in), (2, 0, 1, 3)).astype(BF16)
        if k == 'mix_w_out':
            a = jnp.stack(g_loc[k])
            return jnp.transpose(a.reshape(n_l, N_CHIPS, -1, d), (1, 0, 2, 3)).astype(BF16)
        return jnp.stack(g_loc[k], axis=1)

    parts_big = dict(zip(big, exchange([chip_major(k) for k in big], per_chip=True)))

    small = [k for k in names if k not in big]
    small_full = {}
    for k in small:
        a = jnp.stack(g_loc[k])
        small_full[k] = a.reshape(n_l, -1) if a.shape[1] == 1 else a
    flat = jnp.concatenate([small_full[k].reshape(-1) for k in small])
    n_flat = flat.shape[0]
    rows = -(-n_flat // (SMALL_ROWS * 128)) * SMALL_ROWS
    flat = jnp.pad(flat, (0, rows * 128 - n_flat)).reshape(rows, 128)
    parts_small = exchange([flat], per_chip=False)[0]

    out_g, out_d, out_m, out_v = {}, {}, {}, {}
    for k in big:
        shp = w[k].shape
        r = shp[0] * shp[1]
        res = adamw(parts_big[k].reshape(N_DEV, r, shp[2]), w[k].reshape(r, shp[2]),
                    mom_m[k].reshape(r, shp[2]), mom_v[k].reshape(r, shp[2]), name="adamw_" + k)
        out_g[k], out_d[k], out_m[k], out_v[k] = [o.reshape(shp) for o in res]

    zeros = jnp.zeros((rows, 128), F32)
    g_sum = adamw(parts_small, zeros, zeros, zeros, name="sum_small")[0].reshape(-1)
    off = 0
    for k in small:
        full = small_full[k]
        g = g_sum[off:off + full.size].reshape(full.shape)
        off += full.size
        if k in ('sconv_w', 'cconv_w'):
            g = lax.dynamic_slice_in_dim(g, chip * cs, cs, axis=2)
        out_g[k] = g.reshape(w[k].shape)

    def pack(dct):
        f = jnp.concatenate([dct[k].reshape(-1) for k in small])
        r2 = -(-f.shape[0] // (SMALL_ROWS * 128)) * SMALL_ROWS
        return jnp.pad(f, (0, r2 * 128 - f.shape[0])).reshape(r2, 128), f.shape[0]

    gp, n_small = pack(out_g)
    wp, _ = pack(w)
    mp, _ = pack(mom_m)
    vp, _ = pack(mom_v)
    _, dp, mo, vo = adamw(gp[None], wp, mp, vp, name="adamw_small")
    off = 0
    for k in small:
        sz = w[k].size
        out_d[k] = dp.reshape(-1)[off:off + sz].reshape(w[k].shape)
        out_m[k] = mo.reshape(-1)[off:off + sz].reshape(w[k].shape)
        out_v[k] = vo.reshape(-1)[off:off + sz].reshape(w[k].shape)
        off += sz

    return (loss, grad_x, *[out_g[k] for k in names], *[out_d[k] for k in names],
            *[out_m[k] for k in names], *[out_v[k] for k in names])
```

```python
import jax
import jax.numpy as jnp
from jax import lax
from jax.experimental import pallas as pl
from jax.experimental.pallas import tpu as pltpu

F32 = jnp.float32
BF16 = jnp.bfloat16

DEPTH = 4
ALPHA = (2.0 * DEPTH) ** 0.25
LN_EPS = 1e-5
POOL_W = 256
CONV_W = 384
SCONV_K = 3
CCONV_K = 31
C_POOL = (0, 256)
C_GB = (256, 640)
C_GC = (640, 1024)
C_V = (1024, 1408)
C_CV = (1408, 1792)
C_CG = (1792, 2176)

ADAM_LR = 0.001
ADAM_B1 = 0.9
ADAM_B2 = 0.999
ADAM_EPS = 1e-08
ADAM_WD = 0.01
ADAM_STEP = 10

N_CHIPS = 4
N_DEV = 8
MESH = pl.DeviceIdType.MESH

TM_FWD = 512
TM_BWD = 512
TM_MIX = 256
HALO = 32
SMALL_ROWS = 256
VMEM_LIMIT = 56 * 1024 * 1024


def _cparams():
    return pltpu.CompilerParams(vmem_limit_bytes=VMEM_LIMIT)


def _sigmoid(v):
    return 1.0 / (1.0 + jnp.exp(-v))


def _dot(a, b):
    return jnp.dot(a, b, preferred_element_type=F32)


def _dot_nt(a, b):
    return lax.dot_general(a, b, (((1,), (1,)), ((), ())), preferred_element_type=F32)


def _dot_tn(a, b):
    return lax.dot_general(a, b, (((0,), (0,)), ((), ())), preferred_element_type=F32)


def _ln_fwd(z, g, b):
    mu = jnp.mean(z, axis=-1, keepdims=True)
    zc = z - mu
    var = jnp.mean(zc * zc, axis=-1, keepdims=True)
    return zc * lax.rsqrt(var + LN_EPS) * g + b


def _ln_bwd(dy, z, g):
    mu = jnp.mean(z, axis=-1, keepdims=True)
    zc = z - mu
    var = jnp.mean(zc * zc, axis=-1, keepdims=True)
    rstd = lax.rsqrt(var + LN_EPS)
    xhat = zc * rstd
    dxh = dy * g
    m1 = jnp.mean(dxh, axis=-1, keepdims=True)
    m2 = jnp.mean(dxh * xhat, axis=-1, keepdims=True)
    return rstd * (dxh - m1 - xhat * m2), xhat


def _tile(t, tm):
    tm = min(tm, t)
    assert t % tm == 0, (t, tm)
    return tm


def ffn_fwd(x, wg, wu, wd, ln_g, ln_b, comm=None):
    t, d = x.shape
    s_n, _, fs = wg.shape
    tm = _tile(t, TM_FWD)
    n_i = t // tm

    def body(*refs):
        (x_ref, wg_ref, wu_ref, wd_ref, g_ref, b_ref), c_in, (y_ref, yb_ref, z_ref), c_out, (acc_ref,), c_sem = (
            _split_refs(refs, 6, 3, 1, comm))
        i = pl.program_id(0)
        s = pl.program_id(1)
        if comm is not None:
            @pl.when(jnp.logical_and(i == 0, s == 0))
            def _():
                comm.start(c_in, c_out, c_sem)

        xb = x_ref[...].astype(BF16)
        g = _dot(xb, wg_ref[0])
        u = _dot(xb, wu_ref[0])
        a = (g * _sigmoid(g) * u).astype(BF16)
        part = _dot(a, wd_ref[0])

        @pl.when(s == 0)
        def _():
            acc_ref[...] = part

        @pl.when(s > 0)
        def _():
            acc_ref[...] += part

        @pl.when(s == s_n - 1)
        def _():
            z = ALPHA * x_ref[...] + 0.5 * acc_ref[...]
            z_ref[...] = z
            y = _ln_fwd(z, g_ref[...], b_ref[...])
            y_ref[...] = y
            yb_ref[...] = y.astype(BF16)

        if comm is not None:
            @pl.when(jnp.logical_and(i == n_i - 1, s == s_n - 1))
            def _():
                comm.wait(c_in, c_out, c_sem)

    tok = lambda i, s: (i, 0)
    one = lambda i, s: (0, 0)
    return _call_with_comm(
        body, "ffn_fwd", (n_i, s_n),
        [pl.BlockSpec((tm, d), tok),
         pl.BlockSpec((1, d, fs), lambda i, s: (s, 0, 0)),
         pl.BlockSpec((1, d, fs), lambda i, s: (s, 0, 0)),
         pl.BlockSpec((1, fs, d), lambda i, s: (s, 0, 0)),
         pl.BlockSpec((1, d), one), pl.BlockSpec((1, d), one)],
        [pl.BlockSpec((tm, d), tok), pl.BlockSpec((tm, d), tok), pl.BlockSpec((tm, d), tok)],
        [jax.ShapeDtypeStruct((t, d), F32), jax.ShapeDtypeStruct((t, d), BF16), jax.ShapeDtypeStruct((t, d), F32)],
        [pltpu.VMEM((tm, d), F32)],
        [x, wg, wu, wd, ln_g, ln_b], comm)


def ffn_bwd(xb, dzb, wg, wu, wd, comm=None):
    t, d = xb.shape
    s_n, _, fs = wg.shape
    tm = _tile(t, TM_BWD)
    n_i = t // tm

    def body(*refs):
        ((x_ref, dzb_ref, wg_ref, wu_ref, wd_ref), c_in, (dx_ref, dwg_ref, dwu_ref, dwd_ref), c_out,
         (accg, accu, accd), c_sem) = _split_refs(refs, 5, 4, 3, comm)
        s = pl.program_id(0)
        i = pl.program_id(1)
        if comm is not None:
            @pl.when(jnp.logical_and(i == 0, s == 0))
            def _():
                comm.start(c_in, c_out, c_sem)

        @pl.when(i == 0)
        def _():
            accg[...] = jnp.zeros_like(accg)
            accu[...] = jnp.zeros_like(accu)
            accd[...] = jnp.zeros_like(accd)

        x_v = x_ref[...]
        g = _dot(x_v, wg_ref[0])
        u = _dot(x_v, wu_ref[0])
        sg = _sigmoid(g)
        si = g * sg
        a = (si * u).astype(BF16)
        dfb = dzb_ref[...] * 0.5
        da = _dot_nt(dfb, wd_ref[0])
        dgate = (da * u * (sg * (1.0 + g * (1.0 - sg)))).astype(BF16)
        dup = (da * si).astype(BF16)
        dx_ref[0] = _dot_nt(dgate, wg_ref[0]) + _dot_nt(dup, wu_ref[0])
        accg[...] += _dot_tn(x_v, dgate)
        accu[...] += _dot_tn(x_v, dup)
        accd[...] += _dot_tn(a, dfb)

        @pl.when(i == n_i - 1)
        def _():
            dwg_ref[0] = accg[...].astype(BF16)
            dwu_ref[0] = accu[...].astype(BF16)
            dwd_ref[0] = accd[...].astype(BF16)

        if comm is not None:
            @pl.when(jnp.logical_and(i == n_i - 1, s == s_n - 1))
            def _():
                comm.wait(c_in, c_out, c_sem)

    tok = lambda s, i: (i, 0)
    shard = lambda s, i: (s, 0, 0)
    return _call_with_comm(
        body, "ffn_bwd", (s_n, n_i),
        [pl.BlockSpec((tm, d), tok), pl.BlockSpec((tm, d), tok),
         pl.BlockSpec((1, d, fs), shard), pl.BlockSpec((1, d, fs), shard), pl.BlockSpec((1, fs, d), shard)],
        [pl.BlockSpec((1, tm, d), lambda s, i: (s, i, 0)),
         pl.BlockSpec((1, d, fs), shard), pl.BlockSpec((1, d, fs), shard), pl.BlockSpec((1, fs, d), shard)],
        [jax.ShapeDtypeStruct((s_n, t, d), F32),
         jax.ShapeDtypeStruct((s_n, d, fs), BF16), jax.ShapeDtypeStruct((s_n, d, fs), BF16),
         jax.ShapeDtypeStruct((s_n, fs, d), BF16)],
        [pltpu.VMEM((d, fs), F32), pltpu.VMEM((d, fs), F32), pltpu.VMEM((fs, d), F32)],
        [xb, dzb, wg, wu, wd], comm)


def _sum_parts(base_ref, parts_ref):
    v = base_ref[...]
    if parts_ref is not None:
        for p in range(parts_ref.shape[0]):
            v = v + parts_ref[p]
    return v


def ln_bwd(dy, parts, z, ln_g):
    t, d = dy.shape
    tm = _tile(t, TM_FWD)
    n_p = 0 if parts is None else parts.shape[0]

    def body(*refs):
        if parts is None:
            dy_ref, z_ref, g_ref, dzb_ref, dxr_ref, dg_ref, db_ref = refs
            parts_ref = None
        else:
            dy_ref, parts_ref, z_ref, g_ref, dzb_ref, dxr_ref, dg_ref, db_ref = refs
        i = pl.program_id(0)
        dy_v = _sum_parts(dy_ref, parts_ref)
        dz, xhat = _ln_bwd(dy_v, z_ref[...], g_ref[...])
        dzb_ref[...] = dz.astype(BF16)
        dxr_ref[...] = ALPHA * dz

        @pl.when(i == 0)
        def _():
            dg_ref[...] = jnp.zeros_like(dg_ref)
            db_ref[...] = jnp.zeros_like(db_ref)

        dg_ref[...] += jnp.sum(dy_v * xhat, axis=0, keepdims=True)
        db_ref[...] += jnp.sum(dy_v, axis=0, keepdims=True)

    tok = lambda i: (i, 0)
    one = lambda i: (0, 0)
    in_specs = [pl.BlockSpec((tm, d), tok)]
    args = [dy]
    if parts is not None:
        in_specs.append(pl.BlockSpec((n_p, tm, d), lambda i: (0, i, 0)))
        args.append(parts)
    return pl.pallas_call(
        body, name="ln_bwd" if parts is None else "ln_bwd_parts",
        grid=(t // tm,),
        in_specs=in_specs + [pl.BlockSpec((tm, d), tok), pl.BlockSpec((1, d), one)],
        out_specs=[pl.BlockSpec((tm, d), tok), pl.BlockSpec((tm, d), tok),
                   pl.BlockSpec((1, d), one), pl.BlockSpec((1, d), one)],
        out_shape=[jax.ShapeDtypeStruct((t, d), BF16), jax.ShapeDtypeStruct((t, d), F32),
                   jax.ShapeDtypeStruct((1, d), F32), jax.ShapeDtypeStruct((1, d), F32)],
        compiler_params=_cparams(),
    )(*args, z, ln_g)


def add_parts(base, parts):
    t, d = base.shape
    tm = _tile(t, TM_FWD)

    def body(b_ref, p_ref, o_ref):
        o_ref[...] = _sum_parts(b_ref, p_ref)

    tok = lambda i: (i, 0)
    return pl.pallas_call(
        body, name="add_parts",
        grid=(t // tm,),
        in_specs=[pl.BlockSpec((tm, d), tok), pl.BlockSpec((parts.shape[0], tm, d), lambda i: (0, i, 0))],
        out_specs=pl.BlockSpec((tm, d), tok),
        out_shape=jax.ShapeDtypeStruct((t, d), F32),
        compiler_params=_cparams(),
    )(base, parts)


def loss_and_grad(y, target):
    t, d = y.shape
    tm = _tile(t, TM_FWD)

    def body(y_ref, t_ref, dy_ref, l_ref):
        i = pl.program_id(0)
        e = y_ref[...] - t_ref[...]
        dy_ref[...] = e * (1.0 / d)

        @pl.when(i == 0)
        def _():
            l_ref[...] = jnp.zeros_like(l_ref)

        l_ref[...] += (0.5 / d) * jnp.sum(e * e)

    tok = lambda i: (i, 0)
    return pl.pallas_call(
        body, name="loss",
        grid=(t // tm,),
        in_specs=[pl.BlockSpec((tm, d), tok), pl.BlockSpec((tm, d), tok)],
        out_specs=[pl.BlockSpec((tm, d), tok), pl.BlockSpec((8, 128), lambda i: (0, 0))],
        out_shape=[jax.ShapeDtypeStruct((t, d), F32), jax.ShapeDtypeStruct((8, 128), F32)],
        compiler_params=_cparams(),
    )(y, target)


def matmul(a, w, add=None, out_dtype=F32, name="matmul"):
    t, k = a.shape
    n = w.shape[1]
    tm = _tile(t, TM_FWD)

    def body(*refs):
        if add is None:
            a_ref, w_ref, o_ref = refs
        else:
            a_ref, w_ref, add_ref, o_ref = refs
        o = _dot(a_ref[...].astype(BF16), w_ref[...])
        if add is not None:
            o = o + add_ref[...]
        o_ref[...] = o.astype(out_dtype)

    tok = lambda i: (i, 0)
    in_specs = [pl.BlockSpec((tm, k), tok), pl.BlockSpec((k, n), lambda i: (0, 0))]
    args = [a, w]
    if add is not None:
        in_specs.append(pl.BlockSpec((tm, n), tok))
        args.append(add)
    return pl.pallas_call(
        body, name=name,
        grid=(t // tm,),
        in_specs=in_specs,
        out_specs=pl.BlockSpec((tm, n), tok),
        out_shape=jax.ShapeDtypeStruct((t, n), out_dtype),
        compiler_params=_cparams(),
    )(*args)


def matmul_tn(a, b, name="matmul_tn"):
    t, k = a.shape
    n = b.shape[1]
    tm = _tile(t, TM_FWD)

    def body(a_ref, b_ref, o_ref):
        i = pl.program_id(0)

        @pl.when(i == 0)
        def _():
            o_ref[...] = jnp.zeros_like(o_ref)

        o_ref[...] += _dot_tn(a_ref[...].astype(BF16), b_ref[...].astype(BF16))

    tok = lambda i: (i, 0)
    return pl.pallas_call(
        body, name=name,
        grid=(t // tm,),
        in_specs=[pl.BlockSpec((tm, k), tok), pl.BlockSpec((tm, n), tok)],
        out_specs=pl.BlockSpec((k, n), lambda i: (0, 0)),
        out_shape=jax.ShapeDtypeStruct((k, n), F32),
        compiler_params=_cparams(),
    )(a, b)


def out_proj_ln(ycat, w_out, h, ln_g, ln_b):
    t, d = h.shape
    k = ycat.shape[1]
    tm = _tile(t, TM_FWD)

    def body(yc_ref, w_ref, h_ref, g_ref, b_ref, y_ref, yb_ref, z_ref):
        z = ALPHA * h_ref[...] + _dot(yc_ref[...], w_ref[...])
        z_ref[...] = z
        y = _ln_fwd(z, g_ref[...], b_ref[...])
        y_ref[...] = y
        yb_ref[...] = y.astype(BF16)

    tok = lambda i: (i, 0)
    one = lambda i: (0, 0)
    return pl.pallas_call(
        body, name="out_proj_ln",
        grid=(t // tm,),
        in_specs=[pl.BlockSpec((tm, k), tok), pl.BlockSpec((k, d), one), pl.BlockSpec((tm, d), tok),
                  pl.BlockSpec((1, d), one), pl.BlockSpec((1, d), one)],
        out_specs=[pl.BlockSpec((tm, d), tok), pl.BlockSpec((tm, d), tok), pl.BlockSpec((tm, d), tok)],
        out_shape=[jax.ShapeDtypeStruct((t, d), F32), jax.ShapeDtypeStruct((t, d), BF16),
                   jax.ShapeDtypeStruct((t, d), F32)],
        compiler_params=_cparams(),
    )(ycat, w_out, h, ln_g, ln_b)


def _halo_specs(tm, cols, n_rows):
    r = tm // HALO
    last = n_rows // HALO - 1
    return [pl.BlockSpec((HALO, cols), lambda i: (jnp.maximum(i * r - 1, 0), 0)),
            pl.BlockSpec((tm, cols), lambda i: (i, 0)),
            pl.BlockSpec((HALO, cols), lambda i: (jnp.minimum((i + 1) * r, last), 0))]


def _fill_ext(dst, prev_ref, main_ref, next_ref, i, n_i):
    tm = main_ref.shape[0]
    dst[0:HALO, :] = jnp.where(i > 0, prev_ref[...], 0.0)
    dst[HALO:HALO + tm, :] = main_ref[...]
    dst[HALO + tm:HALO + tm + HALO, :] = jnp.where(i < n_i - 1, next_ref[...], 0.0)


def _pool_lane_half():
    lane = lax.broadcasted_iota(jnp.int32, (1, POOL_W), 1)
    return jnp.left_shift(1, lane // 64)


def _pool_inv_count(t0, rows, seq):
    half = _pool_lane_half()
    tpos = t0 + lax.broadcasted_iota(jnp.int32, (rows, 1), 0)
    lo = jnp.maximum(tpos - half, 0)
    hi = jnp.minimum(tpos + half, seq)
    cnt = jnp.maximum(hi - lo, 1)
    return 1.0 / cnt.astype(F32)


def _pool_forward(p_ext, tm, t0, seq):
    half = _pool_lane_half()
    total = jnp.zeros((tm, POOL_W), F32)
    for o in range(-8, 8):
        m = ((o >= -half) & (o < half)).astype(F32)
        total = total + m * p_ext[HALO + o:HALO + o + tm, C_POOL[0]:C_POOL[1]]
    u = p_ext[HALO:HALO + tm, C_POOL[0]:C_POOL[1]]
    return total * _pool_inv_count(t0, tm, seq) - u


def mix_fwd(proj, pw_bd, pool_scale, sconv_w, cconv_w, cconv_b, cnorm_g, cnorm_b):
    t, pc = proj.shape
    tm = _tile(t, TM_MIX)
    n_i = t // tm
    e = tm + 2 * HALO

    def body(pp_ref, pm_ref, pn_ref, pw_ref, ps_ref, sw_ref, cw_ref, cb_ref, cg_ref, cbb_ref,
             yc_ref, a1_ref, p_ext, q_ext, a0_ext):
        i = pl.program_id(0)
        _fill_ext(p_ext, pp_ref, pm_ref, pn_ref, i, n_i)
        pooled = _pool_forward(p_ext, tm, i * tm, t)
        y_a = _dot(pooled.astype(BF16), pw_ref[...]) * ps_ref[...]
        yc_ref[:, 0:256] = y_a.astype(BF16)
        q_ext[...] = p_ext[:, C_GC[0]:C_GC[1]] * p_ext[:, C_V[0]:C_V[1]]
        conv = jnp.zeros((tm, CONV_W), F32)
        for k in range(SCONV_K):
            conv = conv + sw_ref[k:k + 1, :] * q_ext[HALO + k - 1:HALO + k - 1 + tm, :]
        y_b = p_ext[HALO:HALO + tm, C_GB[0]:C_GB[1]] * conv
        yc_ref[:, 256:640] = y_b.astype(BF16)
        a0_ext[...] = p_ext[:, C_CV[0]:C_CV[1]] * _sigmoid(p_ext[:, C_CG[0]:C_CG[1]])
        a1 = jnp.zeros((tm, CONV_W), F32) + cb_ref[...]
        for k in range(CCONV_K):
            a1 = a1 + cw_ref[k:k + 1, :] * a0_ext[HALO + k - 15:HALO + k - 15 + tm, :]
        a1_ref[...] = a1
        ln = _ln_fwd(a1, cg_ref[...], cbb_ref[...])
        yc_ref[:, 640:1024] = (ln * _sigmoid(ln)).astype(BF16)

    one = lambda i: (0, 0)
    return pl.pallas_call(
        body, name="mix_fwd",
        grid=(n_i,),
        in_specs=_halo_specs(tm, pc, t) + [
            pl.BlockSpec((POOL_W, POOL_W), one), pl.BlockSpec((1, POOL_W), one),
            pl.BlockSpec((8, CONV_W), one), pl.BlockSpec((32, CONV_W), one),
            pl.BlockSpec((1, CONV_W), one), pl.BlockSpec((1, CONV_W), one), pl.BlockSpec((1, CONV_W), one)],
        out_specs=[pl.BlockSpec((tm, 1024), lambda i: (i, 0)), pl.BlockSpec((tm, CONV_W), lambda i: (i, 0))],
        out_shape=[jax.ShapeDtypeStruct((t, 1024), BF16), jax.ShapeDtypeStruct((t, CONV_W), F32)],
        scratch_shapes=[pltpu.VMEM((e, pc), F32), pltpu.VMEM((e, CONV_W), F32), pltpu.VMEM((e, CONV_W), F32)],
        compiler_params=_cparams(),
    )(proj, proj, proj, pw_bd, pool_scale, sconv_w, cconv_w, cconv_b, cnorm_g, cnorm_b)


def mix_bwd(proj, dycat, a1, pw_bd, pw_bd_t, pool_scale, sconv_w, cconv_w, cnorm_g, cnorm_b):
    t, pc = proj.shape
    tm = _tile(t, TM_MIX)
    n_i = t // tm
    e = tm + 2 * HALO

    def body(pp_ref, pm_ref, pn_ref, dp_ref, dm_ref, dn_ref, ap_ref, am_ref, an_ref,
             pw_ref, pwt_ref, ps_ref, sw_ref, cw_ref, cg_ref, cbb_ref,
             dproj_ref, dpw_ref, dps_ref, dsw_ref, dcw_ref, dcb_ref, dcg_ref, dcbb_ref,
             p_ext, dy_ext, a1_ext, a0_ext, da1_ext, q_ext, dc_ext, dpn_ext):
        i = pl.program_id(0)
        main = slice(HALO, HALO + tm)

        @pl.when(i == 0)
        def _():
            for r in (dpw_ref, dps_ref, dsw_ref, dcw_ref, dcb_ref, dcg_ref, dcbb_ref):
                r[...] = jnp.zeros_like(r)

        _fill_ext(p_ext, pp_ref, pm_ref, pn_ref, i, n_i)
        _fill_ext(dy_ext, dp_ref, dm_ref, dn_ref, i, n_i)
        _fill_ext(a1_ext, ap_ref, am_ref, an_ref, i, n_i)

        sig_cg = _sigmoid(p_ext[:, C_CG[0]:C_CG[1]])
        a0_ext[...] = p_ext[:, C_CV[0]:C_CV[1]] * sig_cg
        a1_v = a1_ext[...]
        mu = jnp.mean(a1_v, axis=-1, keepdims=True)
        zc = a1_v - mu
        var = jnp.mean(zc * zc, axis=-1, keepdims=True)
        rstd = lax.rsqrt(var + LN_EPS)
        xhat = zc * rstd
        ln = xhat * cg_ref[...] + cbb_ref[...]
        sl = _sigmoid(ln)
        dln = dy_ext[:, 640:1024] * (sl * (1.0 + ln * (1.0 - sl)))
        dcg_ref[...] += jnp.sum((dln * xhat)[main], axis=0, keepdims=True)
        dcbb_ref[...] += jnp.sum(dln[main], axis=0, keepdims=True)
        dxh = dln * cg_ref[...]
        m1 = jnp.mean(dxh, axis=-1, keepdims=True)
        m2 = jnp.mean(dxh * xhat, axis=-1, keepdims=True)
        da1 = rstd * (dxh - m1 - xhat * m2)
        da1_ext[...] = da1
        da1_m = da1[main]
        dcb_ref[...] += jnp.sum(da1_m, axis=0, keepdims=True)
        da0 = jnp.zeros((tm, CONV_W), F32)
        for k in range(CCONV_K):
            da0 = da0 + cw_ref[k:k + 1, :] * da1_ext[HALO - k + 15:HALO - k + 15 + tm, :]
            dcw_ref[k:k + 1, :] += jnp.sum(da1_m * a0_ext[HALO + k - 15:HALO + k - 15 + tm, :],
                                           axis=0, keepdims=True)
        sig_m = sig_cg[main]
        cv_m = p_ext[main, C_CV[0]:C_CV[1]]
        dproj_ref[:, C_CV[0]:C_CV[1]] = (da0 * sig_m).astype(BF16)
        dproj_ref[:, C_CG[0]:C_CG[1]] = (da0 * cv_m * sig_m * (1.0 - sig_m)).astype(BF16)

        q_ext[...] = p_ext[:, C_GC[0]:C_GC[1]] * p_ext[:, C_V[0]:C_V[1]]
        dc_ext[...] = dy_ext[:, 256:640] * p_ext[:, C_GB[0]:C_GB[1]]
        dc_m = dc_ext[main, :]
        conv = jnp.zeros((tm, CONV_W), F32)
        dq = jnp.zeros((tm, CONV_W), F32)
        for k in range(SCONV_K):
            q_k = q_ext[HALO + k - 1:HALO + k - 1 + tm, :]
            conv = conv + sw_ref[k:k + 1, :] * q_k
            dq = dq + sw_ref[k:k + 1, :] * dc_ext[HALO - k + 1:HALO - k + 1 + tm, :]
            dsw_ref[k:k + 1, :] += jnp.sum(dc_m * q_k, axis=0, keepdims=True)
        dproj_ref[:, C_GB[0]:C_GB[1]] = (dy_ext[main, 256:640] * conv).astype(BF16)
        dproj_ref[:, C_GC[0]:C_GC[1]] = (dq * p_ext[main, C_V[0]:C_V[1]]).astype(BF16)
        dproj_ref[:, C_V[0]:C_V[1]] = (dq * p_ext[main, C_GC[0]:C_GC[1]]).astype(BF16)

        t0 = i * tm
        dya = dy_ext[:, 0:256] * ps_ref[...]
        dpooled = _dot(dya.astype(BF16), pwt_ref[...])
        dpn_ext[...] = dpooled * _pool_inv_count(t0 - HALO, e, t)
        half = _pool_lane_half()
        du = jnp.zeros((tm, POOL_W), F32)
        for o in range(-7, 9):
            m = ((o > -half) & (o <= half)).astype(F32)
            du = du + m * dpn_ext[HALO + o:HALO + o + tm, :]
        dproj_ref[:, C_POOL[0]:C_POOL[1]] = (du - dpooled[main]).astype(BF16)
        pooled = _pool_forward(p_ext, tm, t0, t)
        pooled_b = pooled.astype(BF16)
        ya_pre = _dot(pooled_b, pw_ref[...])
        dps_ref[...] += jnp.sum(dy_ext[main, 0:256] * ya_pre, axis=0, keepdims=True)
        dpw_ref[...] += _dot_tn(pooled_b, dya[main].astype(BF16))

    one = lambda i: (0, 0)
    small = [((POOL_W, POOL_W), F32), ((1, POOL_W), F32), ((8, CONV_W), F32), ((32, CONV_W), F32),
             ((1, CONV_W), F32), ((1, CONV_W), F32), ((1, CONV_W), F32)]
    return pl.pallas_call(
        body, name="mix_bwd",
        grid=(n_i,),
        in_specs=_halo_specs(tm, pc, t) + _halo_specs(tm, 1024, t) + _halo_specs(tm, CONV_W, t) + [
            pl.BlockSpec((POOL_W, POOL_W), one), pl.BlockSpec((POOL_W, POOL_W), one), pl.BlockSpec((1, POOL_W), one),
            pl.BlockSpec((8, CONV_W), one), pl.BlockSpec((32, CONV_W), one),
            pl.BlockSpec((1, CONV_W), one), pl.BlockSpec((1, CONV_W), one)],
        out_specs=[pl.BlockSpec((tm, pc), lambda i: (i, 0))] + [pl.BlockSpec(s, one) for s, _ in small],
        out_shape=[jax.ShapeDtypeStruct((t, pc), BF16)] + [jax.ShapeDtypeStruct(s, dt) for s, dt in small],
        scratch_shapes=[pltpu.VMEM((e, pc), F32), pltpu.VMEM((e, 1024), F32), pltpu.VMEM((e, CONV_W), F32),
                        pltpu.VMEM((e, CONV_W), F32), pltpu.VMEM((e, CONV_W), F32), pltpu.VMEM((e, CONV_W), F32),
                        pltpu.VMEM((e, CONV_W), F32), pltpu.VMEM((e, POOL_W), F32)],
        compiler_params=_cparams(),
    )(proj, proj, proj, dycat, dycat, dycat, a1, a1, a1,
      pw_bd, pw_bd_t, pool_scale, sconv_w, cconv_w, cnorm_g, cnorm_b)


def _mesh_pos():
    return lax.axis_index("x"), lax.axis_index("y"), lax.axis_index("c")


def _flip(v, f):
    return 1 - v if f else v


class _Comm:
    def __init__(self, kind, arrs, lands=None, layer=0):
        self.kind = kind
        self.arrs = list(arrs)
        self.n = len(self.arrs)
        self.lands = None if lands is None else list(lands)
        self.layer = layer
        if kind == "gather":
            self.flips = [(1, 0, 0), (0, 1, 0), (1, 1, 0)]
            self.out_shape = [jax.ShapeDtypeStruct((N_CHIPS,) + a.shape, a.dtype) for a in self.arrs]
        else:
            self.flips = [(fx, fy, fc) for fx in (0, 1) for fy in (0, 1) for fc in (0, 1)][1:]
            self.out_shape = [jax.ShapeDtypeStruct(b.shape, b.dtype) for b in self.lands]
        self.n_peer = len(self.flips)

    def operands(self):
        return self.arrs + (self.lands or [])

    def scratch(self):
        return [pltpu.SemaphoreType.DMA((self.n * self.n_peer,)), pltpu.SemaphoreType.DMA((self.n * self.n_peer,)),
                pltpu.SemaphoreType.DMA((self.n,))]

    def aliases(self, in_off, out_off):
        if self.lands is None:
            return {}
        return {in_off + self.n + j: out_off + j for j in range(self.n)}

    def _copies(self, ins, outs, sems):
        send_sems, recv_sems, loc_sems = sems
        x, y, c = _mesh_pos()
        local, sends, recvs = [], [], []
        for a in range(self.n):
            def src(px, py):
                return ins[a] if self.kind == "gather" else ins[a].at[2 * px + py]

            def dst(px, py, pc):
                if self.kind == "gather":
                    return outs[a].at[2 * px + py]
                return outs[a].at[4 * px + 2 * py + pc, self.layer]

            local.append(pltpu.make_async_copy(src(x, y), dst(x, y, c), loc_sems.at[a]))
            for k, (fx, fy, fc) in enumerate(self.flips):
                px, py, pc = _flip(x, fx), _flip(y, fy), _flip(c, fc)
                sem = a * self.n_peer + k
                sends.append(pltpu.make_async_remote_copy(
                    src_ref=src(px, py), dst_ref=dst(x, y, c),
                    send_sem=send_sems.at[sem], recv_sem=recv_sems.at[sem],
                    device_id=(px, py, pc), device_id_type=MESH))
                recvs.append(pltpu.make_async_remote_copy(
                    src_ref=src(px, py), dst_ref=dst(px, py, pc),
                    send_sem=send_sems.at[sem], recv_sem=recv_sems.at[sem],
                    device_id=(px, py, pc), device_id_type=MESH))
        return local, sends, recvs

    def start(self, ins, outs, sems):
        local, sends, _ = self._copies(ins, outs, sems)
        for cp in local + sends:
            cp.start()

    def wait(self, ins, outs, sems):
        local, _, recvs = self._copies(ins, outs, sems)
        for cp in local:
            cp.wait()
        for cp in recvs:
            cp.wait()


def _split_refs(refs, n_in, n_out, n_scr, comm):
    c_in = len(comm.operands()) if comm is not None else 0
    c_out = comm.n if comm is not None else 0
    cuts = [n_in, c_in, n_out, c_out, n_scr]
    out, pos = [], 0
    for m in cuts:
        out.append(refs[pos:pos + m])
        pos += m
    out.append(refs[pos:])
    return out


def _call_with_comm(body, name, grid, in_specs, out_specs, out_shape, scratch, args, comm):
    hbm = pl.BlockSpec(memory_space=pl.ANY)
    aliases = {}
    if comm is not None:
        aliases = comm.aliases(len(in_specs), len(out_specs))
        in_specs = in_specs + [hbm] * len(comm.operands())
        out_specs = out_specs + [hbm] * comm.n
        out_shape = out_shape + comm.out_shape
        scratch = scratch + comm.scratch()
        args = args + comm.operands()
        name = name + "_" + comm.kind
    res = pl.pallas_call(
        body, name=name, grid=grid, in_specs=in_specs, out_specs=out_specs, out_shape=out_shape,
        scratch_shapes=scratch, input_output_aliases=aliases, compiler_params=_cparams(),
    )(*args)
    if comm is None:
        return res, None
    return res[:len(res) - comm.n], res[len(res) - comm.n:]


def run_comm(comm):
    def body(*refs):
        _, c_in, _, c_out, _, c_sem = _split_refs(refs, 0, 0, 0, comm)
        comm.start(c_in, c_out, c_sem)
        comm.wait(c_in, c_out, c_sem)

    hbm = pl.BlockSpec(memory_space=pl.ANY)
    return pl.pallas_call(
        body, name="comm_" + comm.kind,
        in_specs=[hbm] * len(comm.operands()), out_specs=[hbm] * comm.n, out_shape=comm.out_shape,
        scratch_shapes=comm.scratch(), input_output_aliases=comm.aliases(0, 0),
    )(*comm.operands())


def exchange(arrs, per_chip):
    n = len(arrs)
    flips = [(fx, fy, fc) for fx in (0, 1) for fy in (0, 1) for fc in (0, 1)][1:]

    def body(*refs):
        ins, outs = refs[:n], refs[n:2 * n]
        send_sems, recv_sems, loc_sems = refs[2 * n:]
        x, y, c = _mesh_pos()
        me = 4 * x + 2 * y + c
        local = []
        remote = []
        for a in range(n):
            def part(px, py):
                return ins[a].at[2 * px + py] if per_chip else ins[a]

            cp = pltpu.make_async_copy(part(x, y), outs[a].at[me], loc_sems.at[a])
            cp.start()
            local.append(cp)
            for k, (fx, fy, fc) in enumerate(flips):
                px, py, pc = _flip(x, fx), _flip(y, fy), _flip(c, fc)
                sem = a * 7 + k
                rc = pltpu.make_async_remote_copy(
                    src_ref=part(px, py), dst_ref=outs[a].at[me],
                    send_sem=send_sems.at[sem], recv_sem=recv_sems.at[sem],
                    device_id=(px, py, pc), device_id_type=MESH)
                rc.start()
                remote.append(pltpu.make_async_remote_copy(
                    src_ref=part(px, py), dst_ref=outs[a].at[4 * px + 2 * py + pc],
                    send_sem=send_sems.at[sem], recv_sem=recv_sems.at[sem],
                    device_id=(px, py, pc), device_id_type=MESH))
        for cp in local:
            cp.wait()
        for rc in remote:
            rc.wait()

    hbm = pl.BlockSpec(memory_space=pl.ANY)
    shapes = [a.shape[1:] if per_chip else a.shape for a in arrs]
    return pl.pallas_call(
        body, name="exchange_per_chip" if per_chip else "exchange_all",
        in_specs=[hbm] * n, out_specs=[hbm] * n,
        out_shape=[jax.ShapeDtypeStruct((N_DEV,) + s, a.dtype) for s, a in zip(shapes, arrs)],
        scratch_shapes=[pltpu.SemaphoreType.DMA((7 * n,)), pltpu.SemaphoreType.DMA((7 * n,)),
                        pltpu.SemaphoreType.DMA((n,))],
    )(*arrs)


def adamw(parts, w, m, v, name):
    k_n, r, c = parts.shape
    tr = r
    for cand in (512, 256, 128, 64, 32, 16, 8):
        if r % cand == 0:
            tr = cand
            break

    def body(p_ref, w_ref, m_ref, v_ref, g_ref, d_ref, mo_ref, vo_ref):
        g = p_ref[0].astype(F32)
        for k in range(1, k_n):
            g = g + p_ref[k].astype(F32)
        m_new = ADAM_B1 * m_ref[...] + (1.0 - ADAM_B1) * g
        v_new = ADAM_B2 * v_ref[...] + (1.0 - ADAM_B2) * (g * g)
        m_hat = m_new / (1.0 - ADAM_B1 ** ADAM_STEP)
        v_hat = v_new / (1.0 - ADAM_B2 ** ADAM_STEP)
        g_ref[...] = g
        d_ref[...] = -ADAM_LR * (m_hat / (jnp.sqrt(v_hat) + ADAM_EPS) + ADAM_WD * w_ref[...])
        mo_ref[...] = m_new
        vo_ref[...] = v_new

    blk = pl.BlockSpec((tr, c), lambda i: (i, 0))
    return pl.pallas_call(
        body, name=name,
        grid=(r // tr,),
        in_specs=[pl.BlockSpec((k_n, tr, c), lambda i: (0, i, 0)), blk, blk, blk],
        out_specs=[blk, blk, blk, blk],
        out_shape=[jax.ShapeDtypeStruct((r, c), F32)] * 4,
        compiler_params=_cparams(),
    )(parts, w, m, v)


def _block_diag(pool_w):
    out = jnp.zeros((POOL_W, POOL_W), pool_w.dtype)
    for g in range(4):
        out = lax.dynamic_update_slice(out, pool_w[g], (64 * g, 64 * g))
    return out


def _pad_rows(a, rows):
    return jnp.pad(a, ((0, rows - a.shape[0]), (0, 0)))


def kernel(x, ln1_g, ln1_b, ffn1_w_gate, ffn1_w_up, ffn1_w_down, mix_w_in, pool_w, pool_scale, sconv_w, cconv_w, cconv_b, cnorm_g, cnorm_b, mix_w_out, ln2_g, ln2_b, ffn2_w_gate, ffn2_w_up, ffn2_w_down, ln3_g, ln3_b, loss_target, m_ln1_g, m_ln1_b, m_ffn1_w_gate, m_ffn1_w_up, m_ffn1_w_down, m_mix_w_in, m_pool_w, m_pool_scale, m_sconv_w, m_cconv_w, m_cconv_b, m_cnorm_g, m_cnorm_b, m_mix_w_out, m_ln2_g, m_ln2_b, m_ffn2_w_gate, m_ffn2_w_up, m_ffn2_w_down, m_ln3_g, m_ln3_b, v_ln1_g, v_ln1_b, v_ffn1_w_gate, v_ffn1_w_up, v_ffn1_w_down, v_mix_w_in, v_pool_w, v_pool_scale, v_sconv_w, v_cconv_w, v_cconv_b, v_cnorm_g, v_cnorm_b, v_mix_w_out, v_ln2_g, v_ln2_b, v_ffn2_w_gate, v_ffn2_w_up, v_ffn2_w_down, v_ln3_g, v_ln3_b):
    names = ['ln1_g', 'ln1_b', 'ffn1_w_gate', 'ffn1_w_up', 'ffn1_w_down', 'mix_w_in', 'pool_w', 'pool_scale',
             'sconv_w', 'cconv_w', 'cconv_b', 'cnorm_g', 'cnorm_b', 'mix_w_out', 'ln2_g', 'ln2_b',
             'ffn2_w_gate', 'ffn2_w_up', 'ffn2_w_down', 'ln3_g', 'ln3_b']
    w = dict(zip(names, (ln1_g, ln1_b, ffn1_w_gate, ffn1_w_up, ffn1_w_down, mix_w_in, pool_w, pool_scale, sconv_w,
                         cconv_w, cconv_b, cnorm_g, cnorm_b, mix_w_out, ln2_g, ln2_b, ffn2_w_gate, ffn2_w_up,
                         ffn2_w_down, ln3_g, ln3_b)))
    mom_m = dict(zip(names, (m_ln1_g, m_ln1_b, m_ffn1_w_gate, m_ffn1_w_up, m_ffn1_w_down, m_mix_w_in, m_pool_w,
                             m_pool_scale, m_sconv_w, m_cconv_w, m_cconv_b, m_cnorm_g, m_cnorm_b, m_mix_w_out,
                             m_ln2_g, m_ln2_b, m_ffn2_w_gate, m_ffn2_w_up, m_ffn2_w_down, m_ln3_g, m_ln3_b)))
    mom_v = dict(zip(names, (v_ln1_g, v_ln1_b, v_ffn1_w_gate, v_ffn1_w_up, v_ffn1_w_down, v_mix_w_in, v_pool_w,
                             v_pool_scale, v_sconv_w, v_cconv_w, v_cconv_b, v_cnorm_g, v_cnorm_b, v_mix_w_out,
                             v_ln2_g, v_ln2_b, v_ffn2_w_gate, v_ffn2_w_up, v_ffn2_w_down, v_ln3_g, v_ln3_b)))
    big = ['ffn1_w_gate', 'ffn1_w_up', 'ffn1_w_down', 'mix_w_in', 'mix_w_out',
           'ffn2_w_gate', 'ffn2_w_up', 'ffn2_w_down']
    n_l = ln1_g.shape[0]
    d = x.shape[-1]
    fs = ffn1_w_gate.shape[-1]
    ws_in = mix_w_in.shape[-1]
    cs = sconv_w.shape[-1]
    chip = 2 * lax.axis_index("x") + lax.axis_index("y")

    def layer_shards(l):
        conv_loc = jnp.concatenate([sconv_w[l], cconv_w[l]], axis=0)
        return [w[k][l].astype(BF16) for k in big] + [conv_loc]

    def cols(a):
        return jnp.transpose(a, (1, 0, 2)).reshape(a.shape[1], -1)

    def layer_weights(l, gathered):
        gw = dict(zip(big, gathered[:-1]))
        conv_all = jnp.transpose(gathered[-1], (1, 0, 2)).reshape(SCONV_K + CCONV_K, N_CHIPS * cs)
        return dict(
            g1=gw['ffn1_w_gate'], u1=gw['ffn1_w_up'], d1=gw['ffn1_w_down'],
            g2=gw['ffn2_w_gate'], u2=gw['ffn2_w_up'], d2=gw['ffn2_w_down'],
            w_in=cols(gw['mix_w_in']),
            w_out=gw['mix_w_out'].reshape(-1, d),
            pw=_block_diag(pool_w[l]).astype(BF16),
            ps=pool_scale[l][None], sw=_pad_rows(conv_all[:SCONV_K], 8), cw=_pad_rows(conv_all[SCONV_K:], 32),
            cb=cconv_b[l][None], cg=cnorm_g[l][None], cbb=cnorm_b[l][None],
        )

    h = x[0]
    hb = h.astype(BF16)
    target = loss_target[0]
    saved = []
    layer_w = []
    gathered = run_comm(_Comm("gather", layer_shards(0)))
    for l in range(n_l):
        lw = layer_weights(l, gathered)
        layer_w.append(lw)
        comm = _Comm("gather", layer_shards(l + 1)) if l + 1 < n_l else None
        (y1, y1b, z1), gathered = ffn_fwd(h, lw['g1'], lw['u1'], lw['d1'], ln1_g[l][None], ln1_b[l][None], comm)
        proj = matmul(y1b, lw['w_in'], name="proj")
        ycat, a1 = mix_fwd(proj, lw['pw'], lw['ps'], lw['sw'], lw['cw'], lw['cb'], lw['cg'], lw['cbb'])
        y2, y2b, z2 = out_proj_ln(ycat, lw['w_out'], y1, ln2_g[l][None], ln2_b[l][None])
        (y3, y3b, z3), _ = ffn_fwd(y2, lw['g2'], lw['u2'], lw['d2'], ln3_g[l][None], ln3_b[l][None])
        saved.append(dict(x0b=hb, z1=z1, y1b=y1b, proj=proj, ycat=ycat, a1=a1, z2=z2, y2b=y2b, z3=z3))
        h, hb = y3, y3b

    dy, loss_blk = loss_and_grad(h, target)
    loss = lax.psum(loss_blk[0, 0], ("x", "y", "c"))

    g_loc = {k: [None] * n_l for k in names if k not in big}
    keys_a = ['ffn2_w_gate', 'ffn2_w_up', 'ffn2_w_down', 'mix_w_in', 'mix_w_out']
    keys_b = ['ffn1_w_gate', 'ffn1_w_up', 'ffn1_w_down']
    lands = {k: lax.empty((N_DEV, n_l) + w[k].shape[1:], BF16) for k in big}

    def scatter(keys, arrs, layer):
        return _Comm("scatter", arrs, [lands[k] for k in keys], layer)

    pend_b = None
    parts = None
    for l in reversed(range(n_l)):
        lw, sv = layer_w[l], saved[l]
        dzb, dxr, g_loc['ln3_g'][l], g_loc['ln3_b'][l] = ln_bwd(dy, parts, sv['z3'], ln3_g[l][None])
        comm = None if pend_b is None else scatter(keys_b, pend_b, l + 1)
        (parts, dg2, du2, dd2), landed = ffn_bwd(sv['y2b'], dzb, lw['g2'], lw['u2'], lw['d2'], comm)
        if comm is not None:
            lands.update(zip(keys_b, landed))
        dzb, dxr, g_loc['ln2_g'][l], g_loc['ln2_b'][l] = ln_bwd(dxr, parts, sv['z2'], ln2_g[l][None])
        dycat = matmul(dzb, lw['w_out'].T, name="dycat")
        dw_out = matmul_tn(sv['ycat'], dzb, name="dw_out")
        (dproj, dpw, g_loc['pool_scale'][l], dsw, dcw, g_loc['cconv_b'][l], g_loc['cnorm_g'][l],
         g_loc['cnorm_b'][l]) = mix_bwd(sv['proj'], dycat, sv['a1'], lw['pw'], lw['pw'].T, lw['ps'],
                                        lw['sw'], lw['cw'], lw['cg'], lw['cbb'])
        g_loc['pool_w'][l] = jnp.stack([dpw[64 * g:64 * g + 64, 64 * g:64 * g + 64] for g in range(4)])
        g_loc['sconv_w'][l] = dsw[:SCONV_K]
        g_loc['cconv_w'][l] = dcw[:CCONV_K]
        dy = matmul(dproj, lw['w_in'].T, add=dxr, name="dmix_in")
        dw_in = matmul_tn(sv['y1b'], dproj, name="dw_in")
        dw_in_c = jnp.transpose(dw_in.reshape(d, N_CHIPS, ws_in), (1, 0, 2)).astype(BF16)
        dw_out_c = dw_out.reshape(N_CHIPS, -1, d).astype(BF16)
        dzb, dxr, g_loc['ln1_g'][l], g_loc['ln1_b'][l] = ln_bwd(dy, None, sv['z1'], ln1_g[l][None])
        comm = scatter(keys_a, [dg2, du2, dd2, dw_in_c, dw_out_c], l)
        (parts, dg1, du1, dd1), landed = ffn_bwd(sv['x0b'], dzb, lw['g1'], lw['u1'], lw['d1'], comm)
        lands.update(zip(keys_a, landed))
        pend_b = [dg1, du1, dd1]
        dy = dxr
    grad_x = add_parts(dy, parts)[None]
    lands.update(zip(keys_b, run_comm(scatter(keys_b, pend_b, 0))))
    parts_big = lands

    small = [k for k in names if k not in big]
    small_full = {}
    for k in small:
        a = jnp.stack(g_loc[k])
        small_full[k] = a.reshape(n_l, -1) if a.shape[1] == 1 else a
    flat = jnp.concatenate([small_full[k].reshape(-1) for k in small])
    n_flat = flat.shape[0]
    rows = -(-n_flat // (SMALL_ROWS * 128)) * SMALL_ROWS
    flat = jnp.pad(flat, (0, rows * 128 - n_flat)).reshape(rows, 128)
    parts_small = exchange([flat], per_chip=False)[0]

    out_g, out_d, out_m, out_v = {}, {}, {}, {}
    for k in big:
        shp = w[k].shape
        r = shp[0] * shp[1]
        res = adamw(parts_big[k].reshape(N_DEV, r, shp[2]), w[k].reshape(r, shp[2]),
                    mom_m[k].reshape(r, shp[2]), mom_v[k].reshape(r, shp[2]), name="adamw_" + k)
        out_g[k], out_d[k], out_m[k], out_v[k] = [o.reshape(shp) for o in res]

    zeros = jnp.zeros((rows, 128), F32)
    g_sum = adamw(parts_small, zeros, zeros, zeros, name="sum_small")[0].reshape(-1)
    off = 0
    for k in small:
        full = small_full[k]
        g = g_sum[off:off + full.size].reshape(full.shape)
        off += full.size
        if k in ('sconv_w', 'cconv_w'):
            g = lax.dynamic_slice_in_dim(g, chip * cs, cs, axis=2)
        out_g[k] = g.reshape(w[k].shape)

    def pack(dct):
        f = jnp.concatenate([dct[k].reshape(-1) for k in small])
        r2 = -(-f.shape[0] // (SMALL_ROWS * 128)) * SMALL_ROWS
        return jnp.pad(f, (0, r2 * 128 - f.shape[0])).reshape(r2, 128), f.shape[0]

    gp, n_small = pack(out_g)
    wp, _ = pack(w)
    mp, _ = pack(mom_m)
    vp, _ = pack(mom_v)
    _, dp, mo, vo = adamw(gp[None], wp, mp, vp, name="adamw_small")
    off = 0
    for k in small:
        sz = w[k].size
        out_d[k] = dp.reshape(-1)[off:off + sz].reshape(w[k].shape)
        out_m[k] = mo.reshape(-1)[off:off + sz].reshape(w[k].shape)
        out_v[k] = vo.reshape(-1)[off:off + sz].reshape(w[k].shape)
        off += sz

    return (loss, grad_x, *[out_g[k] for k in names], *[out_d[k] for k in names],
            *[out_m[k] for k in names], *[out_v[k] for k in names])
```

```python
import jax
import jax.numpy as jnp
from jax import lax
from jax.experimental import pallas as pl
from jax.experimental.pallas import tpu as pltpu

F32 = jnp.float32
BF16 = jnp.bfloat16

DEPTH = 4
ALPHA = (2.0 * DEPTH) ** 0.25
LN_EPS = 1e-5
POOL_W = 256
CONV_W = 384
SCONV_K = 3
CCONV_K = 31
C_POOL = (0, 256)
C_GB = (256, 640)
C_GC = (640, 1024)
C_V = (1024, 1408)
C_CV = (1408, 1792)
C_CG = (1792, 2176)

ADAM_LR = 0.001
ADAM_B1 = 0.9
ADAM_B2 = 0.999
ADAM_EPS = 1e-08
ADAM_WD = 0.01
ADAM_STEP = 10

N_CHIPS = 4
N_DEV = 8
MESH = pl.DeviceIdType.MESH

TM_FWD = 512
TM_BWD = 512
TM_MIX = 256
HALO = 32
SMALL_ROWS = 256
VMEM_LIMIT = 56 * 1024 * 1024


def _cparams():
    return pltpu.CompilerParams(vmem_limit_bytes=VMEM_LIMIT)


def _sigmoid(v):
    return 1.0 / (1.0 + jnp.exp(-v))


def _dot(a, b):
    return jnp.dot(a, b, preferred_element_type=F32)


def _dot_nt(a, b):
    return lax.dot_general(a, b, (((1,), (1,)), ((), ())), preferred_element_type=F32)


def _dot_tn(a, b):
    return lax.dot_general(a, b, (((0,), (0,)), ((), ())), preferred_element_type=F32)


def _ln_fwd(z, g, b):
    mu = jnp.mean(z, axis=-1, keepdims=True)
    zc = z - mu
    var = jnp.mean(zc * zc, axis=-1, keepdims=True)
    return zc * lax.rsqrt(var + LN_EPS) * g + b


def _ln_bwd(dy, z, g):
    mu = jnp.mean(z, axis=-1, keepdims=True)
    zc = z - mu
    var = jnp.mean(zc * zc, axis=-1, keepdims=True)
    rstd = lax.rsqrt(var + LN_EPS)
    xhat = zc * rstd
    dxh = dy * g
    m1 = jnp.mean(dxh, axis=-1, keepdims=True)
    m2 = jnp.mean(dxh * xhat, axis=-1, keepdims=True)
    return rstd * (dxh - m1 - xhat * m2), xhat


def _tile(t, tm):
    tm = min(tm, t)
    assert t % tm == 0, (t, tm)
    return tm


def ffn_fwd(x, wg, wu, wd, ln_g, ln_b, comm=None):
    t, d = x.shape
    s_n, _, fs = wg.shape
    tm = _tile(t, TM_FWD)
    n_i = t // tm

    def body(*refs):
        (x_ref, wg_ref, wu_ref, wd_ref, g_ref, b_ref), c_in, (y_ref, yb_ref, z_ref), c_out, (acc_ref,), c_sem = (
            _split_refs(refs, 6, 3, 1, comm))
        i = pl.program_id(0)
        s = pl.program_id(1)
        if comm is not None:
            @pl.when(jnp.logical_and(i == 0, s == 0))
            def _():
                comm.start(c_in, c_out, c_sem)

        xb = x_ref[...].astype(BF16)
        g = _dot(xb, wg_ref[0])
        u = _dot(xb, wu_ref[0])
        a = (g * _sigmoid(g) * u).astype(BF16)
        part = _dot(a, wd_ref[0])

        @pl.when(s == 0)
        def _():
            acc_ref[...] = part

        @pl.when(s > 0)
        def _():
            acc_ref[...] += part

        @pl.when(s == s_n - 1)
        def _():
            z = ALPHA * x_ref[...] + 0.5 * acc_ref[...]
            z_ref[...] = z
            y = _ln_fwd(z, g_ref[...], b_ref[...])
            y_ref[...] = y
            yb_ref[...] = y.astype(BF16)

        if comm is not None:
            @pl.when(jnp.logical_and(i == n_i - 1, s == s_n - 1))
            def _():
                comm.wait(c_in, c_out, c_sem)

    tok = lambda i, s: (i, 0)
    one = lambda i, s: (0, 0)
    return _call_with_comm(
        body, "ffn_fwd", (n_i, s_n),
        [pl.BlockSpec((tm, d), tok),
         pl.BlockSpec((1, d, fs), lambda i, s: (s, 0, 0)),
         pl.BlockSpec((1, d, fs), lambda i, s: (s, 0, 0)),
         pl.BlockSpec((1, fs, d), lambda i, s: (s, 0, 0)),
         pl.BlockSpec((1, d), one), pl.BlockSpec((1, d), one)],
        [pl.BlockSpec((tm, d), tok), pl.BlockSpec((tm, d), tok), pl.BlockSpec((tm, d), tok)],
        [jax.ShapeDtypeStruct((t, d), F32), jax.ShapeDtypeStruct((t, d), BF16), jax.ShapeDtypeStruct((t, d), F32)],
        [pltpu.VMEM((tm, d), F32)],
        [x, wg, wu, wd, ln_g, ln_b], comm)


def ffn_bwd(xb, dzb, wg, wu, wd, comm=None):
    t, d = xb.shape
    s_n, _, fs = wg.shape
    tm = _tile(t, TM_BWD)
    n_i = t // tm

    def body(*refs):
        ((x_ref, dzb_ref, wg_ref, wu_ref, wd_ref), c_in, (dx_ref, dwg_ref, dwu_ref, dwd_ref), c_out,
         (accg, accu, accd), c_sem) = _split_refs(refs, 5, 4, 3, comm)
        s = pl.program_id(0)
        i = pl.program_id(1)
        if comm is not None:
            @pl.when(jnp.logical_and(i == 0, s == 0))
            def _():
                comm.start(c_in, c_out, c_sem)

        @pl.when(i == 0)
        def _():
            accg[...] = jnp.zeros_like(accg)
            accu[...] = jnp.zeros_like(accu)
            accd[...] = jnp.zeros_like(accd)

        x_v = x_ref[...]
        g = _dot(x_v, wg_ref[0])
        u = _dot(x_v, wu_ref[0])
        sg = _sigmoid(g)
        si = g * sg
        a = (si * u).astype(BF16)
        dfb = dzb_ref[...] * 0.5
        da = _dot_nt(dfb, wd_ref[0])
        dgate = (da * u * (sg * (1.0 + g * (1.0 - sg)))).astype(BF16)
        dup = (da * si).astype(BF16)
        dx_ref[0] = (_dot_nt(dgate, wg_ref[0]) + _dot_nt(dup, wu_ref[0])).astype(BF16)
        accg[...] += _dot_tn(x_v, dgate)
        accu[...] += _dot_tn(x_v, dup)
        accd[...] += _dot_tn(a, dfb)

        @pl.when(i == n_i - 1)
        def _():
            dwg_ref[0] = accg[...].astype(BF16)
            dwu_ref[0] = accu[...].astype(BF16)
            dwd_ref[0] = accd[...].astype(BF16)

        if comm is not None:
            @pl.when(jnp.logical_and(i == n_i - 1, s == s_n - 1))
            def _():
                comm.wait(c_in, c_out, c_sem)

    tok = lambda s, i: (i, 0)
    shard = lambda s, i: (s, 0, 0)
    return _call_with_comm(
        body, "ffn_bwd", (s_n, n_i),
        [pl.BlockSpec((tm, d), tok), pl.BlockSpec((tm, d), tok),
         pl.BlockSpec((1, d, fs), shard), pl.BlockSpec((1, d, fs), shard), pl.BlockSpec((1, fs, d), shard)],
        [pl.BlockSpec((1, tm, d), lambda s, i: (s, i, 0)),
         pl.BlockSpec((1, d, fs), shard), pl.BlockSpec((1, d, fs), shard), pl.BlockSpec((1, fs, d), shard)],
        [jax.ShapeDtypeStruct((s_n, t, d), BF16),
         jax.ShapeDtypeStruct((s_n, d, fs), BF16), jax.ShapeDtypeStruct((s_n, d, fs), BF16),
         jax.ShapeDtypeStruct((s_n, fs, d), BF16)],
        [pltpu.VMEM((d, fs), F32), pltpu.VMEM((d, fs), F32), pltpu.VMEM((fs, d), F32)],
        [xb, dzb, wg, wu, wd], comm)


def _sum_parts(base_ref, parts_ref):
    v = base_ref[...]
    if parts_ref is not None:
        for p in range(parts_ref.shape[0]):
            v = v + parts_ref[p].astype(F32)
    return v


def ln_bwd(dy, parts, z, ln_g):
    t, d = dy.shape
    tm = _tile(t, TM_FWD)
    n_p = 0 if parts is None else parts.shape[0]

    def body(*refs):
        if parts is None:
            dy_ref, z_ref, g_ref, dzb_ref, dxr_ref, dg_ref, db_ref = refs
            parts_ref = None
        else:
            dy_ref, parts_ref, z_ref, g_ref, dzb_ref, dxr_ref, dg_ref, db_ref = refs
        i = pl.program_id(0)
        dy_v = _sum_parts(dy_ref, parts_ref)
        dz, xhat = _ln_bwd(dy_v, z_ref[...], g_ref[...])
        dzb_ref[...] = dz.astype(BF16)
        dxr_ref[...] = ALPHA * dz

        @pl.when(i == 0)
        def _():
            dg_ref[...] = jnp.zeros_like(dg_ref)
            db_ref[...] = jnp.zeros_like(db_ref)

        dg_ref[...] += jnp.sum(dy_v * xhat, axis=0, keepdims=True)
        db_ref[...] += jnp.sum(dy_v, axis=0, keepdims=True)

    tok = lambda i: (i, 0)
    one = lambda i: (0, 0)
    in_specs = [pl.BlockSpec((tm, d), tok)]
    args = [dy]
    if parts is not None:
        in_specs.append(pl.BlockSpec((n_p, tm, d), lambda i: (0, i, 0)))
        args.append(parts)
    return pl.pallas_call(
        body, name="ln_bwd" if parts is None else "ln_bwd_parts",
        grid=(t // tm,),
        in_specs=in_specs + [pl.BlockSpec((tm, d), tok), pl.BlockSpec((1, d), one)],
        out_specs=[pl.BlockSpec((tm, d), tok), pl.BlockSpec((tm, d), tok),
                   pl.BlockSpec((1, d), one), pl.BlockSpec((1, d), one)],
        out_shape=[jax.ShapeDtypeStruct((t, d), BF16), jax.ShapeDtypeStruct((t, d), F32),
                   jax.ShapeDtypeStruct((1, d), F32), jax.ShapeDtypeStruct((1, d), F32)],
        compiler_params=_cparams(),
    )(*args, z, ln_g)


def add_parts(base, parts):
    t, d = base.shape
    tm = _tile(t, TM_FWD)

    def body(b_ref, p_ref, o_ref):
        o_ref[...] = _sum_parts(b_ref, p_ref)

    tok = lambda i: (i, 0)
    return pl.pallas_call(
        body, name="add_parts",
        grid=(t // tm,),
        in_specs=[pl.BlockSpec((tm, d), tok), pl.BlockSpec((parts.shape[0], tm, d), lambda i: (0, i, 0))],
        out_specs=pl.BlockSpec((tm, d), tok),
        out_shape=jax.ShapeDtypeStruct((t, d), F32),
        compiler_params=_cparams(),
    )(base, parts)


def loss_and_grad(y, target):
    t, d = y.shape
    tm = _tile(t, TM_FWD)

    def body(y_ref, t_ref, dy_ref, l_ref):
        i = pl.program_id(0)
        e = y_ref[...] - t_ref[...]
        dy_ref[...] = e * (1.0 / d)

        @pl.when(i == 0)
        def _():
            l_ref[...] = jnp.zeros_like(l_ref)

        l_ref[...] += (0.5 / d) * jnp.sum(e * e)

    tok = lambda i: (i, 0)
    return pl.pallas_call(
        body, name="loss",
        grid=(t // tm,),
        in_specs=[pl.BlockSpec((tm, d), tok), pl.BlockSpec((tm, d), tok)],
        out_specs=[pl.BlockSpec((tm, d), tok), pl.BlockSpec((8, 128), lambda i: (0, 0))],
        out_shape=[jax.ShapeDtypeStruct((t, d), F32), jax.ShapeDtypeStruct((8, 128), F32)],
        compiler_params=_cparams(),
    )(y, target)


def matmul(a, w, add=None, out_dtype=F32, name="matmul"):
    t, k = a.shape
    n = w.shape[1]
    tm = _tile(t, TM_FWD)

    def body(*refs):
        if add is None:
            a_ref, w_ref, o_ref = refs
        else:
            a_ref, w_ref, add_ref, o_ref = refs
        o = _dot(a_ref[...].astype(BF16), w_ref[...])
        if add is not None:
            o = o + add_ref[...]
        o_ref[...] = o.astype(out_dtype)

    tok = lambda i: (i, 0)
    in_specs = [pl.BlockSpec((tm, k), tok), pl.BlockSpec((k, n), lambda i: (0, 0))]
    args = [a, w]
    if add is not None:
        in_specs.append(pl.BlockSpec((tm, n), tok))
        args.append(add)
    return pl.pallas_call(
        body, name=name,
        grid=(t // tm,),
        in_specs=in_specs,
        out_specs=pl.BlockSpec((tm, n), tok),
        out_shape=jax.ShapeDtypeStruct((t, n), out_dtype),
        compiler_params=_cparams(),
    )(*args)


def matmul_tn(a, b, name="matmul_tn"):
    t, k = a.shape
    n = b.shape[1]
    tm = _tile(t, TM_FWD)

    def body(a_ref, b_ref, o_ref):
        i = pl.program_id(0)

        @pl.when(i == 0)
        def _():
            o_ref[...] = jnp.zeros_like(o_ref)

        o_ref[...] += _dot_tn(a_ref[...].astype(BF16), b_ref[...].astype(BF16))

    tok = lambda i: (i, 0)
    return pl.pallas_call(
        body, name=name,
        grid=(t // tm,),
        in_specs=[pl.BlockSpec((tm, k), tok), pl.BlockSpec((tm, n), tok)],
        out_specs=pl.BlockSpec((k, n), lambda i: (0, 0)),
        out_shape=jax.ShapeDtypeStruct((k, n), F32),
        compiler_params=_cparams(),
    )(a, b)


def out_proj_ln(ycat, w_out, h, ln_g, ln_b):
    t, d = h.shape
    k = ycat.shape[1]
    tm = _tile(t, TM_FWD)

    def body(yc_ref, w_ref, h_ref, g_ref, b_ref, y_ref, yb_ref, z_ref):
        z = ALPHA * h_ref[...] + _dot(yc_ref[...], w_ref[...])
        z_ref[...] = z
        y = _ln_fwd(z, g_ref[...], b_ref[...])
        y_ref[...] = y
        yb_ref[...] = y.astype(BF16)

    tok = lambda i: (i, 0)
    one = lambda i: (0, 0)
    return pl.pallas_call(
        body, name="out_proj_ln",
        grid=(t // tm,),
        in_specs=[pl.BlockSpec((tm, k), tok), pl.BlockSpec((k, d), one), pl.BlockSpec((tm, d), tok),
                  pl.BlockSpec((1, d), one), pl.BlockSpec((1, d), one)],
        out_specs=[pl.BlockSpec((tm, d), tok), pl.BlockSpec((tm, d), tok), pl.BlockSpec((tm, d), tok)],
        out_shape=[jax.ShapeDtypeStruct((t, d), F32), jax.ShapeDtypeStruct((t, d), BF16),
                   jax.ShapeDtypeStruct((t, d), F32)],
        compiler_params=_cparams(),
    )(ycat, w_out, h, ln_g, ln_b)


def _halo_specs(tm, cols, n_rows):
    r = tm // HALO
    last = n_rows // HALO - 1
    return [pl.BlockSpec((HALO, cols), lambda i: (jnp.maximum(i * r - 1, 0), 0)),
            pl.BlockSpec((tm, cols), lambda i: (i, 0)),
            pl.BlockSpec((HALO, cols), lambda i: (jnp.minimum((i + 1) * r, last), 0))]


def _fill_ext(dst, prev_ref, main_ref, next_ref, i, n_i):
    tm = main_ref.shape[0]
    dst[0:HALO, :] = jnp.where(i > 0, prev_ref[...], 0.0)
    dst[HALO:HALO + tm, :] = main_ref[...]
    dst[HALO + tm:HALO + tm + HALO, :] = jnp.where(i < n_i - 1, next_ref[...], 0.0)


def _pool_lane_half():
    lane = lax.broadcasted_iota(jnp.int32, (1, POOL_W), 1)
    return jnp.left_shift(1, lane // 64)


def _pool_inv_count(t0, rows, seq):
    half = _pool_lane_half()
    tpos = t0 + lax.broadcasted_iota(jnp.int32, (rows, 1), 0)
    lo = jnp.maximum(tpos - half, 0)
    hi = jnp.minimum(tpos + half, seq)
    cnt = jnp.maximum(hi - lo, 1)
    return 1.0 / cnt.astype(F32)


def _taps(ext_ref, cols, offsets, tm):
    e = ext_ref.shape[0]
    by_phase = {}
    for o in offsets:
        by_phase.setdefault((HALO + o) % 8, []).append(o)
    for r, group in by_phase.items():
        shifted = ext_ref[r:r + e - 8, cols]
        for o in group:
            start = HALO + o - r
            yield o, shifted[start:start + tm]


def _pool_forward(p_ext, tm, t0, seq):
    half = _pool_lane_half()
    total = jnp.zeros((tm, POOL_W), F32)
    for o, win in _taps(p_ext, slice(C_POOL[0], C_POOL[1]), range(-8, 8), tm):
        m = ((o >= -half) & (o < half)).astype(F32)
        total = total + m * win
    u = p_ext[HALO:HALO + tm, C_POOL[0]:C_POOL[1]]
    return total * _pool_inv_count(t0, tm, seq) - u


def mix_fwd(proj, pw_bd, pool_scale, sconv_w, cconv_w, cconv_b, cnorm_g, cnorm_b):
    t, pc = proj.shape
    tm = _tile(t, TM_MIX)
    n_i = t // tm
    e = tm + 2 * HALO

    def body(pp_ref, pm_ref, pn_ref, pw_ref, ps_ref, sw_ref, cw_ref, cb_ref, cg_ref, cbb_ref,
             yc_ref, a1_ref, p_ext, q_ext, a0_ext):
        i = pl.program_id(0)
        _fill_ext(p_ext, pp_ref, pm_ref, pn_ref, i, n_i)
        pooled = _pool_forward(p_ext, tm, i * tm, t)
        y_a = _dot(pooled.astype(BF16), pw_ref[...]) * ps_ref[...]
        yc_ref[:, 0:256] = y_a.astype(BF16)
        q_ext[...] = p_ext[:, C_GC[0]:C_GC[1]] * p_ext[:, C_V[0]:C_V[1]]
        conv = jnp.zeros((tm, CONV_W), F32)
        for k in range(SCONV_K):
            conv = conv + sw_ref[k:k + 1, :] * q_ext[HALO + k - 1:HALO + k - 1 + tm, :]
        y_b = p_ext[HALO:HALO + tm, C_GB[0]:C_GB[1]] * conv
        yc_ref[:, 256:640] = y_b.astype(BF16)
        a0_ext[...] = p_ext[:, C_CV[0]:C_CV[1]] * _sigmoid(p_ext[:, C_CG[0]:C_CG[1]])
        a1 = jnp.zeros((tm, CONV_W), F32) + cb_ref[...]
        for o, win in _taps(a0_ext, slice(None), range(-15, 16), tm):
            a1 = a1 + cw_ref[o + 15:o + 16, :] * win
        a1_ref[...] = a1
        ln = _ln_fwd(a1, cg_ref[...], cbb_ref[...])
        yc_ref[:, 640:1024] = (ln * _sigmoid(ln)).astype(BF16)

    one = lambda i: (0, 0)
    return pl.pallas_call(
        body, name="mix_fwd",
        grid=(n_i,),
        in_specs=_halo_specs(tm, pc, t) + [
            pl.BlockSpec((POOL_W, POOL_W), one), pl.BlockSpec((1, POOL_W), one),
            pl.BlockSpec((8, CONV_W), one), pl.BlockSpec((32, CONV_W), one),
            pl.BlockSpec((1, CONV_W), one), pl.BlockSpec((1, CONV_W), one), pl.BlockSpec((1, CONV_W), one)],
        out_specs=[pl.BlockSpec((tm, 1024), lambda i: (i, 0)), pl.BlockSpec((tm, CONV_W), lambda i: (i, 0))],
        out_shape=[jax.ShapeDtypeStruct((t, 1024), BF16), jax.ShapeDtypeStruct((t, CONV_W), F32)],
        scratch_shapes=[pltpu.VMEM((e, pc), F32), pltpu.VMEM((e, CONV_W), F32), pltpu.VMEM((e, CONV_W), F32)],
        compiler_params=_cparams(),
    )(proj, proj, proj, pw_bd, pool_scale, sconv_w, cconv_w, cconv_b, cnorm_g, cnorm_b)


def mix_bwd(proj, dycat, a1, pw_bd, pw_bd_t, pool_scale, sconv_w, cconv_w, cnorm_g, cnorm_b):
    t, pc = proj.shape
    tm = _tile(t, TM_MIX)
    n_i = t // tm
    e = tm + 2 * HALO

    def body(pp_ref, pm_ref, pn_ref, dp_ref, dm_ref, dn_ref, ap_ref, am_ref, an_ref,
             pw_ref, pwt_ref, ps_ref, sw_ref, cw_ref, cg_ref, cbb_ref,
             dproj_ref, dpw_ref, dps_ref, dsw_ref, dcw_ref, dcb_ref, dcg_ref, dcbb_ref,
             p_ext, dy_ext, a1_ext, a0_ext, da1_ext, q_ext, dc_ext, dpn_ext):
        i = pl.program_id(0)
        main = slice(HALO, HALO + tm)

        @pl.when(i == 0)
        def _():
            for r in (dpw_ref, dps_ref, dsw_ref, dcw_ref, dcb_ref, dcg_ref, dcbb_ref):
                r[...] = jnp.zeros_like(r)

        _fill_ext(p_ext, pp_ref, pm_ref, pn_ref, i, n_i)
        _fill_ext(dy_ext, dp_ref, dm_ref, dn_ref, i, n_i)
        _fill_ext(a1_ext, ap_ref, am_ref, an_ref, i, n_i)

        sig_cg = _sigmoid(p_ext[:, C_CG[0]:C_CG[1]])
        a0_ext[...] = p_ext[:, C_CV[0]:C_CV[1]] * sig_cg
        a1_v = a1_ext[...]
        mu = jnp.mean(a1_v, axis=-1, keepdims=True)
        zc = a1_v - mu
        var = jnp.mean(zc * zc, axis=-1, keepdims=True)
        rstd = lax.rsqrt(var + LN_EPS)
        xhat = zc * rstd
        ln = xhat * cg_ref[...] + cbb_ref[...]
        sl = _sigmoid(ln)
        dln = dy_ext[:, 640:1024] * (sl * (1.0 + ln * (1.0 - sl)))
        dcg_ref[...] += jnp.sum((dln * xhat)[main], axis=0, keepdims=True)
        dcbb_ref[...] += jnp.sum(dln[main], axis=0, keepdims=True)
        dxh = dln * cg_ref[...]
        m1 = jnp.mean(dxh, axis=-1, keepdims=True)
        m2 = jnp.mean(dxh * xhat, axis=-1, keepdims=True)
        da1 = rstd * (dxh - m1 - xhat * m2)
        da1_ext[...] = da1
        da1_m = da1[main]
        dcb_ref[...] += jnp.sum(da1_m, axis=0, keepdims=True)
        da0 = jnp.zeros((tm, CONV_W), F32)
        for o, win in _taps(da1_ext, slice(None), range(-15, 16), tm):
            da0 = da0 + cw_ref[15 - o:16 - o, :] * win
        for o, win in _taps(a0_ext, slice(None), range(-15, 16), tm):
            dcw_ref[o + 15:o + 16, :] += jnp.sum(da1_m * win, axis=0, keepdims=True)
        sig_m = sig_cg[main]
        cv_m = p_ext[main, C_CV[0]:C_CV[1]]
        dproj_ref[:, C_CV[0]:C_CV[1]] = (da0 * sig_m).astype(BF16)
        dproj_ref[:, C_CG[0]:C_CG[1]] = (da0 * cv_m * sig_m * (1.0 - sig_m)).astype(BF16)

        q_ext[...] = p_ext[:, C_GC[0]:C_GC[1]] * p_ext[:, C_V[0]:C_V[1]]
        dc_ext[...] = dy_ext[:, 256:640] * p_ext[:, C_GB[0]:C_GB[1]]
        dc_m = dc_ext[main, :]
        conv = jnp.zeros((tm, CONV_W), F32)
        dq = jnp.zeros((tm, CONV_W), F32)
        for k in range(SCONV_K):
            q_k = q_ext[HALO + k - 1:HALO + k - 1 + tm, :]
            conv = conv + sw_ref[k:k + 1, :] * q_k
            dq = dq + sw_ref[k:k + 1, :] * dc_ext[HALO - k + 1:HALO - k + 1 + tm, :]
            dsw_ref[k:k + 1, :] += jnp.sum(dc_m * q_k, axis=0, keepdims=True)
        dproj_ref[:, C_GB[0]:C_GB[1]] = (dy_ext[main, 256:640] * conv).astype(BF16)
        dproj_ref[:, C_GC[0]:C_GC[1]] = (dq * p_ext[main, C_V[0]:C_V[1]]).astype(BF16)
        dproj_ref[:, C_V[0]:C_V[1]] = (dq * p_ext[main, C_GC[0]:C_GC[1]]).astype(BF16)

        t0 = i * tm
        dya = dy_ext[:, 0:256] * ps_ref[...]
        dpooled = _dot(dya.astype(BF16), pwt_ref[...])
        dpn_ext[...] = dpooled * _pool_inv_count(t0 - HALO, e, t)
        half = _pool_lane_half()
        du = jnp.zeros((tm, POOL_W), F32)
        for o, win in _taps(dpn_ext, slice(None), range(-7, 9), tm):
            m = ((o > -half) & (o <= half)).astype(F32)
            du = du + m * win
        dproj_ref[:, C_POOL[0]:C_POOL[1]] = (du - dpooled[main]).astype(BF16)
        pooled = _pool_forward(p_ext, tm, t0, t)
        pooled_b = pooled.astype(BF16)
        ya_pre = _dot(pooled_b, pw_ref[...])
        dps_ref[...] += jnp.sum(dy_ext[main, 0:256] * ya_pre, axis=0, keepdims=True)
        dpw_ref[...] += _dot_tn(pooled_b, dya[main].astype(BF16))

    one = lambda i: (0, 0)
    small = [((POOL_W, POOL_W), F32), ((1, POOL_W), F32), ((8, CONV_W), F32), ((32, CONV_W), F32),
             ((1, CONV_W), F32), ((1, CONV_W), F32), ((1, CONV_W), F32)]
    return pl.pallas_call(
        body, name="mix_bwd",
        grid=(n_i,),
        in_specs=_halo_specs(tm, pc, t) + _halo_specs(tm, 1024, t) + _halo_specs(tm, CONV_W, t) + [
            pl.BlockSpec((POOL_W, POOL_W), one), pl.BlockSpec((POOL_W, POOL_W), one), pl.BlockSpec((1, POOL_W), one),
            pl.BlockSpec((8, CONV_W), one), pl.BlockSpec((32, CONV_W), one),
            pl.BlockSpec((1, CONV_W), one), pl.BlockSpec((1, CONV_W), one)],
        out_specs=[pl.BlockSpec((tm, pc), lambda i: (i, 0))] + [pl.BlockSpec(s, one) for s, _ in small],
        out_shape=[jax.ShapeDtypeStruct((t, pc), BF16)] + [jax.ShapeDtypeStruct(s, dt) for s, dt in small],
        scratch_shapes=[pltpu.VMEM((e, pc), F32), pltpu.VMEM((e, 1024), F32), pltpu.VMEM((e, CONV_W), F32),
                        pltpu.VMEM((e, CONV_W), F32), pltpu.VMEM((e, CONV_W), F32), pltpu.VMEM((e, CONV_W), F32),
                        pltpu.VMEM((e, CONV_W), F32), pltpu.VMEM((e, POOL_W), F32)],
        compiler_params=_cparams(),
    )(proj, proj, proj, dycat, dycat, dycat, a1, a1, a1,
      pw_bd, pw_bd_t, pool_scale, sconv_w, cconv_w, cnorm_g, cnorm_b)


def _mesh_pos():
    return lax.axis_index("x"), lax.axis_index("y"), lax.axis_index("c")


def _flip(v, f):
    return 1 - v if f else v


class _Comm:
    def __init__(self, kind, arrs, lands=None, layer=0):
        self.kind = kind
        self.arrs = list(arrs)
        self.n = len(self.arrs)
        self.lands = None if lands is None else list(lands)
        self.layer = layer
        if kind == "gather":
            self.flips = [(1, 0, 0), (0, 1, 0), (1, 1, 0)]
            self.out_shape = [jax.ShapeDtypeStruct((N_CHIPS,) + a.shape, a.dtype) for a in self.arrs]
        else:
            self.flips = [(fx, fy, fc) for fx in (0, 1) for fy in (0, 1) for fc in (0, 1)][1:]
            self.out_shape = [jax.ShapeDtypeStruct(b.shape, b.dtype) for b in self.lands]
        self.n_peer = len(self.flips)

    def operands(self):
        return self.arrs + (self.lands or [])

    def scratch(self):
        return [pltpu.SemaphoreType.DMA((self.n * self.n_peer,)), pltpu.SemaphoreType.DMA((self.n * self.n_peer,)),
                pltpu.SemaphoreType.DMA((self.n,))]

    def aliases(self, in_off, out_off):
        if self.lands is None:
            return {}
        return {in_off + self.n + j: out_off + j for j in range(self.n)}

    def _copies(self, ins, outs, sems):
        send_sems, recv_sems, loc_sems = sems
        x, y, c = _mesh_pos()
        local, sends, recvs = [], [], []
        for a in range(self.n):
            def src(px, py):
                return ins[a] if self.kind == "gather" else ins[a].at[2 * px + py]

            def dst(px, py, pc):
                if self.kind == "gather":
                    return outs[a].at[2 * px + py]
                return outs[a].at[4 * px + 2 * py + pc, self.layer]

            local.append(pltpu.make_async_copy(src(x, y), dst(x, y, c), loc_sems.at[a]))
            for k, (fx, fy, fc) in enumerate(self.flips):
                px, py, pc = _flip(x, fx), _flip(y, fy), _flip(c, fc)
                sem = a * self.n_peer + k
                sends.append(pltpu.make_async_remote_copy(
                    src_ref=src(px, py), dst_ref=dst(x, y, c),
                    send_sem=send_sems.at[sem], recv_sem=recv_sems.at[sem],
                    device_id=(px, py, pc), device_id_type=MESH))
                recvs.append(pltpu.make_async_remote_copy(
                    src_ref=src(px, py), dst_ref=dst(px, py, pc),
                    send_sem=send_sems.at[sem], recv_sem=recv_sems.at[sem],
                    device_id=(px, py, pc), device_id_type=MESH))
        return local, sends, recvs

    def start(self, ins, outs, sems):
        local, sends, _ = self._copies(ins, outs, sems)
        for cp in local + sends:
            cp.start()

    def wait(self, ins, outs, sems):
        local, _, recvs = self._copies(ins, outs, sems)
        for cp in local:
            cp.wait()
        for cp in recvs:
            cp.wait()


def _split_refs(refs, n_in, n_out, n_scr, comm):
    c_in = len(comm.operands()) if comm is not None else 0
    c_out = comm.n if comm is not None else 0
    cuts = [n_in, c_in, n_out, c_out, n_scr]
    out, pos = [], 0
    for m in cuts:
        out.append(refs[pos:pos + m])
        pos += m
    out.append(refs[pos:])
    return out


def _call_with_comm(body, name, grid, in_specs, out_specs, out_shape, scratch, args, comm):
    hbm = pl.BlockSpec(memory_space=pl.ANY)
    aliases = {}
    if comm is not None:
        aliases = comm.aliases(len(in_specs), len(out_specs))
        in_specs = in_specs + [hbm] * len(comm.operands())
        out_specs = out_specs + [hbm] * comm.n
        out_shape = out_shape + comm.out_shape
        scratch = scratch + comm.scratch()
        args = args + comm.operands()
        name = name + "_" + comm.kind
    res = pl.pallas_call(
        body, name=name, grid=grid, in_specs=in_specs, out_specs=out_specs, out_shape=out_shape,
        scratch_shapes=scratch, input_output_aliases=aliases, compiler_params=_cparams(),
    )(*args)
    if comm is None:
        return res, None
    return res[:len(res) - comm.n], res[len(res) - comm.n:]


def run_comm(comm):
    def body(*refs):
        _, c_in, _, c_out, _, c_sem = _split_refs(refs, 0, 0, 0, comm)
        comm.start(c_in, c_out, c_sem)
        comm.wait(c_in, c_out, c_sem)

    hbm = pl.BlockSpec(memory_space=pl.ANY)
    return pl.pallas_call(
        body, name="comm_" + comm.kind,
        in_specs=[hbm] * len(comm.operands()), out_specs=[hbm] * comm.n, out_shape=comm.out_shape,
        scratch_shapes=comm.scratch(), input_output_aliases=comm.aliases(0, 0),
    )(*comm.operands())


def exchange(arrs, per_chip):
    n = len(arrs)
    flips = [(fx, fy, fc) for fx in (0, 1) for fy in (0, 1) for fc in (0, 1)][1:]

    def body(*refs):
        ins, outs = refs[:n], refs[n:2 * n]
        send_sems, recv_sems, loc_sems = refs[2 * n:]
        x, y, c = _mesh_pos()
        me = 4 * x + 2 * y + c
        local = []
        remote = []
        for a in range(n):
            def part(px, py):
                return ins[a].at[2 * px + py] if per_chip else ins[a]

            cp = pltpu.make_async_copy(part(x, y), outs[a].at[me], loc_sems.at[a])
            cp.start()
            local.append(cp)
            for k, (fx, fy, fc) in enumerate(flips):
                px, py, pc = _flip(x, fx), _flip(y, fy), _flip(c, fc)
                sem = a * 7 + k
                rc = pltpu.make_async_remote_copy(
                    src_ref=part(px, py), dst_ref=outs[a].at[me],
                    send_sem=send_sems.at[sem], recv_sem=recv_sems.at[sem],
                    device_id=(px, py, pc), device_id_type=MESH)
                rc.start()
                remote.append(pltpu.make_async_remote_copy(
                    src_ref=part(px, py), dst_ref=outs[a].at[4 * px + 2 * py + pc],
                    send_sem=send_sems.at[sem], recv_sem=recv_sems.at[sem],
                    device_id=(px, py, pc), device_id_type=MESH))
        for cp in local:
            cp.wait()
        for rc in remote:
            rc.wait()

    hbm = pl.BlockSpec(memory_space=pl.ANY)
    shapes = [a.shape[1:] if per_chip else a.shape for a in arrs]
    return pl.pallas_call(
        body, name="exchange_per_chip" if per_chip else "exchange_all",
        in_specs=[hbm] * n, out_specs=[hbm] * n,
        out_shape=[jax.ShapeDtypeStruct((N_DEV,) + s, a.dtype) for s, a in zip(shapes, arrs)],
        scratch_shapes=[pltpu.SemaphoreType.DMA((7 * n,)), pltpu.SemaphoreType.DMA((7 * n,)),
                        pltpu.SemaphoreType.DMA((n,))],
    )(*arrs)


def adamw(parts, w, m, v, name):
    k_n, r, c = parts.shape
    tr = r
    for cand in (512, 256, 128, 64, 32, 16, 8):
        if r % cand == 0:
            tr = cand
            break

    def body(p_ref, w_ref, m_ref, v_ref, g_ref, d_ref, mo_ref, vo_ref):
        g = p_ref[0].astype(F32)
        for k in range(1, k_n):
            g = g + p_ref[k].astype(F32)
        m_new = ADAM_B1 * m_ref[...] + (1.0 - ADAM_B1) * g
        v_new = ADAM_B2 * v_ref[...] + (1.0 - ADAM_B2) * (g * g)
        m_hat = m_new / (1.0 - ADAM_B1 ** ADAM_STEP)
        v_hat = v_new / (1.0 - ADAM_B2 ** ADAM_STEP)
        g_ref[...] = g
        d_ref[...] = -ADAM_LR * (m_hat / (jnp.sqrt(v_hat) + ADAM_EPS) + ADAM_WD * w_ref[...])
        mo_ref[...] = m_new
        vo_ref[...] = v_new

    blk = pl.BlockSpec((tr, c), lambda i: (i, 0))
    return pl.pallas_call(
        body, name=name,
        grid=(r // tr,),
        in_specs=[pl.BlockSpec((k_n, tr, c), lambda i: (0, i, 0)), blk, blk, blk],
        out_specs=[blk, blk, blk, blk],
        out_shape=[jax.ShapeDtypeStruct((r, c), F32)] * 4,
        compiler_params=_cparams(),
    )(parts, w, m, v)


def _block_diag(pool_w):
    out = jnp.zeros((POOL_W, POOL_W), pool_w.dtype)
    for g in range(4):
        out = lax.dynamic_update_slice(out, pool_w[g], (64 * g, 64 * g))
    return out


def _pad_rows(a, rows):
    return jnp.pad(a, ((0, rows - a.shape[0]), (0, 0)))


def kernel(x, ln1_g, ln1_b, ffn1_w_gate, ffn1_w_up, ffn1_w_down, mix_w_in, pool_w, pool_scale, sconv_w, cconv_w, cconv_b, cnorm_g, cnorm_b, mix_w_out, ln2_g, ln2_b, ffn2_w_gate, ffn2_w_up, ffn2_w_down, ln3_g, ln3_b, loss_target, m_ln1_g, m_ln1_b, m_ffn1_w_gate, m_ffn1_w_up, m_ffn1_w_down, m_mix_w_in, m_pool_w, m_pool_scale, m_sconv_w, m_cconv_w, m_cconv_b, m_cnorm_g, m_cnorm_b, m_mix_w_out, m_ln2_g, m_ln2_b, m_ffn2_w_gate, m_ffn2_w_up, m_ffn2_w_down, m_ln3_g, m_ln3_b, v_ln1_g, v_ln1_b, v_ffn1_w_gate, v_ffn1_w_up, v_ffn1_w_down, v_mix_w_in, v_pool_w, v_pool_scale, v_sconv_w, v_cconv_w, v_cconv_b, v_cnorm_g, v_cnorm_b, v_mix_w_out, v_ln2_g, v_ln2_b, v_ffn2_w_gate, v_ffn2_w_up, v_ffn2_w_down, v_ln3_g, v_ln3_b):
    names = ['ln1_g', 'ln1_b', 'ffn1_w_gate', 'ffn1_w_up', 'ffn1_w_down', 'mix_w_in', 'pool_w', 'pool_scale',
             'sconv_w', 'cconv_w', 'cconv_b', 'cnorm_g', 'cnorm_b', 'mix_w_out', 'ln2_g', 'ln2_b',
             'ffn2_w_gate', 'ffn2_w_up', 'ffn2_w_down', 'ln3_g', 'ln3_b']
    w = dict(zip(names, (ln1_g, ln1_b, ffn1_w_gate, ffn1_w_up, ffn1_w_down, mix_w_in, pool_w, pool_scale, sconv_w,
                         cconv_w, cconv_b, cnorm_g, cnorm_b, mix_w_out, ln2_g, ln2_b, ffn2_w_gate, ffn2_w_up,
                         ffn2_w_down, ln3_g, ln3_b)))
    mom_m = dict(zip(names, (m_ln1_g, m_ln1_b, m_ffn1_w_gate, m_ffn1_w_up, m_ffn1_w_down, m_mix_w_in, m_pool_w,
                             m_pool_scale, m_sconv_w, m_cconv_w, m_cconv_b, m_cnorm_g, m_cnorm_b, m_mix_w_out,
                             m_ln2_g, m_ln2_b, m_ffn2_w_gate, m_ffn2_w_up, m_ffn2_w_down, m_ln3_g, m_ln3_b)))
    mom_v = dict(zip(names, (v_ln1_g, v_ln1_b, v_ffn1_w_gate, v_ffn1_w_up, v_ffn1_w_down, v_mix_w_in, v_pool_w,
                             v_pool_scale, v_sconv_w, v_cconv_w, v_cconv_b, v_cnorm_g, v_cnorm_b, v_mix_w_out,
                             v_ln2_g, v_ln2_b, v_ffn2_w_gate, v_ffn2_w_up, v_ffn2_w_down, v_ln3_g, v_ln3_b)))
    big = ['ffn1_w_gate', 'ffn1_w_up', 'ffn1_w_down', 'mix_w_in', 'mix_w_out',
           'ffn2_w_gate', 'ffn2_w_up', 'ffn2_w_down']
    n_l = ln1_g.shape[0]
    d = x.shape[-1]
    fs = ffn1_w_gate.shape[-1]
    ws_in = mix_w_in.shape[-1]
    cs = sconv_w.shape[-1]
    chip = 2 * lax.axis_index("x") + lax.axis_index("y")

    keys_a = ['ffn2_w_gate', 'ffn2_w_up', 'ffn2_w_down', 'mix_w_in', 'mix_w_out']
    keys_b = ['ffn1_w_gate', 'ffn1_w_up', 'ffn1_w_down']

    def shards_b(l):
        return [w[k][l].astype(BF16) for k in keys_b]

    def shards_a(l):
        conv_loc = jnp.concatenate([sconv_w[l], cconv_w[l]], axis=0)
        return [w[k][l].astype(BF16) for k in keys_a] + [conv_loc]

    def cols(a):
        return jnp.transpose(a, (1, 0, 2)).reshape(a.shape[1], -1)

    def layer_weights(l, got_b, got_a):
        gw = dict(zip(keys_b + keys_a, list(got_b) + list(got_a[:-1])))
        conv_all = jnp.transpose(got_a[-1], (1, 0, 2)).reshape(SCONV_K + CCONV_K, N_CHIPS * cs)
        return dict(
            g1=gw['ffn1_w_gate'], u1=gw['ffn1_w_up'], d1=gw['ffn1_w_down'],
            g2=gw['ffn2_w_gate'], u2=gw['ffn2_w_up'], d2=gw['ffn2_w_down'],
            w_in=cols(gw['mix_w_in']),
            w_out=gw['mix_w_out'].reshape(-1, d),
            pw=_block_diag(pool_w[l]).astype(BF16),
            ps=pool_scale[l][None], sw=_pad_rows(conv_all[:SCONV_K], 8), cw=_pad_rows(conv_all[SCONV_K:], 32),
            cb=cconv_b[l][None], cg=cnorm_g[l][None], cbb=cnorm_b[l][None],
        )

    h = x[0]
    hb = h.astype(BF16)
    target = loss_target[0]
    saved = []
    layer_w = []
    got_b = run_comm(_Comm("gather", shards_b(0)))
    for l in range(n_l):
        (y1, y1b, z1), got_a = ffn_fwd(h, got_b[0], got_b[1], got_b[2], ln1_g[l][None], ln1_b[l][None],
                                       _Comm("gather", shards_a(l)))
        lw = layer_weights(l, got_b, got_a)
        layer_w.append(lw)
        proj = matmul(y1b, lw['w_in'], name="proj")
        ycat, a1 = mix_fwd(proj, lw['pw'], lw['ps'], lw['sw'], lw['cw'], lw['cb'], lw['cg'], lw['cbb'])
        y2, y2b, z2 = out_proj_ln(ycat, lw['w_out'], y1, ln2_g[l][None], ln2_b[l][None])
        comm = _Comm("gather", shards_b(l + 1)) if l + 1 < n_l else None
        (y3, y3b, z3), got_b = ffn_fwd(y2, lw['g2'], lw['u2'], lw['d2'], ln3_g[l][None], ln3_b[l][None], comm)
        saved.append(dict(x0b=hb, z1=z1, y1b=y1b, proj=proj, ycat=ycat, a1=a1, z2=z2, y2b=y2b, z3=z3))
        h, hb = y3, y3b

    dy, loss_blk = loss_and_grad(h, target)
    loss = lax.psum(loss_blk[0, 0], ("x", "y", "c"))

    g_loc = {k: [None] * n_l for k in names if k not in big}
    lands = {k: lax.empty((N_DEV, n_l) + w[k].shape[1:], BF16) for k in big}

    def scatter(keys, arrs, layer):
        return _Comm("scatter", arrs, [lands[k] for k in keys], layer)

    pend_b = None
    parts = None
    for l in reversed(range(n_l)):
        lw, sv = layer_w[l], saved[l]
        dzb, dxr, g_loc['ln3_g'][l], g_loc['ln3_b'][l] = ln_bwd(dy, parts, sv['z3'], ln3_g[l][None])
        comm = None if pend_b is None else scatter(keys_b, pend_b, l + 1)
        (parts, dg2, du2, dd2), landed = ffn_bwd(sv['y2b'], dzb, lw['g2'], lw['u2'], lw['d2'], comm)
        if comm is not None:
            lands.update(zip(keys_b, landed))
        dzb, dxr, g_loc['ln2_g'][l], g_loc['ln2_b'][l] = ln_bwd(dxr, parts, sv['z2'], ln2_g[l][None])
        dycat = matmul(dzb, lw['w_out'].T, name="dycat")
        dw_out = matmul_tn(sv['ycat'], dzb, name="dw_out")
        (dproj, dpw, g_loc['pool_scale'][l], dsw, dcw, g_loc['cconv_b'][l], g_loc['cnorm_g'][l],
         g_loc['cnorm_b'][l]) = mix_bwd(sv['proj'], dycat, sv['a1'], lw['pw'], lw['pw'].T, lw['ps'],
                                        lw['sw'], lw['cw'], lw['cg'], lw['cbb'])
        g_loc['pool_w'][l] = jnp.stack([dpw[64 * g:64 * g + 64, 64 * g:64 * g + 64] for g in range(4)])
        g_loc['sconv_w'][l] = dsw[:SCONV_K]
        g_loc['cconv_w'][l] = dcw[:CCONV_K]
        dy = matmul(dproj, lw['w_in'].T, add=dxr, name="dmix_in")
        dw_in = matmul_tn(sv['y1b'], dproj, name="dw_in")
        dw_in_c = jnp.transpose(dw_in.reshape(d, N_CHIPS, ws_in), (1, 0, 2)).astype(BF16)
        dw_out_c = dw_out.reshape(N_CHIPS, -1, d).astype(BF16)
        dzb, dxr, g_loc['ln1_g'][l], g_loc['ln1_b'][l] = ln_bwd(dy, None, sv['z1'], ln1_g[l][None])
        comm = scatter(keys_a, [dg2, du2, dd2, dw_in_c, dw_out_c], l)
        (parts, dg1, du1, dd1), landed = ffn_bwd(sv['x0b'], dzb, lw['g1'], lw['u1'], lw['d1'], comm)
        lands.update(zip(keys_a, landed))
        pend_b = [dg1, du1, dd1]
        dy = dxr
    grad_x = add_parts(dy, parts)[None]
    lands.update(zip(keys_b, run_comm(scatter(keys_b, pend_b, 0))))
    parts_big = lands

    small = [k for k in names if k not in big]
    small_full = {}
    for k in small:
        a = jnp.stack(g_loc[k])
        small_full[k] = a.reshape(n_l, -1) if a.shape[1] == 1 else a
    flat = jnp.concatenate([small_full[k].reshape(-1) for k in small])
    n_flat = flat.shape[0]
    rows = -(-n_flat // (SMALL_ROWS * 128)) * SMALL_ROWS
    flat = jnp.pad(flat, (0, rows * 128 - n_flat)).reshape(rows, 128)
    parts_small = exchange([flat], per_chip=False)[0]

    out_g, out_d, out_m, out_v = {}, {}, {}, {}
    for k in big:
        shp = w[k].shape
        r = shp[0] * shp[1]
        res = adamw(parts_big[k].reshape(N_DEV, r, shp[2]), w[k].reshape(r, shp[2]),
                    mom_m[k].reshape(r, shp[2]), mom_v[k].reshape(r, shp[2]), name="adamw_" + k)
        out_g[k], out_d[k], out_m[k], out_v[k] = [o.reshape(shp) for o in res]

    zeros = jnp.zeros((rows, 128), F32)
    g_sum = adamw(parts_small, zeros, zeros, zeros, name="sum_small")[0].reshape(-1)
    off = 0
    for k in small:
        full = small_full[k]
        g = g_sum[off:off + full.size].reshape(full.shape)
        off += full.size
        if k in ('sconv_w', 'cconv_w'):
            g = lax.dynamic_slice_in_dim(g, chip * cs, cs, axis=2)
        out_g[k] = g.reshape(w[k].shape)

    def pack(dct):
        f = jnp.concatenate([dct[k].reshape(-1) for k in small])
        r2 = -(-f.shape[0] // (SMALL_ROWS * 128)) * SMALL_ROWS
        return jnp.pad(f, (0, r2 * 128 - f.shape[0])).reshape(r2, 128), f.shape[0]

    gp, n_small = pack(out_g)
    wp, _ = pack(w)
    mp, _ = pack(mom_m)
    vp, _ = pack(mom_v)
    _, dp, mo, vo = adamw(gp[None], wp, mp, vp, name="adamw_small")
    off = 0
    for k in small:
        sz = w[k].size
        out_d[k] = dp.reshape(-1)[off:off + sz].reshape(w[k].shape)
        out_m[k] = mo.reshape(-1)[off:off + sz].reshape(w[k].shape)
        out_v[k] = vo.reshape(-1)[off:off + sz].reshape(w[k].shape)
        off += sz

    return (loss, grad_x, *[out_g[k] for k in names], *[out_d[k] for k in names],
            *[out_m[k] for k in names], *[out_v[k] for k in names])
```

```python
import jax
import jax.numpy as jnp
from jax import lax
from jax.experimental import pallas as pl
from jax.experimental.pallas import tpu as pltpu

F32 = jnp.float32
BF16 = jnp.bfloat16

DEPTH = 4
ALPHA = (2.0 * DEPTH) ** 0.25
LN_EPS = 1e-5
POOL_W = 256
CONV_W = 384
SCONV_K = 3
CCONV_K = 31
C_POOL = (0, 256)
C_GB = (256, 640)
C_GC = (640, 1024)
C_V = (1024, 1408)
C_CV = (1408, 1792)
C_CG = (1792, 2176)

ADAM_LR = 0.001
ADAM_B1 = 0.9
ADAM_B2 = 0.999
ADAM_EPS = 1e-08
ADAM_WD = 0.01
ADAM_STEP = 10

N_CHIPS = 4
N_DEV = 8
MESH = pl.DeviceIdType.MESH

TM_FWD = 512
TM_BWD = 512
TM_MIX = 256
HALO = 32
SMALL_ROWS = 256
VMEM_LIMIT = 56 * 1024 * 1024


def _cparams():
    return pltpu.CompilerParams(vmem_limit_bytes=VMEM_LIMIT)


def _sigmoid(v):
    return 1.0 / (1.0 + jnp.exp(-v))


def _dot(a, b):
    return jnp.dot(a, b, preferred_element_type=F32)


def _dot_nt(a, b):
    return lax.dot_general(a, b, (((1,), (1,)), ((), ())), preferred_element_type=F32)


def _dot_tn(a, b):
    return lax.dot_general(a, b, (((0,), (0,)), ((), ())), preferred_element_type=F32)


def _ln_fwd(z, g, b):
    mu = jnp.mean(z, axis=-1, keepdims=True)
    zc = z - mu
    var = jnp.mean(zc * zc, axis=-1, keepdims=True)
    return zc * lax.rsqrt(var + LN_EPS) * g + b


def _ln_bwd(dy, z, g):
    mu = jnp.mean(z, axis=-1, keepdims=True)
    zc = z - mu
    var = jnp.mean(zc * zc, axis=-1, keepdims=True)
    rstd = lax.rsqrt(var + LN_EPS)
    xhat = zc * rstd
    dxh = dy * g
    m1 = jnp.mean(dxh, axis=-1, keepdims=True)
    m2 = jnp.mean(dxh * xhat, axis=-1, keepdims=True)
    return rstd * (dxh - m1 - xhat * m2), xhat


def _tile(t, tm):
    tm = min(tm, t)
    assert t % tm == 0, (t, tm)
    return tm


def ffn_fwd(x, xb, wg, wu, wd, ln_g, ln_b, comm=None):
    t, d = x.shape
    s_n, _, fs = wg.shape
    tm = _tile(t, TM_FWD)
    n_i = t // tm

    def body(*refs):
        ((x_ref, xb_ref, wg_ref, wu_ref, wd_ref, g_ref, b_ref), c_in, (y_ref, yb_ref, z_ref), c_out, (acc_ref,),
         c_sem) = _split_refs(refs, 7, 3, 1, comm)
        i = pl.program_id(0)
        s = pl.program_id(1)
        if comm is not None:
            @pl.when(jnp.logical_and(i == 0, s == 0))
            def _():
                comm.start(c_in, c_out, c_sem)

        xb = xb_ref[...]
        g = _dot(xb, wg_ref[0])
        u = _dot(xb, wu_ref[0])
        a = (g * _sigmoid(g) * u).astype(BF16)
        part = _dot(a, wd_ref[0])

        @pl.when(s == 0)
        def _():
            acc_ref[...] = part

        @pl.when(s > 0)
        def _():
            acc_ref[...] += part

        @pl.when(s == s_n - 1)
        def _():
            z = ALPHA * x_ref[...] + 0.5 * acc_ref[...]
            z_ref[...] = z
            y = _ln_fwd(z, g_ref[...], b_ref[...])
            y_ref[...] = y
            yb_ref[...] = y.astype(BF16)

        if comm is not None:
            @pl.when(jnp.logical_and(i == n_i - 1, s == s_n - 1))
            def _():
                comm.wait(c_in, c_out, c_sem)

    tok = lambda i, s: (i, 0)
    one = lambda i, s: (0, 0)
    return _call_with_comm(
        body, "ffn_fwd", (n_i, s_n),
        [pl.BlockSpec((tm, d), tok), pl.BlockSpec((tm, d), tok),
         pl.BlockSpec((1, d, fs), lambda i, s: (s, 0, 0)),
         pl.BlockSpec((1, d, fs), lambda i, s: (s, 0, 0)),
         pl.BlockSpec((1, fs, d), lambda i, s: (s, 0, 0)),
         pl.BlockSpec((1, d), one), pl.BlockSpec((1, d), one)],
        [pl.BlockSpec((tm, d), tok), pl.BlockSpec((tm, d), tok), pl.BlockSpec((tm, d), tok)],
        [jax.ShapeDtypeStruct((t, d), F32), jax.ShapeDtypeStruct((t, d), BF16), jax.ShapeDtypeStruct((t, d), F32)],
        [pltpu.VMEM((tm, d), F32)],
        [x, xb, wg, wu, wd, ln_g, ln_b], comm)


def ffn_bwd(xb, dzb, wg, wu, wd, comm=None):
    t, d = xb.shape
    s_n, _, fs = wg.shape
    tm = _tile(t, TM_BWD)
    n_i = t // tm

    def body(*refs):
        ((x_ref, dzb_ref, wg_ref, wu_ref, wd_ref), c_in, (dx_ref, dwg_ref, dwu_ref, dwd_ref), c_out,
         (accg, accu, accd), c_sem) = _split_refs(refs, 5, 4, 3, comm)
        s = pl.program_id(0)
        i = pl.program_id(1)
        if comm is not None:
            @pl.when(jnp.logical_and(i == 0, s == 0))
            def _():
                comm.start(c_in, c_out, c_sem)

        @pl.when(i == 0)
        def _():
            accg[...] = jnp.zeros_like(accg)
            accu[...] = jnp.zeros_like(accu)
            accd[...] = jnp.zeros_like(accd)

        x_v = x_ref[...]
        g = _dot(x_v, wg_ref[0])
        u = _dot(x_v, wu_ref[0])
        sg = _sigmoid(g)
        si = g * sg
        a = (si * u).astype(BF16)
        dfb = dzb_ref[...] * 0.5
        da = _dot_nt(dfb, wd_ref[0])
        dgate = (da * u * (sg * (1.0 + g * (1.0 - sg)))).astype(BF16)
        dup = (da * si).astype(BF16)
        dx_ref[0] = (_dot_nt(dgate, wg_ref[0]) + _dot_nt(dup, wu_ref[0])).astype(BF16)
        accg[...] += _dot_tn(x_v, dgate)
        accu[...] += _dot_tn(x_v, dup)
        accd[...] += _dot_tn(a, dfb)

        @pl.when(i == n_i - 1)
        def _():
            dwg_ref[0] = accg[...].astype(BF16)
            dwu_ref[0] = accu[...].astype(BF16)
            dwd_ref[0] = accd[...].astype(BF16)

        if comm is not None:
            @pl.when(jnp.logical_and(i == n_i - 1, s == s_n - 1))
            def _():
                comm.wait(c_in, c_out, c_sem)

    tok = lambda s, i: (i, 0)
    shard = lambda s, i: (s, 0, 0)
    return _call_with_comm(
        body, "ffn_bwd", (s_n, n_i),
        [pl.BlockSpec((tm, d), tok), pl.BlockSpec((tm, d), tok),
         pl.BlockSpec((1, d, fs), shard), pl.BlockSpec((1, d, fs), shard), pl.BlockSpec((1, fs, d), shard)],
        [pl.BlockSpec((1, tm, d), lambda s, i: (s, i, 0)),
         pl.BlockSpec((1, d, fs), shard), pl.BlockSpec((1, d, fs), shard), pl.BlockSpec((1, fs, d), shard)],
        [jax.ShapeDtypeStruct((s_n, t, d), BF16),
         jax.ShapeDtypeStruct((s_n, d, fs), BF16), jax.ShapeDtypeStruct((s_n, d, fs), BF16),
         jax.ShapeDtypeStruct((s_n, fs, d), BF16)],
        [pltpu.VMEM((d, fs), F32), pltpu.VMEM((d, fs), F32), pltpu.VMEM((fs, d), F32)],
        [xb, dzb, wg, wu, wd], comm)


def _sum_parts(base_ref, parts_ref):
    v = base_ref[...]
    if parts_ref is not None:
        for p in range(parts_ref.shape[0]):
            v = v + parts_ref[p].astype(F32)
    return v


def ln_bwd(dy, parts, z, ln_g, mm=None):
    t, d = dy.shape
    tm = _tile(t, TM_FWD)

    def body(*refs):
        refs = list(refs)
        dy_ref = refs.pop(0)
        parts_ref = refs.pop(0) if parts is not None else None
        a_ref, w_ref = (refs.pop(0), refs.pop(0)) if mm is not None else (None, None)
        z_ref, g_ref, dzb_ref, dxr_ref, dg_ref, db_ref = refs
        i = pl.program_id(0)
        dy_v = _sum_parts(dy_ref, parts_ref)
        if mm is not None:
            dy_v = dy_v + _dot(a_ref[...], w_ref[...])
        dz, xhat = _ln_bwd(dy_v, z_ref[...], g_ref[...])
        dzb_ref[...] = dz.astype(BF16)
        dxr_ref[...] = ALPHA * dz

        @pl.when(i == 0)
        def _():
            dg_ref[...] = jnp.zeros_like(dg_ref)
            db_ref[...] = jnp.zeros_like(db_ref)

        dg_ref[...] += jnp.sum(dy_v * xhat, axis=0, keepdims=True)
        db_ref[...] += jnp.sum(dy_v, axis=0, keepdims=True)

    tok = lambda i: (i, 0)
    one = lambda i: (0, 0)
    in_specs = [pl.BlockSpec((tm, d), tok)]
    args = [dy]
    name = "ln_bwd"
    if parts is not None:
        in_specs.append(pl.BlockSpec((parts.shape[0], tm, d), lambda i: (0, i, 0)))
        args.append(parts)
        name += "_parts"
    if mm is not None:
        in_specs += [pl.BlockSpec((tm, mm[0].shape[1]), tok), pl.BlockSpec(mm[1].shape, one)]
        args += list(mm)
        name += "_mm"
    return pl.pallas_call(
        body, name=name,
        grid=(t // tm,),
        in_specs=in_specs + [pl.BlockSpec((tm, d), tok), pl.BlockSpec((1, d), one)],
        out_specs=[pl.BlockSpec((tm, d), tok), pl.BlockSpec((tm, d), tok),
                   pl.BlockSpec((1, d), one), pl.BlockSpec((1, d), one)],
        out_shape=[jax.ShapeDtypeStruct((t, d), BF16), jax.ShapeDtypeStruct((t, d), F32),
                   jax.ShapeDtypeStruct((1, d), F32), jax.ShapeDtypeStruct((1, d), F32)],
        compiler_params=_cparams(),
    )(*args, z, ln_g)


def add_parts(base, parts):
    t, d = base.shape
    tm = _tile(t, TM_FWD)

    def body(b_ref, p_ref, o_ref):
        o_ref[...] = _sum_parts(b_ref, p_ref)

    tok = lambda i: (i, 0)
    return pl.pallas_call(
        body, name="add_parts",
        grid=(t // tm,),
        in_specs=[pl.BlockSpec((tm, d), tok), pl.BlockSpec((parts.shape[0], tm, d), lambda i: (0, i, 0))],
        out_specs=pl.BlockSpec((tm, d), tok),
        out_shape=jax.ShapeDtypeStruct((t, d), F32),
        compiler_params=_cparams(),
    )(base, parts)


def loss_and_grad(y, target):
    t, d = y.shape
    tm = _tile(t, TM_FWD)

    def body(y_ref, t_ref, dy_ref, l_ref):
        i = pl.program_id(0)
        e = y_ref[...] - t_ref[...]
        dy_ref[...] = e * (1.0 / d)

        @pl.when(i == 0)
        def _():
            l_ref[...] = jnp.zeros_like(l_ref)

        l_ref[...] += (0.5 / d) * jnp.sum(e * e)

    tok = lambda i: (i, 0)
    return pl.pallas_call(
        body, name="loss",
        grid=(t // tm,),
        in_specs=[pl.BlockSpec((tm, d), tok), pl.BlockSpec((tm, d), tok)],
        out_specs=[pl.BlockSpec((tm, d), tok), pl.BlockSpec((8, 128), lambda i: (0, 0))],
        out_shape=[jax.ShapeDtypeStruct((t, d), F32), jax.ShapeDtypeStruct((8, 128), F32)],
        compiler_params=_cparams(),
    )(y, target)


def matmul(a, w, add=None, out_dtype=F32, name="matmul"):
    t, k = a.shape
    n = w.shape[1]
    tm = _tile(t, TM_FWD)

    def body(*refs):
        if add is None:
            a_ref, w_ref, o_ref = refs
        else:
            a_ref, w_ref, add_ref, o_ref = refs
        o = _dot(a_ref[...].astype(BF16), w_ref[...])
        if add is not None:
            o = o + add_ref[...]
        o_ref[...] = o.astype(out_dtype)

    tok = lambda i: (i, 0)
    in_specs = [pl.BlockSpec((tm, k), tok), pl.BlockSpec((k, n), lambda i: (0, 0))]
    args = [a, w]
    if add is not None:
        in_specs.append(pl.BlockSpec((tm, n), tok))
        args.append(add)
    return pl.pallas_call(
        body, name=name,
        grid=(t // tm,),
        in_specs=in_specs,
        out_specs=pl.BlockSpec((tm, n), tok),
        out_shape=jax.ShapeDtypeStruct((t, n), out_dtype),
        compiler_params=_cparams(),
    )(*args)


def matmul_tn(a, b, name="matmul_tn"):
    t, k = a.shape
    n = b.shape[1]
    tm = _tile(t, TM_FWD)

    def body(a_ref, b_ref, o_ref):
        i = pl.program_id(0)

        @pl.when(i == 0)
        def _():
            o_ref[...] = jnp.zeros_like(o_ref)

        o_ref[...] += _dot_tn(a_ref[...].astype(BF16), b_ref[...].astype(BF16))

    tok = lambda i: (i, 0)
    return pl.pallas_call(
        body, name=name,
        grid=(t // tm,),
        in_specs=[pl.BlockSpec((tm, k), tok), pl.BlockSpec((tm, n), tok)],
        out_specs=pl.BlockSpec((k, n), lambda i: (0, 0)),
        out_shape=jax.ShapeDtypeStruct((k, n), F32),
        compiler_params=_cparams(),
    )(a, b)


def out_proj_ln(ycat, w_out, h, ln_g, ln_b):
    t, d = h.shape
    k = ycat.shape[1]
    tm = _tile(t, TM_FWD)

    def body(yc_ref, w_ref, h_ref, g_ref, b_ref, y_ref, yb_ref, z_ref):
        z = ALPHA * h_ref[...] + _dot(yc_ref[...], w_ref[...])
        z_ref[...] = z
        y = _ln_fwd(z, g_ref[...], b_ref[...])
        y_ref[...] = y
        yb_ref[...] = y.astype(BF16)

    tok = lambda i: (i, 0)
    one = lambda i: (0, 0)
    return pl.pallas_call(
        body, name="out_proj_ln",
        grid=(t // tm,),
        in_specs=[pl.BlockSpec((tm, k), tok), pl.BlockSpec((k, d), one), pl.BlockSpec((tm, d), tok),
                  pl.BlockSpec((1, d), one), pl.BlockSpec((1, d), one)],
        out_specs=[pl.BlockSpec((tm, d), tok), pl.BlockSpec((tm, d), tok), pl.BlockSpec((tm, d), tok)],
        out_shape=[jax.ShapeDtypeStruct((t, d), F32), jax.ShapeDtypeStruct((t, d), BF16),
                   jax.ShapeDtypeStruct((t, d), F32)],
        compiler_params=_cparams(),
    )(ycat, w_out, h, ln_g, ln_b)


def _halo_specs(tm, cols, n_rows):
    r = tm // HALO
    last = n_rows // HALO - 1
    return [pl.BlockSpec((HALO, cols), lambda i: (jnp.maximum(i * r - 1, 0), 0)),
            pl.BlockSpec((tm, cols), lambda i: (i, 0)),
            pl.BlockSpec((HALO, cols), lambda i: (jnp.minimum((i + 1) * r, last), 0))]


def _fill_ext(dst, prev_ref, main_ref, next_ref, i, n_i):
    tm = main_ref.shape[0]
    dst[0:HALO, :] = jnp.where(i > 0, prev_ref[...], 0.0)
    dst[HALO:HALO + tm, :] = main_ref[...]
    dst[HALO + tm:HALO + tm + HALO, :] = jnp.where(i < n_i - 1, next_ref[...], 0.0)


def _pool_lane_half():
    lane = lax.broadcasted_iota(jnp.int32, (1, POOL_W), 1)
    return jnp.left_shift(1, lane // 64)


def _pool_inv_count(t0, rows, seq):
    half = _pool_lane_half()
    tpos = t0 + lax.broadcasted_iota(jnp.int32, (rows, 1), 0)
    lo = jnp.maximum(tpos - half, 0)
    hi = jnp.minimum(tpos + half, seq)
    cnt = jnp.maximum(hi - lo, 1)
    return 1.0 / cnt.astype(F32)


def _taps(ext_ref, cols, offsets, tm, tmp_ref):
    e = ext_ref.shape[0]
    width = len(range(*cols.indices(ext_ref.shape[1])))
    by_phase = {}
    for o in offsets:
        by_phase.setdefault((HALO + o) % 8, []).append(o)
    for r, group in by_phase.items():
        if r == 0:
            for o in group:
                yield o, ext_ref[HALO + o:HALO + o + tm, cols]
            continue
        tmp_ref[:, 0:width] = ext_ref[r:r + e - 8, cols]
        for o in group:
            start = HALO + o - r
            yield o, tmp_ref[start:start + tm, 0:width]


def _pool_forward(p_ext, tm, t0, seq, tmp_ref):
    half = _pool_lane_half()
    total = jnp.zeros((tm, POOL_W), F32)
    for o, win in _taps(p_ext, slice(C_POOL[0], C_POOL[1]), range(-8, 8), tm, tmp_ref):
        m = ((o >= -half) & (o < half)).astype(F32)
        total = total + m * win
    u = p_ext[HALO:HALO + tm, C_POOL[0]:C_POOL[1]]
    return total * _pool_inv_count(t0, tm, seq) - u


def mix_fwd(proj, pw_bd, pool_scale, sconv_w, cconv_w, cconv_b, cnorm_g, cnorm_b):
    t, pc = proj.shape
    tm = _tile(t, TM_MIX)
    n_i = t // tm
    e = tm + 2 * HALO

    def body(pp_ref, pm_ref, pn_ref, pw_ref, ps_ref, sw_ref, cw_ref, cb_ref, cg_ref, cbb_ref,
             yc_ref, a1_ref, p_ext, q_ext, a0_ext, tmp):
        i = pl.program_id(0)
        _fill_ext(p_ext, pp_ref, pm_ref, pn_ref, i, n_i)
        pooled = _pool_forward(p_ext, tm, i * tm, t, tmp)
        y_a = _dot(pooled.astype(BF16), pw_ref[...]) * ps_ref[...]
        yc_ref[:, 0:256] = y_a.astype(BF16)
        q_ext[...] = p_ext[:, C_GC[0]:C_GC[1]] * p_ext[:, C_V[0]:C_V[1]]
        conv = jnp.zeros((tm, CONV_W), F32)
        for k in range(SCONV_K):
            conv = conv + sw_ref[k:k + 1, :] * q_ext[HALO + k - 1:HALO + k - 1 + tm, :]
        y_b = p_ext[HALO:HALO + tm, C_GB[0]:C_GB[1]] * conv
        yc_ref[:, 256:640] = y_b.astype(BF16)
        a0_ext[...] = p_ext[:, C_CV[0]:C_CV[1]] * _sigmoid(p_ext[:, C_CG[0]:C_CG[1]])
        a1 = jnp.zeros((tm, CONV_W), F32) + cb_ref[...]
        for o, win in _taps(a0_ext, slice(None), range(-15, 16), tm, tmp):
            a1 = a1 + cw_ref[o + 15:o + 16, :] * win
        a1_ref[...] = a1
        ln = _ln_fwd(a1, cg_ref[...], cbb_ref[...])
        yc_ref[:, 640:1024] = (ln * _sigmoid(ln)).astype(BF16)

    one = lambda i: (0, 0)
    return pl.pallas_call(
        body, name="mix_fwd",
        grid=(n_i,),
        in_specs=_halo_specs(tm, pc, t) + [
            pl.BlockSpec((POOL_W, POOL_W), one), pl.BlockSpec((1, POOL_W), one),
            pl.BlockSpec((8, CONV_W), one), pl.BlockSpec((32, CONV_W), one),
            pl.BlockSpec((1, CONV_W), one), pl.BlockSpec((1, CONV_W), one), pl.BlockSpec((1, CONV_W), one)],
        out_specs=[pl.BlockSpec((tm, 1024), lambda i: (i, 0)), pl.BlockSpec((tm, CONV_W), lambda i: (i, 0))],
        out_shape=[jax.ShapeDtypeStruct((t, 1024), BF16), jax.ShapeDtypeStruct((t, CONV_W), F32)],
        scratch_shapes=[pltpu.VMEM((e, pc), F32), pltpu.VMEM((e, CONV_W), F32), pltpu.VMEM((e, CONV_W), F32),
                        pltpu.VMEM((e - 8, CONV_W), F32)],
        compiler_params=_cparams(),
    )(proj, proj, proj, pw_bd, pool_scale, sconv_w, cconv_w, cconv_b, cnorm_g, cnorm_b)


def mix_bwd(proj, dzb, w_out_t, a1, pw_bd, pw_bd_t, pool_scale, sconv_w, cconv_w, cnorm_g, cnorm_b):
    t, pc = proj.shape
    tm = _tile(t, TM_MIX)
    n_i = t // tm
    e = tm + 2 * HALO

    def body(pp_ref, pm_ref, pn_ref, dp_ref, dm_ref, dn_ref, ap_ref, am_ref, an_ref,
             wot_ref, pw_ref, pwt_ref, ps_ref, sw_ref, cw_ref, cg_ref, cbb_ref,
             dproj_ref, dpw_ref, dps_ref, dsw_ref, dcw_ref, dcb_ref, dcg_ref, dcbb_ref,
             p_ext, dz_ext, dy_ext, a1_ext, a0_ext, da1_ext, q_ext, dc_ext, dpn_ext, tmp):
        i = pl.program_id(0)
        main = slice(HALO, HALO + tm)

        @pl.when(i == 0)
        def _():
            for r in (dpw_ref, dps_ref, dsw_ref, dcw_ref, dcb_ref, dcg_ref, dcbb_ref):
                r[...] = jnp.zeros_like(r)

        _fill_ext(p_ext, pp_ref, pm_ref, pn_ref, i, n_i)
        _fill_ext(dz_ext, dp_ref, dm_ref, dn_ref, i, n_i)
        dy_ext[...] = _dot(dz_ext[...], wot_ref[...])
        _fill_ext(a1_ext, ap_ref, am_ref, an_ref, i, n_i)

        sig_cg = _sigmoid(p_ext[:, C_CG[0]:C_CG[1]])
        a0_ext[...] = p_ext[:, C_CV[0]:C_CV[1]] * sig_cg
        a1_v = a1_ext[...]
        mu = jnp.mean(a1_v, axis=-1, keepdims=True)
        zc = a1_v - mu
        var = jnp.mean(zc * zc, axis=-1, keepdims=True)
        rstd = lax.rsqrt(var + LN_EPS)
        xhat = zc * rstd
        ln = xhat * cg_ref[...] + cbb_ref[...]
        sl = _sigmoid(ln)
        dln = dy_ext[:, 640:1024] * (sl * (1.0 + ln * (1.0 - sl)))
        dcg_ref[...] += jnp.sum((dln * xhat)[main], axis=0, keepdims=True)
        dcbb_ref[...] += jnp.sum(dln[main], axis=0, keepdims=True)
        dxh = dln * cg_ref[...]
        m1 = jnp.mean(dxh, axis=-1, keepdims=True)
        m2 = jnp.mean(dxh * xhat, axis=-1, keepdims=True)
        da1 = rstd * (dxh - m1 - xhat * m2)
        da1_ext[...] = da1
        da1_m = da1[main]
        dcb_ref[...] += jnp.sum(da1_m, axis=0, keepdims=True)
        da0 = jnp.zeros((tm, CONV_W), F32)
        for o, win in _taps(da1_ext, slice(None), range(-15, 16), tm, tmp):
            da0 = da0 + cw_ref[15 - o:16 - o, :] * win
        for o, win in _taps(a0_ext, slice(None), range(-15, 16), tm, tmp):
            dcw_ref[o + 15:o + 16, :] += jnp.sum(da1_m * win, axis=0, keepdims=True)
        sig_m = sig_cg[main]
        cv_m = p_ext[main, C_CV[0]:C_CV[1]]
        dproj_ref[:, C_CV[0]:C_CV[1]] = (da0 * sig_m).astype(BF16)
        dproj_ref[:, C_CG[0]:C_CG[1]] = (da0 * cv_m * sig_m * (1.0 - sig_m)).astype(BF16)

        q_ext[...] = p_ext[:, C_GC[0]:C_GC[1]] * p_ext[:, C_V[0]:C_V[1]]
        dc_ext[...] = dy_ext[:, 256:640] * p_ext[:, C_GB[0]:C_GB[1]]
        dc_m = dc_ext[main, :]
        conv = jnp.zeros((tm, CONV_W), F32)
        dq = jnp.zeros((tm, CONV_W), F32)
        for k in range(SCONV_K):
            q_k = q_ext[HALO + k - 1:HALO + k - 1 + tm, :]
            conv = conv + sw_ref[k:k + 1, :] * q_k
            dq = dq + sw_ref[k:k + 1, :] * dc_ext[HALO - k + 1:HALO - k + 1 + tm, :]
            dsw_ref[k:k + 1, :] += jnp.sum(dc_m * q_k, axis=0, keepdims=True)
        dproj_ref[:, C_GB[0]:C_GB[1]] = (dy_ext[main, 256:640] * conv).astype(BF16)
        dproj_ref[:, C_GC[0]:C_GC[1]] = (dq * p_ext[main, C_V[0]:C_V[1]]).astype(BF16)
        dproj_ref[:, C_V[0]:C_V[1]] = (dq * p_ext[main, C_GC[0]:C_GC[1]]).astype(BF16)

        t0 = i * tm
        dya = dy_ext[:, 0:256] * ps_ref[...]
        dpooled = _dot(dya.astype(BF16), pwt_ref[...])
        dpn_ext[...] = dpooled * _pool_inv_count(t0 - HALO, e, t)
        half = _pool_lane_half()
        du = jnp.zeros((tm, POOL_W), F32)
        for o, win in _taps(dpn_ext, slice(None), range(-7, 9), tm, tmp):
            m = ((o > -half) & (o <= half)).astype(F32)
            du = du + m * win
        dproj_ref[:, C_POOL[0]:C_POOL[1]] = (du - dpooled[main]).astype(BF16)
        pooled = _pool_forward(p_ext, tm, t0, t, tmp)
        pooled_b = pooled.astype(BF16)
        ya_pre = _dot(pooled_b, pw_ref[...])
        dps_ref[...] += jnp.sum(dy_ext[main, 0:256] * ya_pre, axis=0, keepdims=True)
        dpw_ref[...] += _dot_tn(pooled_b, dya[main].astype(BF16))

    one = lambda i: (0, 0)
    small = [((POOL_W, POOL_W), F32), ((1, POOL_W), F32), ((8, CONV_W), F32), ((32, CONV_W), F32),
             ((1, CONV_W), F32), ((1, CONV_W), F32), ((1, CONV_W), F32)]
    return pl.pallas_call(
        body, name="mix_bwd",
        grid=(n_i,),
        in_specs=_halo_specs(tm, pc, t) + _halo_specs(tm, dzb.shape[1], t) + _halo_specs(tm, CONV_W, t) + [
            pl.BlockSpec(w_out_t.shape, one),
            pl.BlockSpec((POOL_W, POOL_W), one), pl.BlockSpec((POOL_W, POOL_W), one), pl.BlockSpec((1, POOL_W), one),
            pl.BlockSpec((8, CONV_W), one), pl.BlockSpec((32, CONV_W), one),
            pl.BlockSpec((1, CONV_W), one), pl.BlockSpec((1, CONV_W), one)],
        out_specs=[pl.BlockSpec((tm, pc), lambda i: (i, 0))] + [pl.BlockSpec(s, one) for s, _ in small],
        out_shape=[jax.ShapeDtypeStruct((t, pc), BF16)] + [jax.ShapeDtypeStruct(s, dt) for s, dt in small],
        scratch_shapes=[pltpu.VMEM((e, pc), F32), pltpu.VMEM((e, dzb.shape[1]), BF16), pltpu.VMEM((e, 1024), F32),
                        pltpu.VMEM((e, CONV_W), F32),
                        pltpu.VMEM((e, CONV_W), F32), pltpu.VMEM((e, CONV_W), F32), pltpu.VMEM((e, CONV_W), F32),
                        pltpu.VMEM((e, CONV_W), F32), pltpu.VMEM((e, POOL_W), F32),
                        pltpu.VMEM((e - 8, CONV_W), F32)],
        compiler_params=_cparams(),
    )(proj, proj, proj, dzb, dzb, dzb, a1, a1, a1,
      w_out_t, pw_bd, pw_bd_t, pool_scale, sconv_w, cconv_w, cnorm_g, cnorm_b)


def _mesh_pos():
    return lax.axis_index("x"), lax.axis_index("y"), lax.axis_index("c")


def _flip(v, f):
    return 1 - v if f else v


class _Comm:
    def __init__(self, kind, arrs, lands=None, layer=0):
        self.kind = kind
        self.arrs = list(arrs)
        self.n = len(self.arrs)
        self.lands = None if lands is None else list(lands)
        self.layer = layer
        if kind == "gather":
            self.flips = [(1, 0, 0), (0, 1, 0), (1, 1, 0)]
            self.out_shape = [jax.ShapeDtypeStruct((N_CHIPS,) + a.shape, a.dtype) for a in self.arrs]
        else:
            self.flips = [(fx, fy, fc) for fx in (0, 1) for fy in (0, 1) for fc in (0, 1)][1:]
            self.out_shape = [jax.ShapeDtypeStruct(b.shape, b.dtype) for b in self.lands]
        self.n_peer = len(self.flips)

    def operands(self):
        return self.arrs + (self.lands or [])

    def scratch(self):
        return [pltpu.SemaphoreType.DMA((self.n * self.n_peer,)), pltpu.SemaphoreType.DMA((self.n * self.n_peer,)),
                pltpu.SemaphoreType.DMA((self.n,))]

    def aliases(self, in_off, out_off):
        if self.lands is None:
            return {}
        return {in_off + self.n + j: out_off + j for j in range(self.n)}

    def _copies(self, ins, outs, sems):
        send_sems, recv_sems, loc_sems = sems
        x, y, c = _mesh_pos()
        local, sends, recvs = [], [], []
        for a in range(self.n):
            def src(px, py):
                return ins[a] if self.kind == "gather" else ins[a].at[2 * px + py]

            def dst(px, py, pc):
                if self.kind == "gather":
                    return outs[a].at[2 * px + py]
                return outs[a].at[4 * px + 2 * py + pc, self.layer]

            local.append(pltpu.make_async_copy(src(x, y), dst(x, y, c), loc_sems.at[a]))
            for k, (fx, fy, fc) in enumerate(self.flips):
                px, py, pc = _flip(x, fx), _flip(y, fy), _flip(c, fc)
                sem = a * self.n_peer + k
                sends.append(pltpu.make_async_remote_copy(
                    src_ref=src(px, py), dst_ref=dst(x, y, c),
                    send_sem=send_sems.at[sem], recv_sem=recv_sems.at[sem],
                    device_id=(px, py, pc), device_id_type=MESH))
                recvs.append(pltpu.make_async_remote_copy(
                    src_ref=src(px, py), dst_ref=dst(px, py, pc),
                    send_sem=send_sems.at[sem], recv_sem=recv_sems.at[sem],
                    device_id=(px, py, pc), device_id_type=MESH))
        return local, sends, recvs

    def start(self, ins, outs, sems):
        local, sends, _ = self._copies(ins, outs, sems)
        for cp in local + sends:
            cp.start()

    def wait(self, ins, outs, sems):
        local, _, recvs = self._copies(ins, outs, sems)
        for cp in local:
            cp.wait()
        for cp in recvs:
            cp.wait()


def _split_refs(refs, n_in, n_out, n_scr, comm):
    c_in = len(comm.operands()) if comm is not None else 0
    c_out = comm.n if comm is not None else 0
    cuts = [n_in, c_in, n_out, c_out, n_scr]
    out, pos = [], 0
    for m in cuts:
        out.append(refs[pos:pos + m])
        pos += m
    out.append(refs[pos:])
    return out


def _call_with_comm(body, name, grid, in_specs, out_specs, out_shape, scratch, args, comm):
    hbm = pl.BlockSpec(memory_space=pl.ANY)
    aliases = {}
    if comm is not None:
        aliases = comm.aliases(len(in_specs), len(out_specs))
        in_specs = in_specs + [hbm] * len(comm.operands())
        out_specs = out_specs + [hbm] * comm.n
        out_shape = out_shape + comm.out_shape
        scratch = scratch + comm.scratch()
        args = args + comm.operands()
        name = name + "_" + comm.kind
    res = pl.pallas_call(
        body, name=name, grid=grid, in_specs=in_specs, out_specs=out_specs, out_shape=out_shape,
        scratch_shapes=scratch, input_output_aliases=aliases, compiler_params=_cparams(),
    )(*args)
    if comm is None:
        return res, None
    return res[:len(res) - comm.n], res[len(res) - comm.n:]


def run_comm(comm):
    def body(*refs):
        _, c_in, _, c_out, _, c_sem = _split_refs(refs, 0, 0, 0, comm)
        comm.start(c_in, c_out, c_sem)
        comm.wait(c_in, c_out, c_sem)

    hbm = pl.BlockSpec(memory_space=pl.ANY)
    return pl.pallas_call(
        body, name="comm_" + comm.kind,
        in_specs=[hbm] * len(comm.operands()), out_specs=[hbm] * comm.n, out_shape=comm.out_shape,
        scratch_shapes=comm.scratch(), input_output_aliases=comm.aliases(0, 0),
    )(*comm.operands())


def exchange(arrs, per_chip):
    n = len(arrs)
    flips = [(fx, fy, fc) for fx in (0, 1) for fy in (0, 1) for fc in (0, 1)][1:]

    def body(*refs):
        ins, outs = refs[:n], refs[n:2 * n]
        send_sems, recv_sems, loc_sems = refs[2 * n:]
        x, y, c = _mesh_pos()
        me = 4 * x + 2 * y + c
        local = []
        remote = []
        for a in range(n):
            def part(px, py):
                return ins[a].at[2 * px + py] if per_chip else ins[a]

            cp = pltpu.make_async_copy(part(x, y), outs[a].at[me], loc_sems.at[a])
            cp.start()
            local.append(cp)
            for k, (fx, fy, fc) in enumerate(flips):
                px, py, pc = _flip(x, fx), _flip(y, fy), _flip(c, fc)
                sem = a * 7 + k
                rc = pltpu.make_async_remote_copy(
                    src_ref=part(px, py), dst_ref=outs[a].at[me],
                    send_sem=send_sems.at[sem], recv_sem=recv_sems.at[sem],
                    device_id=(px, py, pc), device_id_type=MESH)
                rc.start()
                remote.append(pltpu.make_async_remote_copy(
                    src_ref=part(px, py), dst_ref=outs[a].at[4 * px + 2 * py + pc],
                    send_sem=send_sems.at[sem], recv_sem=recv_sems.at[sem],
                    device_id=(px, py, pc), device_id_type=MESH))
        for cp in local:
            cp.wait()
        for rc in remote:
            rc.wait()

    hbm = pl.BlockSpec(memory_space=pl.ANY)
    shapes = [a.shape[1:] if per_chip else a.shape for a in arrs]
    return pl.pallas_call(
        body, name="exchange_per_chip" if per_chip else "exchange_all",
        in_specs=[hbm] * n, out_specs=[hbm] * n,
        out_shape=[jax.ShapeDtypeStruct((N_DEV,) + s, a.dtype) for s, a in zip(shapes, arrs)],
        scratch_shapes=[pltpu.SemaphoreType.DMA((7 * n,)), pltpu.SemaphoreType.DMA((7 * n,)),
                        pltpu.SemaphoreType.DMA((n,))],
    )(*arrs)


def adamw(parts, w, m, v, name):
    k_n, r, c = parts.shape
    tr = r
    for cand in (512, 256, 128, 64, 32, 16, 8):
        if r % cand == 0:
            tr = cand
            break

    def body(p_ref, w_ref, m_ref, v_ref, g_ref, d_ref, mo_ref, vo_ref):
        g = p_ref[0].astype(F32)
        for k in range(1, k_n):
            g = g + p_ref[k].astype(F32)
        m_new = ADAM_B1 * m_ref[...] + (1.0 - ADAM_B1) * g
        v_new = ADAM_B2 * v_ref[...] + (1.0 - ADAM_B2) * (g * g)
        m_hat = m_new / (1.0 - ADAM_B1 ** ADAM_STEP)
        v_hat = v_new / (1.0 - ADAM_B2 ** ADAM_STEP)
        g_ref[...] = g
        d_ref[...] = -ADAM_LR * (m_hat / (jnp.sqrt(v_hat) + ADAM_EPS) + ADAM_WD * w_ref[...])
        mo_ref[...] = m_new
        vo_ref[...] = v_new

    blk = pl.BlockSpec((tr, c), lambda i: (i, 0))
    return pl.pallas_call(
        body, name=name,
        grid=(r // tr,),
        in_specs=[pl.BlockSpec((k_n, tr, c), lambda i: (0, i, 0)), blk, blk, blk],
        out_specs=[blk, blk, blk, blk],
        out_shape=[jax.ShapeDtypeStruct((r, c), F32)] * 4,
        compiler_params=_cparams(),
    )(parts, w, m, v)


def _block_diag(pool_w):
    out = jnp.zeros((POOL_W, POOL_W), pool_w.dtype)
    for g in range(4):
        out = lax.dynamic_update_slice(out, pool_w[g], (64 * g, 64 * g))
    return out


def _pad_rows(a, rows):
    return jnp.pad(a, ((0, rows - a.shape[0]), (0, 0)))


def kernel(x, ln1_g, ln1_b, ffn1_w_gate, ffn1_w_up, ffn1_w_down, mix_w_in, pool_w, pool_scale, sconv_w, cconv_w, cconv_b, cnorm_g, cnorm_b, mix_w_out, ln2_g, ln2_b, ffn2_w_gate, ffn2_w_up, ffn2_w_down, ln3_g, ln3_b, loss_target, m_ln1_g, m_ln1_b, m_ffn1_w_gate, m_ffn1_w_up, m_ffn1_w_down, m_mix_w_in, m_pool_w, m_pool_scale, m_sconv_w, m_cconv_w, m_cconv_b, m_cnorm_g, m_cnorm_b, m_mix_w_out, m_ln2_g, m_ln2_b, m_ffn2_w_gate, m_ffn2_w_up, m_ffn2_w_down, m_ln3_g, m_ln3_b, v_ln1_g, v_ln1_b, v_ffn1_w_gate, v_ffn1_w_up, v_ffn1_w_down, v_mix_w_in, v_pool_w, v_pool_scale, v_sconv_w, v_cconv_w, v_cconv_b, v_cnorm_g, v_cnorm_b, v_mix_w_out, v_ln2_g, v_ln2_b, v_ffn2_w_gate, v_ffn2_w_up, v_ffn2_w_down, v_ln3_g, v_ln3_b):
    names = ['ln1_g', 'ln1_b', 'ffn1_w_gate', 'ffn1_w_up', 'ffn1_w_down', 'mix_w_in', 'pool_w', 'pool_scale',
             'sconv_w', 'cconv_w', 'cconv_b', 'cnorm_g', 'cnorm_b', 'mix_w_out', 'ln2_g', 'ln2_b',
             'ffn2_w_gate', 'ffn2_w_up', 'ffn2_w_down', 'ln3_g', 'ln3_b']
    w = dict(zip(names, (ln1_g, ln1_b, ffn1_w_gate, ffn1_w_up, ffn1_w_down, mix_w_in, pool_w, pool_scale, sconv_w,
                         cconv_w, cconv_b, cnorm_g, cnorm_b, mix_w_out, ln2_g, ln2_b, ffn2_w_gate, ffn2_w_up,
                         ffn2_w_down, ln3_g, ln3_b)))
    mom_m = dict(zip(names, (m_ln1_g, m_ln1_b, m_ffn1_w_gate, m_ffn1_w_up, m_ffn1_w_down, m_mix_w_in, m_pool_w,
                             m_pool_scale, m_sconv_w, m_cconv_w, m_cconv_b, m_cnorm_g, m_cnorm_b, m_mix_w_out,
                             m_ln2_g, m_ln2_b, m_ffn2_w_gate, m_ffn2_w_up, m_ffn2_w_down, m_ln3_g, m_ln3_b)))
    mom_v = dict(zip(names, (v_ln1_g, v_ln1_b, v_ffn1_w_gate, v_ffn1_w_up, v_ffn1_w_down, v_mix_w_in, v_pool_w,
                             v_pool_scale, v_sconv_w, v_cconv_w, v_cconv_b, v_cnorm_g, v_cnorm_b, v_mix_w_out,
                             v_ln2_g, v_ln2_b, v_ffn2_w_gate, v_ffn2_w_up, v_ffn2_w_down, v_ln3_g, v_ln3_b)))
    big = ['ffn1_w_gate', 'ffn1_w_up', 'ffn1_w_down', 'mix_w_in', 'mix_w_out',
           'ffn2_w_gate', 'ffn2_w_up', 'ffn2_w_down']
    n_l = ln1_g.shape[0]
    d = x.shape[-1]
    fs = ffn1_w_gate.shape[-1]
    ws_in = mix_w_in.shape[-1]
    cs = sconv_w.shape[-1]
    chip = 2 * lax.axis_index("x") + lax.axis_index("y")

    keys_a = ['ffn2_w_gate', 'ffn2_w_up', 'ffn2_w_down', 'mix_w_in', 'mix_w_out']
    keys_b = ['ffn1_w_gate', 'ffn1_w_up', 'ffn1_w_down']

    def shards_b(l):
        return [w[k][l].astype(BF16) for k in keys_b]

    def shards_a(l):
        conv_loc = jnp.concatenate([sconv_w[l], cconv_w[l]], axis=0)
        return [w[k][l].astype(BF16) for k in keys_a] + [conv_loc]

    def cols(a):
        return jnp.transpose(a, (1, 0, 2)).reshape(a.shape[1], -1)

    def layer_weights(l, got_b, got_a):
        gw = dict(zip(keys_b + keys_a, list(got_b) + list(got_a[:-1])))
        conv_all = jnp.transpose(got_a[-1], (1, 0, 2)).reshape(SCONV_K + CCONV_K, N_CHIPS * cs)
        return dict(
            g1=gw['ffn1_w_gate'], u1=gw['ffn1_w_up'], d1=gw['ffn1_w_down'],
            g2=gw['ffn2_w_gate'], u2=gw['ffn2_w_up'], d2=gw['ffn2_w_down'],
            w_in=cols(gw['mix_w_in']),
            w_out=gw['mix_w_out'].reshape(-1, d),
            pw=_block_diag(pool_w[l]).astype(BF16),
            ps=pool_scale[l][None], sw=_pad_rows(conv_all[:SCONV_K], 8), cw=_pad_rows(conv_all[SCONV_K:], 32),
            cb=cconv_b[l][None], cg=cnorm_g[l][None], cbb=cnorm_b[l][None],
        )

    h = x[0]
    hb = h.astype(BF16)
    target = loss_target[0]
    saved = []
    layer_w = []
    got_b = run_comm(_Comm("gather", shards_b(0)))
    for l in range(n_l):
        (y1, y1b, z1), got_a = ffn_fwd(h, hb, got_b[0], got_b[1], got_b[2], ln1_g[l][None], ln1_b[l][None],
                                       _Comm("gather", shards_a(l)))
        lw = layer_weights(l, got_b, got_a)
        layer_w.append(lw)
        proj = matmul(y1b, lw['w_in'], name="proj")
        ycat, a1 = mix_fwd(proj, lw['pw'], lw['ps'], lw['sw'], lw['cw'], lw['cb'], lw['cg'], lw['cbb'])
        y2, y2b, z2 = out_proj_ln(ycat, lw['w_out'], y1, ln2_g[l][None], ln2_b[l][None])
        comm = _Comm("gather", shards_b(l + 1)) if l + 1 < n_l else None
        (y3, y3b, z3), got_b = ffn_fwd(y2, y2b, lw['g2'], lw['u2'], lw['d2'], ln3_g[l][None], ln3_b[l][None], comm)
        saved.append(dict(x0b=hb, z1=z1, y1b=y1b, proj=proj, ycat=ycat, a1=a1, z2=z2, y2b=y2b, z3=z3))
        h, hb = y3, y3b

    dy, loss_blk = loss_and_grad(h, target)
    loss = lax.psum(loss_blk[0, 0], ("x", "y", "c"))

    g_loc = {k: [None] * n_l for k in names if k not in big}
    lands = {k: lax.empty((N_DEV, n_l) + w[k].shape[1:], BF16) for k in big}

    def scatter(keys, arrs, layer):
        return _Comm("scatter", arrs, [lands[k] for k in keys], layer)

    pend_b = None
    parts = None
    for l in reversed(range(n_l)):
        lw, sv = layer_w[l], saved[l]
        dzb, dxr, g_loc['ln3_g'][l], g_loc['ln3_b'][l] = ln_bwd(dy, parts, sv['z3'], ln3_g[l][None])
        comm = None if pend_b is None else scatter(keys_b, pend_b, l + 1)
        (parts, dg2, du2, dd2), landed = ffn_bwd(sv['y2b'], dzb, lw['g2'], lw['u2'], lw['d2'], comm)
        if comm is not None:
            lands.update(zip(keys_b, landed))
        dzb, dxr, g_loc['ln2_g'][l], g_loc['ln2_b'][l] = ln_bwd(dxr, parts, sv['z2'], ln2_g[l][None])
        dw_out = matmul_tn(sv['ycat'], dzb, name="dw_out")
        (dproj, dpw, g_loc['pool_scale'][l], dsw, dcw, g_loc['cconv_b'][l], g_loc['cnorm_g'][l],
         g_loc['cnorm_b'][l]) = mix_bwd(sv['proj'], dzb, lw['w_out'].T, sv['a1'], lw['pw'], lw['pw'].T, lw['ps'],
                                        lw['sw'], lw['cw'], lw['cg'], lw['cbb'])
        g_loc['pool_w'][l] = jnp.stack([dpw[64 * g:64 * g + 64, 64 * g:64 * g + 64] for g in range(4)])
        g_loc['sconv_w'][l] = dsw[:SCONV_K]
        g_loc['cconv_w'][l] = dcw[:CCONV_K]
        dw_in = matmul_tn(sv['y1b'], dproj, name="dw_in")
        dw_in_c = jnp.transpose(dw_in.reshape(d, N_CHIPS, ws_in), (1, 0, 2)).astype(BF16)
        dw_out_c = dw_out.reshape(N_CHIPS, -1, d).astype(BF16)
        dzb, dxr, g_loc['ln1_g'][l], g_loc['ln1_b'][l] = ln_bwd(dxr, None, sv['z1'], ln1_g[l][None],
                                                                 mm=(dproj, lw['w_in'].T))
        comm = scatter(keys_a, [dg2, du2, dd2, dw_in_c, dw_out_c], l)
        (parts, dg1, du1, dd1), landed = ffn_bwd(sv['x0b'], dzb, lw['g1'], lw['u1'], lw['d1'], comm)
        lands.update(zip(keys_a, landed))
        pend_b = [dg1, du1, dd1]
        dy = dxr
    grad_x = add_parts(dy, parts)[None]
    lands.update(zip(keys_b, run_comm(scatter(keys_b, pend_b, 0))))
    parts_big = lands

    small = [k for k in names if k not in big]
    small_full = {}
    for k in small:
        a = jnp.stack(g_loc[k])
        small_full[k] = a.reshape(n_l, -1) if a.shape[1] == 1 else a
    flat = jnp.concatenate([small_full[k].reshape(-1) for k in small])
    n_flat = flat.shape[0]
    rows = -(-n_flat // (SMALL_ROWS * 128)) * SMALL_ROWS
    flat = jnp.pad(flat, (0, rows * 128 - n_flat)).reshape(rows, 128)
    parts_small = exchange([flat], per_chip=False)[0]

    out_g, out_d, out_m, out_v = {}, {}, {}, {}
    for k in big:
        shp = w[k].shape
        r = shp[0] * shp[1]
        res = adamw(parts_big[k].reshape(N_DEV, r, shp[2]), w[k].reshape(r, shp[2]),
                    mom_m[k].reshape(r, shp[2]), mom_v[k].reshape(r, shp[2]), name="adamw_" + k)
        out_g[k], out_d[k], out_m[k], out_v[k] = [o.reshape(shp) for o in res]

    zeros = jnp.zeros((rows, 128), F32)
    g_sum = adamw(parts_small, zeros, zeros, zeros, name="sum_small")[0].reshape(-1)
    off = 0
    for k in small:
        full = small_full[k]
        g = g_sum[off:off + full.size].reshape(full.shape)
        off += full.size
        if k in ('sconv_w', 'cconv_w'):
            g = lax.dynamic_slice_in_dim(g, chip * cs, cs, axis=2)
        out_g[k] = g.reshape(w[k].shape)

    def pack(dct):
        f = jnp.concatenate([dct[k].reshape(-1) for k in small])
        r2 = -(-f.shape[0] // (SMALL_ROWS * 128)) * SMALL_ROWS
        return jnp.pad(f, (0, r2 * 128 - f.shape[0])).reshape(r2, 128), f.shape[0]

    gp, n_small = pack(out_g)
    wp, _ = pack(w)
    mp, _ = pack(mom_m)
    vp, _ = pack(mom_v)
    _, dp, mo, vo = adamw(gp[None], wp, mp, vp, name="adamw_small")
    off = 0
    for k in small:
        sz = w[k].size
        out_d[k] = dp.reshape(-1)[off:off + sz].reshape(w[k].shape)
        out_m[k] = mo.reshape(-1)[off:off + sz].reshape(w[k].shape)
        out_v[k] = vo.reshape(-1)[off:off + sz].reshape(w[k].shape)
        off += sz

    return (loss, grad_x, *[out_g[k] for k in names], *[out_d[k] for k in names],
            *[out_m[k] for k in names], *[out_v[k] for k in names])
```

```python
import jax
import jax.numpy as jnp
from jax import lax
from jax.experimental import pallas as pl
from jax.experimental.pallas import tpu as pltpu

F32 = jnp.float32
BF16 = jnp.bfloat16

DEPTH = 4
ALPHA = (2.0 * DEPTH) ** 0.25
LN_EPS = 1e-5
POOL_W = 256
CONV_W = 384
SCONV_K = 3
CCONV_K = 31
C_POOL = (0, 256)
C_GB = (256, 640)
C_GC = (640, 1024)
C_V = (1024, 1408)
C_CV = (1408, 1792)
C_CG = (1792, 2176)

ADAM_LR = 0.001
ADAM_B1 = 0.9
ADAM_B2 = 0.999
ADAM_EPS = 1e-08
ADAM_WD = 0.01
ADAM_STEP = 10

N_CHIPS = 4
N_DEV = 8
MESH = pl.DeviceIdType.MESH

TM_FWD = 512
TM_BWD = 512
TM_MIX = 256
HALO = 32
SMALL_ROWS = 256
VMEM_LIMIT = 56 * 1024 * 1024


def _cparams():
    return pltpu.CompilerParams(vmem_limit_bytes=VMEM_LIMIT)


def _sigmoid(v):
    return 1.0 / (1.0 + jnp.exp(-v))


def _dot(a, b):
    return jnp.dot(a, b, preferred_element_type=F32)


def _dot_nt(a, b):
    return lax.dot_general(a, b, (((1,), (1,)), ((), ())), preferred_element_type=F32)


def _dot_tn(a, b):
    return lax.dot_general(a, b, (((0,), (0,)), ((), ())), preferred_element_type=F32)


def _ln_fwd(z, g, b):
    mu = jnp.mean(z, axis=-1, keepdims=True)
    zc = z - mu
    var = jnp.mean(zc * zc, axis=-1, keepdims=True)
    return zc * lax.rsqrt(var + LN_EPS) * g + b


def _ln_bwd(dy, z, g):
    mu = jnp.mean(z, axis=-1, keepdims=True)
    zc = z - mu
    var = jnp.mean(zc * zc, axis=-1, keepdims=True)
    rstd = lax.rsqrt(var + LN_EPS)
    xhat = zc * rstd
    dxh = dy * g
    m1 = jnp.mean(dxh, axis=-1, keepdims=True)
    m2 = jnp.mean(dxh * xhat, axis=-1, keepdims=True)
    return rstd * (dxh - m1 - xhat * m2), xhat


def _tile(t, tm):
    tm = min(tm, t)
    assert t % tm == 0, (t, tm)
    return tm


def ffn_fwd(x, xb, wg, wu, wd, ln_g, ln_b, comm=None):
    t, d = x.shape
    s_n, _, fs = wg.shape
    tm = _tile(t, TM_FWD)
    n_i = t // tm

    def body(*refs):
        ((x_ref, xb_ref, wg_ref, wu_ref, wd_ref, g_ref, b_ref), c_in, (y_ref, yb_ref, z_ref), c_out, (acc_ref,),
         c_sem) = _split_refs(refs, 7, 3, 1, comm)
        i = pl.program_id(0)
        s = pl.program_id(1)
        if comm is not None:
            @pl.when(jnp.logical_and(i == 0, s == 0))
            def _():
                comm.start(c_in, c_out, c_sem)

        xb = xb_ref[...]
        g = _dot(xb, wg_ref[0])
        u = _dot(xb, wu_ref[0])
        a = (g * _sigmoid(g) * u).astype(BF16)
        part = _dot(a, wd_ref[0])

        @pl.when(s == 0)
        def _():
            acc_ref[...] = part

        @pl.when(s > 0)
        def _():
            acc_ref[...] += part

        @pl.when(s == s_n - 1)
        def _():
            z = ALPHA * x_ref[...] + 0.5 * acc_ref[...]
            z_ref[...] = z
            y = _ln_fwd(z, g_ref[...], b_ref[...])
            y_ref[...] = y
            yb_ref[...] = y.astype(BF16)

        if comm is not None:
            @pl.when(jnp.logical_and(i == n_i - 1, s == s_n - 1))
            def _():
                comm.wait(c_in, c_out, c_sem)

    tok = lambda i, s: (i, 0)
    one = lambda i, s: (0, 0)
    return _call_with_comm(
        body, "ffn_fwd", (n_i, s_n),
        [pl.BlockSpec((tm, d), tok), pl.BlockSpec((tm, d), tok),
         pl.BlockSpec((1, d, fs), lambda i, s: (s, 0, 0)),
         pl.BlockSpec((1, d, fs), lambda i, s: (s, 0, 0)),
         pl.BlockSpec((1, fs, d), lambda i, s: (s, 0, 0)),
         pl.BlockSpec((1, d), one), pl.BlockSpec((1, d), one)],
        [pl.BlockSpec((tm, d), tok), pl.BlockSpec((tm, d), tok), pl.BlockSpec((tm, d), tok)],
        [jax.ShapeDtypeStruct((t, d), F32), jax.ShapeDtypeStruct((t, d), BF16), jax.ShapeDtypeStruct((t, d), F32)],
        [pltpu.VMEM((tm, d), F32)],
        [x, xb, wg, wu, wd, ln_g, ln_b], comm)


def ffn_bwd(xb, dzb, wg, wu, wd, comm=None):
    t, d = xb.shape
    s_n, _, fs = wg.shape
    tm = _tile(t, TM_BWD)
    n_i = t // tm

    def body(*refs):
        ((x_ref, dzb_ref, wg_ref, wu_ref, wd_ref), c_in, (dx_ref, dwg_ref, dwu_ref, dwd_ref), c_out,
         (accg, accu, accd), c_sem) = _split_refs(refs, 5, 4, 3, comm)
        s = pl.program_id(0)
        i = pl.program_id(1)
        if comm is not None:
            @pl.when(jnp.logical_and(i == 0, s == 0))
            def _():
                comm.start(c_in, c_out, c_sem)

        @pl.when(i == 0)
        def _():
            accg[...] = jnp.zeros_like(accg)
            accu[...] = jnp.zeros_like(accu)
            accd[...] = jnp.zeros_like(accd)

        x_v = x_ref[...]
        g = _dot(x_v, wg_ref[0])
        u = _dot(x_v, wu_ref[0])
        sg = _sigmoid(g)
        si = g * sg
        a = (si * u).astype(BF16)
        dfb = dzb_ref[...] * 0.5
        da = _dot_nt(dfb, wd_ref[0])
        dgate = (da * u * (sg * (1.0 + g * (1.0 - sg)))).astype(BF16)
        dup = (da * si).astype(BF16)
        dx_ref[0] = (_dot_nt(dgate, wg_ref[0]) + _dot_nt(dup, wu_ref[0])).astype(BF16)
        accg[...] += _dot_tn(x_v, dgate)
        accu[...] += _dot_tn(x_v, dup)
        accd[...] += _dot_tn(a, dfb)

        @pl.when(i == n_i - 1)
        def _():
            dwg_ref[0] = accg[...].astype(BF16)
            dwu_ref[0] = accu[...].astype(BF16)
            dwd_ref[0] = accd[...].astype(BF16)

        if comm is not None:
            @pl.when(jnp.logical_and(i == n_i - 1, s == s_n - 1))
            def _():
                comm.wait(c_in, c_out, c_sem)

    tok = lambda s, i: (i, 0)
    shard = lambda s, i: (s, 0, 0)
    return _call_with_comm(
        body, "ffn_bwd", (s_n, n_i),
        [pl.BlockSpec((tm, d), tok), pl.BlockSpec((tm, d), tok),
         pl.BlockSpec((1, d, fs), shard), pl.BlockSpec((1, d, fs), shard), pl.BlockSpec((1, fs, d), shard)],
        [pl.BlockSpec((1, tm, d), lambda s, i: (s, i, 0)),
         pl.BlockSpec((1, d, fs), shard), pl.BlockSpec((1, d, fs), shard), pl.BlockSpec((1, fs, d), shard)],
        [jax.ShapeDtypeStruct((s_n, t, d), BF16),
         jax.ShapeDtypeStruct((s_n, d, fs), BF16), jax.ShapeDtypeStruct((s_n, d, fs), BF16),
         jax.ShapeDtypeStruct((s_n, fs, d), BF16)],
        [pltpu.VMEM((d, fs), F32), pltpu.VMEM((d, fs), F32), pltpu.VMEM((fs, d), F32)],
        [xb, dzb, wg, wu, wd], comm)


def _sum_parts(base_ref, parts_ref):
    v = base_ref[...]
    if parts_ref is not None:
        for p in range(parts_ref.shape[0]):
            v = v + parts_ref[p].astype(F32)
    return v


def ln_bwd(dy, parts, z, ln_g, mm=None):
    t, d = dy.shape
    tm = _tile(t, TM_FWD)

    def body(*refs):
        refs = list(refs)
        dy_ref = refs.pop(0)
        parts_ref = refs.pop(0) if parts is not None else None
        a_ref, w_ref = (refs.pop(0), refs.pop(0)) if mm is not None else (None, None)
        z_ref, g_ref, dzb_ref, dxr_ref, dg_ref, db_ref = refs
        i = pl.program_id(0)
        dy_v = _sum_parts(dy_ref, parts_ref)
        if mm is not None:
            dy_v = dy_v + _dot(a_ref[...], w_ref[...])
        dz, xhat = _ln_bwd(dy_v, z_ref[...], g_ref[...])
        dzb_ref[...] = dz.astype(BF16)
        dxr_ref[...] = ALPHA * dz

        @pl.when(i == 0)
        def _():
            dg_ref[...] = jnp.zeros_like(dg_ref)
            db_ref[...] = jnp.zeros_like(db_ref)

        dg_ref[...] += jnp.sum(dy_v * xhat, axis=0, keepdims=True)
        db_ref[...] += jnp.sum(dy_v, axis=0, keepdims=True)

    tok = lambda i: (i, 0)
    one = lambda i: (0, 0)
    in_specs = [pl.BlockSpec((tm, d), tok)]
    args = [dy]
    name = "ln_bwd"
    if parts is not None:
        in_specs.append(pl.BlockSpec((parts.shape[0], tm, d), lambda i: (0, i, 0)))
        args.append(parts)
        name += "_parts"
    if mm is not None:
        in_specs += [pl.BlockSpec((tm, mm[0].shape[1]), tok), pl.BlockSpec(mm[1].shape, one)]
        args += list(mm)
        name += "_mm"
    return pl.pallas_call(
        body, name=name,
        grid=(t // tm,),
        in_specs=in_specs + [pl.BlockSpec((tm, d), tok), pl.BlockSpec((1, d), one)],
        out_specs=[pl.BlockSpec((tm, d), tok), pl.BlockSpec((tm, d), tok),
                   pl.BlockSpec((1, d), one), pl.BlockSpec((1, d), one)],
        out_shape=[jax.ShapeDtypeStruct((t, d), BF16), jax.ShapeDtypeStruct((t, d), F32),
                   jax.ShapeDtypeStruct((1, d), F32), jax.ShapeDtypeStruct((1, d), F32)],
        compiler_params=_cparams(),
    )(*args, z, ln_g)


def add_parts(base, parts):
    t, d = base.shape
    tm = _tile(t, TM_FWD)

    def body(b_ref, p_ref, o_ref):
        o_ref[...] = _sum_parts(b_ref, p_ref)

    tok = lambda i: (i, 0)
    return pl.pallas_call(
        body, name="add_parts",
        grid=(t // tm,),
        in_specs=[pl.BlockSpec((tm, d), tok), pl.BlockSpec((parts.shape[0], tm, d), lambda i: (0, i, 0))],
        out_specs=pl.BlockSpec((tm, d), tok),
        out_shape=jax.ShapeDtypeStruct((t, d), F32),
        compiler_params=_cparams(),
    )(base, parts)


def loss_and_grad(y, target):
    t, d = y.shape
    tm = _tile(t, TM_FWD)

    def body(y_ref, t_ref, dy_ref, l_ref):
        i = pl.program_id(0)
        e = y_ref[...] - t_ref[...]
        dy_ref[...] = e * (1.0 / d)

        @pl.when(i == 0)
        def _():
            l_ref[...] = jnp.zeros_like(l_ref)

        l_ref[...] += (0.5 / d) * jnp.sum(e * e)

    tok = lambda i: (i, 0)
    return pl.pallas_call(
        body, name="loss",
        grid=(t // tm,),
        in_specs=[pl.BlockSpec((tm, d), tok), pl.BlockSpec((tm, d), tok)],
        out_specs=[pl.BlockSpec((tm, d), tok), pl.BlockSpec((8, 128), lambda i: (0, 0))],
        out_shape=[jax.ShapeDtypeStruct((t, d), F32), jax.ShapeDtypeStruct((8, 128), F32)],
        compiler_params=_cparams(),
    )(y, target)


def out_proj_ln(ycat, w_out, h, ln_g, ln_b):
    t, d = h.shape
    k = ycat.shape[1]
    tm = _tile(t, TM_FWD)

    def body(yc_ref, w_ref, h_ref, g_ref, b_ref, y_ref, yb_ref, z_ref):
        z = ALPHA * h_ref[...] + _dot(yc_ref[...], w_ref[...])
        z_ref[...] = z
        y = _ln_fwd(z, g_ref[...], b_ref[...])
        y_ref[...] = y
        yb_ref[...] = y.astype(BF16)

    tok = lambda i: (i, 0)
    one = lambda i: (0, 0)
    return pl.pallas_call(
        body, name="out_proj_ln",
        grid=(t // tm,),
        in_specs=[pl.BlockSpec((tm, k), tok), pl.BlockSpec((k, d), one), pl.BlockSpec((tm, d), tok),
                  pl.BlockSpec((1, d), one), pl.BlockSpec((1, d), one)],
        out_specs=[pl.BlockSpec((tm, d), tok), pl.BlockSpec((tm, d), tok), pl.BlockSpec((tm, d), tok)],
        out_shape=[jax.ShapeDtypeStruct((t, d), F32), jax.ShapeDtypeStruct((t, d), BF16),
                   jax.ShapeDtypeStruct((t, d), F32)],
        compiler_params=_cparams(),
    )(ycat, w_out, h, ln_g, ln_b)


def _halo_specs(tm, cols, n_rows):
    r = tm // HALO
    last = n_rows // HALO - 1
    return [pl.BlockSpec((HALO, cols), lambda i: (jnp.maximum(i * r - 1, 0), 0)),
            pl.BlockSpec((tm, cols), lambda i: (i, 0)),
            pl.BlockSpec((HALO, cols), lambda i: (jnp.minimum((i + 1) * r, last), 0))]


def _fill_ext(dst, prev_ref, main_ref, next_ref, i, n_i):
    tm = main_ref.shape[0]
    dst[0:HALO, :] = jnp.where(i > 0, prev_ref[...], 0.0)
    dst[HALO:HALO + tm, :] = main_ref[...]
    dst[HALO + tm:HALO + tm + HALO, :] = jnp.where(i < n_i - 1, next_ref[...], 0.0)


def _pool_lane_half():
    lane = lax.broadcasted_iota(jnp.int32, (1, POOL_W), 1)
    return jnp.left_shift(1, lane // 64)


def _pool_inv_count(t0, rows, seq):
    half = _pool_lane_half()
    tpos = t0 + lax.broadcasted_iota(jnp.int32, (rows, 1), 0)
    lo = jnp.maximum(tpos - half, 0)
    hi = jnp.minimum(tpos + half, seq)
    cnt = jnp.maximum(hi - lo, 1)
    return 1.0 / cnt.astype(F32)


def _taps(ext_ref, cols, offsets, tm, tmp_ref):
    e = ext_ref.shape[0]
    width = len(range(*cols.indices(ext_ref.shape[1])))
    by_phase = {}
    for o in offsets:
        by_phase.setdefault((HALO + o) % 8, []).append(o)
    for r, group in by_phase.items():
        if r == 0:
            for o in group:
                yield o, ext_ref[HALO + o:HALO + o + tm, cols]
            continue
        tmp_ref[:, 0:width] = ext_ref[r:r + e - 8, cols]
        for o in group:
            start = HALO + o - r
            yield o, tmp_ref[start:start + tm, 0:width]


def _pool_forward(p_ext, tm, t0, seq, tmp_ref):
    half = _pool_lane_half()
    total = jnp.zeros((tm, POOL_W), F32)
    for o, win in _taps(p_ext, slice(C_POOL[0], C_POOL[1]), range(-8, 8), tm, tmp_ref):
        m = ((o >= -half) & (o < half)).astype(F32)
        total = total + m * win
    u = p_ext[HALO:HALO + tm, C_POOL[0]:C_POOL[1]]
    return total * _pool_inv_count(t0, tm, seq) - u


CONV_ROWS = 64


def _fill_phases(ph_ref, ext_ref):
    e = ext_ref.shape[0]
    for r in range(8):
        ph_ref[r] = ext_ref[r:r + e - 8, :]


def _phase_win(ph_ref, o, row0, rows):
    r = (HALO + o) % 8
    start = HALO + o - r + row0
    return ph_ref[r, start:start + rows, :]


def mix_fwd(yb, w_in, pw_bd, pool_scale, sconv_w, cconv_w, cconv_b, cnorm_g, cnorm_b):
    t, d = yb.shape
    pc = w_in.shape[1]
    tm = _tile(t, TM_MIX)
    n_i = t // tm
    e = tm + 2 * HALO

    def body(yp_ref, ym_ref, yn_ref, win_ref, pw_ref, ps_ref, sw_ref, cw_ref, cb_ref, cg_ref, cbb_ref,
             proj_ref, yc_ref, a1_ref, yb_ext, p_ext, q_ext, a0_ext, tmp, ph, a1_s):
        i = pl.program_id(0)
        _fill_ext(yb_ext, yp_ref, ym_ref, yn_ref, i, n_i)
        p_ext[...] = _dot(yb_ext[...], win_ref[...])
        proj_ref[...] = p_ext[HALO:HALO + tm, :]
        pooled = _pool_forward(p_ext, tm, i * tm, t, tmp)
        y_a = _dot(pooled.astype(BF16), pw_ref[...]) * ps_ref[...]
        yc_ref[:, 0:256] = y_a.astype(BF16)
        q_ext[...] = p_ext[:, C_GC[0]:C_GC[1]] * p_ext[:, C_V[0]:C_V[1]]
        conv = jnp.zeros((tm, CONV_W), F32)
        for k in range(SCONV_K):
            conv = conv + sw_ref[k:k + 1, :] * q_ext[HALO + k - 1:HALO + k - 1 + tm, :]
        y_b = p_ext[HALO:HALO + tm, C_GB[0]:C_GB[1]] * conv
        yc_ref[:, 256:640] = y_b.astype(BF16)
        a0_ext[...] = p_ext[:, C_CV[0]:C_CV[1]] * _sigmoid(p_ext[:, C_CG[0]:C_CG[1]])
        _fill_phases(ph, a0_ext)
        for c0 in range(0, tm, CONV_ROWS):
            acc = jnp.zeros((CONV_ROWS, CONV_W), F32) + cb_ref[...]
            for o in range(-15, 16):
                acc = acc + cw_ref[o + 15:o + 16, :] * _phase_win(ph, o, c0, CONV_ROWS)
            a1_s[c0:c0 + CONV_ROWS, :] = acc
        a1 = a1_s[...]
        a1_ref[...] = a1
        ln = _ln_fwd(a1, cg_ref[...], cbb_ref[...])
        yc_ref[:, 640:1024] = (ln * _sigmoid(ln)).astype(BF16)

    one = lambda i: (0, 0)
    tok = lambda i: (i, 0)
    return pl.pallas_call(
        body, name="mix_fwd",
        grid=(n_i,),
        in_specs=_halo_specs(tm, d, t) + [
            pl.BlockSpec((d, pc), one),
            pl.BlockSpec((POOL_W, POOL_W), one), pl.BlockSpec((1, POOL_W), one),
            pl.BlockSpec((8, CONV_W), one), pl.BlockSpec((32, CONV_W), one),
            pl.BlockSpec((1, CONV_W), one), pl.BlockSpec((1, CONV_W), one), pl.BlockSpec((1, CONV_W), one)],
        out_specs=[pl.BlockSpec((tm, pc), tok), pl.BlockSpec((tm, 1024), tok), pl.BlockSpec((tm, CONV_W), tok)],
        out_shape=[jax.ShapeDtypeStruct((t, pc), F32), jax.ShapeDtypeStruct((t, 1024), BF16),
                   jax.ShapeDtypeStruct((t, CONV_W), F32)],
        scratch_shapes=[pltpu.VMEM((e, d), BF16), pltpu.VMEM((e, pc), F32), pltpu.VMEM((e, CONV_W), F32),
                        pltpu.VMEM((e, CONV_W), F32), pltpu.VMEM((e - 8, CONV_W), F32),
                        pltpu.VMEM((8, e - 8, CONV_W), F32), pltpu.VMEM((tm, CONV_W), F32)],
        compiler_params=_cparams(),
    )(yb, yb, yb, w_in, pw_bd, pool_scale, sconv_w, cconv_w, cconv_b, cnorm_g, cnorm_b)


def mix_bwd(proj, dzb, w_out_t, a1, yb, ycat, pw_bd, pw_bd_t, pool_scale, sconv_w, cconv_w, cnorm_g, cnorm_b):
    t, pc = proj.shape
    d = yb.shape[1]
    tm = _tile(t, TM_MIX)
    n_i = t // tm
    e = tm + 2 * HALO

    def body(pp_ref, pm_ref, pn_ref, dp_ref, dm_ref, dn_ref, ap_ref, am_ref, an_ref, yb_ref, yc_ref,
             wot_ref, pw_ref, pwt_ref, ps_ref, sw_ref, cw_ref, cg_ref, cbb_ref,
             dproj_ref, dwin_ref, dwout_ref, dpw_ref, dps_ref, dsw_ref, dcw_ref, dcb_ref, dcg_ref, dcbb_ref,
             p_ext, dz_ext, dy_ext, a1_ext, a0_ext, da1_ext, q_ext, dc_ext, dpn_ext, tmp, ph, da0_s):
        i = pl.program_id(0)
        main = slice(HALO, HALO + tm)

        @pl.when(i == 0)
        def _():
            for r in (dwin_ref, dwout_ref, dpw_ref, dps_ref, dsw_ref, dcw_ref, dcb_ref, dcg_ref, dcbb_ref):
                r[...] = jnp.zeros_like(r)

        _fill_ext(p_ext, pp_ref, pm_ref, pn_ref, i, n_i)
        _fill_ext(dz_ext, dp_ref, dm_ref, dn_ref, i, n_i)
        dy_ext[...] = _dot(dz_ext[...], wot_ref[...])
        _fill_ext(a1_ext, ap_ref, am_ref, an_ref, i, n_i)

        sig_cg = _sigmoid(p_ext[:, C_CG[0]:C_CG[1]])
        a0_ext[...] = p_ext[:, C_CV[0]:C_CV[1]] * sig_cg
        a1_v = a1_ext[...]
        mu = jnp.mean(a1_v, axis=-1, keepdims=True)
        zc = a1_v - mu
        var = jnp.mean(zc * zc, axis=-1, keepdims=True)
        rstd = lax.rsqrt(var + LN_EPS)
        xhat = zc * rstd
        ln = xhat * cg_ref[...] + cbb_ref[...]
        sl = _sigmoid(ln)
        dln = dy_ext[:, 640:1024] * (sl * (1.0 + ln * (1.0 - sl)))
        dcg_ref[...] += jnp.sum((dln * xhat)[main], axis=0, keepdims=True)
        dcbb_ref[...] += jnp.sum(dln[main], axis=0, keepdims=True)
        dxh = dln * cg_ref[...]
        m1 = jnp.mean(dxh, axis=-1, keepdims=True)
        m2 = jnp.mean(dxh * xhat, axis=-1, keepdims=True)
        da1 = rstd * (dxh - m1 - xhat * m2)
        da1_ext[...] = da1
        da1_m = da1[main]
        dcb_ref[...] += jnp.sum(da1_m, axis=0, keepdims=True)
        _fill_phases(ph, da1_ext)
        for c0 in range(0, tm, CONV_ROWS):
            acc = jnp.zeros((CONV_ROWS, CONV_W), F32)
            for o in range(-15, 16):
                acc = acc + cw_ref[15 - o:16 - o, :] * _phase_win(ph, o, c0, CONV_ROWS)
            da0_s[c0:c0 + CONV_ROWS, :] = acc
        da0 = da0_s[...]
        _fill_phases(ph, a0_ext)
        for c0 in range(0, tm, CONV_ROWS):
            da1_c = da1_ext[HALO + c0:HALO + c0 + CONV_ROWS, :]
            for o in range(-15, 16):
                dcw_ref[o + 15:o + 16, :] += jnp.sum(da1_c * _phase_win(ph, o, c0, CONV_ROWS), axis=0, keepdims=True)
        sig_m = sig_cg[main]
        cv_m = p_ext[main, C_CV[0]:C_CV[1]]
        dproj_ref[:, C_CV[0]:C_CV[1]] = (da0 * sig_m).astype(BF16)
        dproj_ref[:, C_CG[0]:C_CG[1]] = (da0 * cv_m * sig_m * (1.0 - sig_m)).astype(BF16)

        q_ext[...] = p_ext[:, C_GC[0]:C_GC[1]] * p_ext[:, C_V[0]:C_V[1]]
        dc_ext[...] = dy_ext[:, 256:640] * p_ext[:, C_GB[0]:C_GB[1]]
        dc_m = dc_ext[main, :]
        conv = jnp.zeros((tm, CONV_W), F32)
        dq = jnp.zeros((tm, CONV_W), F32)
        for k in range(SCONV_K):
            q_k = q_ext[HALO + k - 1:HALO + k - 1 + tm, :]
            conv = conv + sw_ref[k:k + 1, :] * q_k
            dq = dq + sw_ref[k:k + 1, :] * dc_ext[HALO - k + 1:HALO - k + 1 + tm, :]
            dsw_ref[k:k + 1, :] += jnp.sum(dc_m * q_k, axis=0, keepdims=True)
        dproj_ref[:, C_GB[0]:C_GB[1]] = (dy_ext[main, 256:640] * conv).astype(BF16)
        dproj_ref[:, C_GC[0]:C_GC[1]] = (dq * p_ext[main, C_V[0]:C_V[1]]).astype(BF16)
        dproj_ref[:, C_V[0]:C_V[1]] = (dq * p_ext[main, C_GC[0]:C_GC[1]]).astype(BF16)

        t0 = i * tm
        dya = dy_ext[:, 0:256] * ps_ref[...]
        dpooled = _dot(dya.astype(BF16), pwt_ref[...])
        dpn_ext[...] = dpooled * _pool_inv_count(t0 - HALO, e, t)
        half = _pool_lane_half()
        du = jnp.zeros((tm, POOL_W), F32)
        for o, win in _taps(dpn_ext, slice(None), range(-7, 9), tm, tmp):
            m = ((o > -half) & (o <= half)).astype(F32)
            du = du + m * win
        dproj_ref[:, C_POOL[0]:C_POOL[1]] = (du - dpooled[main]).astype(BF16)
        pooled = _pool_forward(p_ext, tm, t0, t, tmp)
        pooled_b = pooled.astype(BF16)
        ya_pre = _dot(pooled_b, pw_ref[...])
        dps_ref[...] += jnp.sum(dy_ext[main, 0:256] * ya_pre, axis=0, keepdims=True)
        dpw_ref[...] += _dot_tn(pooled_b, dya[main].astype(BF16))

        dwin_ref[...] += _dot_tn(yb_ref[...], dproj_ref[...])
        dwout_ref[...] += _dot_tn(yc_ref[...], dm_ref[...])

    one = lambda i: (0, 0)
    tok = lambda i: (i, 0)
    small = [((d, pc), F32), ((ycat.shape[1], d), F32),
             ((POOL_W, POOL_W), F32), ((1, POOL_W), F32), ((8, CONV_W), F32), ((32, CONV_W), F32),
             ((1, CONV_W), F32), ((1, CONV_W), F32), ((1, CONV_W), F32)]
    return pl.pallas_call(
        body, name="mix_bwd",
        grid=(n_i,),
        in_specs=_halo_specs(tm, pc, t) + _halo_specs(tm, dzb.shape[1], t) + _halo_specs(tm, CONV_W, t) + [
            pl.BlockSpec((tm, d), tok), pl.BlockSpec((tm, ycat.shape[1]), tok),
            pl.BlockSpec(w_out_t.shape, one),
            pl.BlockSpec((POOL_W, POOL_W), one), pl.BlockSpec((POOL_W, POOL_W), one), pl.BlockSpec((1, POOL_W), one),
            pl.BlockSpec((8, CONV_W), one), pl.BlockSpec((32, CONV_W), one),
            pl.BlockSpec((1, CONV_W), one), pl.BlockSpec((1, CONV_W), one)],
        out_specs=[pl.BlockSpec((tm, pc), lambda i: (i, 0))] + [pl.BlockSpec(s, one) for s, _ in small],
        out_shape=[jax.ShapeDtypeStruct((t, pc), BF16)] + [jax.ShapeDtypeStruct(s, dt) for s, dt in small],
        scratch_shapes=[pltpu.VMEM((e, pc), F32), pltpu.VMEM((e, dzb.shape[1]), BF16), pltpu.VMEM((e, 1024), F32),
                        pltpu.VMEM((e, CONV_W), F32),
                        pltpu.VMEM((e, CONV_W), F32), pltpu.VMEM((e, CONV_W), F32), pltpu.VMEM((e, CONV_W), F32),
                        pltpu.VMEM((e, CONV_W), F32), pltpu.VMEM((e, POOL_W), F32),
                        pltpu.VMEM((e - 8, CONV_W), F32),
                        pltpu.VMEM((8, e - 8, CONV_W), F32), pltpu.VMEM((tm, CONV_W), F32)],
        compiler_params=_cparams(),
    )(proj, proj, proj, dzb, dzb, dzb, a1, a1, a1, yb, ycat,
      w_out_t, pw_bd, pw_bd_t, pool_scale, sconv_w, cconv_w, cnorm_g, cnorm_b)


def _mesh_pos():
    return lax.axis_index("x"), lax.axis_index("y"), lax.axis_index("c")


def _flip(v, f):
    return 1 - v if f else v


class _Comm:
    def __init__(self, kind, arrs, lands=None, layer=0):
        self.kind = kind
        self.arrs = list(arrs)
        self.n = len(self.arrs)
        self.lands = None if lands is None else list(lands)
        self.layer = layer
        if kind == "gather":
            self.flips = [(1, 0, 0), (0, 1, 0), (1, 1, 0)]
            self.out_shape = [jax.ShapeDtypeStruct((N_CHIPS,) + a.shape, a.dtype) for a in self.arrs]
        else:
            self.flips = [(fx, fy, fc) for fx in (0, 1) for fy in (0, 1) for fc in (0, 1)][1:]
            self.out_shape = [jax.ShapeDtypeStruct(b.shape, b.dtype) for b in self.lands]
        self.n_peer = len(self.flips)

    def operands(self):
        return self.arrs + (self.lands or [])

    def scratch(self):
        return [pltpu.SemaphoreType.DMA((self.n * self.n_peer,)), pltpu.SemaphoreType.DMA((self.n * self.n_peer,)),
                pltpu.SemaphoreType.DMA((self.n,))]

    def aliases(self, in_off, out_off):
        if self.lands is None:
            return {}
        return {in_off + self.n + j: out_off + j for j in range(self.n)}

    def _copies(self, ins, outs, sems):
        send_sems, recv_sems, loc_sems = sems
        x, y, c = _mesh_pos()
        local, sends, recvs = [], [], []
        for a in range(self.n):
            def src(px, py):
                return ins[a] if self.kind == "gather" else ins[a].at[2 * px + py]

            def dst(px, py, pc):
                if self.kind == "gather":
                    return outs[a].at[2 * px + py]
                return outs[a].at[4 * px + 2 * py + pc, self.layer]

            local.append(pltpu.make_async_copy(src(x, y), dst(x, y, c), loc_sems.at[a]))
            for k, (fx, fy, fc) in enumerate(self.flips):
                px, py, pc = _flip(x, fx), _flip(y, fy), _flip(c, fc)
                sem = a * self.n_peer + k
                sends.append(pltpu.make_async_remote_copy(
                    src_ref=src(px, py), dst_ref=dst(x, y, c),
                    send_sem=send_sems.at[sem], recv_sem=recv_sems.at[sem],
                    device_id=(px, py, pc), device_id_type=MESH))
                recvs.append(pltpu.make_async_remote_copy(
                    src_ref=src(px, py), dst_ref=dst(px, py, pc),
                    send_sem=send_sems.at[sem], recv_sem=recv_sems.at[sem],
                    device_id=(px, py, pc), device_id_type=MESH))
        return local, sends, recvs

    def start(self, ins, outs, sems):
        local, sends, _ = self._copies(ins, outs, sems)
        for cp in local + sends:
            cp.start()

    def wait(self, ins, outs, sems):
        local, _, recvs = self._copies(ins, outs, sems)
        for cp in local:
            cp.wait()
        for cp in recvs:
            cp.wait()


def _split_refs(refs, n_in, n_out, n_scr, comm):
    c_in = len(comm.operands()) if comm is not None else 0
    c_out = comm.n if comm is not None else 0
    cuts = [n_in, c_in, n_out, c_out, n_scr]
    out, pos = [], 0
    for m in cuts:
        out.append(refs[pos:pos + m])
        pos += m
    out.append(refs[pos:])
    return out


def _call_with_comm(body, name, grid, in_specs, out_specs, out_shape, scratch, args, comm):
    hbm = pl.BlockSpec(memory_space=pl.ANY)
    aliases = {}
    if comm is not None:
        aliases = comm.aliases(len(in_specs), len(out_specs))
        in_specs = in_specs + [hbm] * len(comm.operands())
        out_specs = out_specs + [hbm] * comm.n
        out_shape = out_shape + comm.out_shape
        scratch = scratch + comm.scratch()
        args = args + comm.operands()
        name = name + "_" + comm.kind
    res = pl.pallas_call(
        body, name=name, grid=grid, in_specs=in_specs, out_specs=out_specs, out_shape=out_shape,
        scratch_shapes=scratch, input_output_aliases=aliases, compiler_params=_cparams(),
    )(*args)
    if comm is None:
        return res, None
    return res[:len(res) - comm.n], res[len(res) - comm.n:]


def run_comm(comm):
    def body(*refs):
        _, c_in, _, c_out, _, c_sem = _split_refs(refs, 0, 0, 0, comm)
        comm.start(c_in, c_out, c_sem)
        comm.wait(c_in, c_out, c_sem)

    hbm = pl.BlockSpec(memory_space=pl.ANY)
    return pl.pallas_call(
        body, name="comm_" + comm.kind,
        in_specs=[hbm] * len(comm.operands()), out_specs=[hbm] * comm.n, out_shape=comm.out_shape,
        scratch_shapes=comm.scratch(), input_output_aliases=comm.aliases(0, 0),
    )(*comm.operands())


def exchange(arrs, per_chip):
    n = len(arrs)
    flips = [(fx, fy, fc) for fx in (0, 1) for fy in (0, 1) for fc in (0, 1)][1:]

    def body(*refs):
        ins, outs = refs[:n], refs[n:2 * n]
        send_sems, recv_sems, loc_sems = refs[2 * n:]
        x, y, c = _mesh_pos()
        me = 4 * x + 2 * y + c
        local = []
        remote = []
        for a in range(n):
            def part(px, py):
                return ins[a].at[2 * px + py] if per_chip else ins[a]

            cp = pltpu.make_async_copy(part(x, y), outs[a].at[me], loc_sems.at[a])
            cp.start()
            local.append(cp)
            for k, (fx, fy, fc) in enumerate(flips):
                px, py, pc = _flip(x, fx), _flip(y, fy), _flip(c, fc)
                sem = a * 7 + k
                rc = pltpu.make_async_remote_copy(
                    src_ref=part(px, py), dst_ref=outs[a].at[me],
                    send_sem=send_sems.at[sem], recv_sem=recv_sems.at[sem],
                    device_id=(px, py, pc), device_id_type=MESH)
                rc.start()
                remote.append(pltpu.make_async_remote_copy(
                    src_ref=part(px, py), dst_ref=outs[a].at[4 * px + 2 * py + pc],
                    send_sem=send_sems.at[sem], recv_sem=recv_sems.at[sem],
                    device_id=(px, py, pc), device_id_type=MESH))
        for cp in local:
            cp.wait()
        for rc in remote:
            rc.wait()

    hbm = pl.BlockSpec(memory_space=pl.ANY)
    shapes = [a.shape[1:] if per_chip else a.shape for a in arrs]
    return pl.pallas_call(
        body, name="exchange_per_chip" if per_chip else "exchange_all",
        in_specs=[hbm] * n, out_specs=[hbm] * n,
        out_shape=[jax.ShapeDtypeStruct((N_DEV,) + s, a.dtype) for s, a in zip(shapes, arrs)],
        scratch_shapes=[pltpu.SemaphoreType.DMA((7 * n,)), pltpu.SemaphoreType.DMA((7 * n,)),
                        pltpu.SemaphoreType.DMA((n,))],
    )(*arrs)


def adamw(parts, w, m, v, name):
    k_n, r, c = parts.shape
    tr = r
    for cand in (512, 256, 128, 64, 32, 16, 8):
        if r % cand == 0:
            tr = cand
            break

    def body(p_ref, w_ref, m_ref, v_ref, g_ref, d_ref, mo_ref, vo_ref):
        g = p_ref[0].astype(F32)
        for k in range(1, k_n):
            g = g + p_ref[k].astype(F32)
        m_new = ADAM_B1 * m_ref[...] + (1.0 - ADAM_B1) * g
        v_new = ADAM_B2 * v_ref[...] + (1.0 - ADAM_B2) * (g * g)
        m_hat = m_new / (1.0 - ADAM_B1 ** ADAM_STEP)
        v_hat = v_new / (1.0 - ADAM_B2 ** ADAM_STEP)
        g_ref[...] = g
        d_ref[...] = -ADAM_LR * (m_hat / (jnp.sqrt(v_hat) + ADAM_EPS) + ADAM_WD * w_ref[...])
        mo_ref[...] = m_new
        vo_ref[...] = v_new

    blk = pl.BlockSpec((tr, c), lambda i: (i, 0))
    return pl.pallas_call(
        body, name=name,
        grid=(r // tr,),
        in_specs=[pl.BlockSpec((k_n, tr, c), lambda i: (0, i, 0)), blk, blk, blk],
        out_specs=[blk, blk, blk, blk],
        out_shape=[jax.ShapeDtypeStruct((r, c), F32)] * 4,
        compiler_params=_cparams(),
    )(parts, w, m, v)


def _block_diag(pool_w):
    out = jnp.zeros((POOL_W, POOL_W), pool_w.dtype)
    for g in range(4):
        out = lax.dynamic_update_slice(out, pool_w[g], (64 * g, 64 * g))
    return out


def _pad_rows(a, rows):
    return jnp.pad(a, ((0, rows - a.shape[0]), (0, 0)))


def kernel(x, ln1_g, ln1_b, ffn1_w_gate, ffn1_w_up, ffn1_w_down, mix_w_in, pool_w, pool_scale, sconv_w, cconv_w, cconv_b, cnorm_g, cnorm_b, mix_w_out, ln2_g, ln2_b, ffn2_w_gate, ffn2_w_up, ffn2_w_down, ln3_g, ln3_b, loss_target, m_ln1_g, m_ln1_b, m_ffn1_w_gate, m_ffn1_w_up, m_ffn1_w_down, m_mix_w_in, m_pool_w, m_pool_scale, m_sconv_w, m_cconv_w, m_cconv_b, m_cnorm_g, m_cnorm_b, m_mix_w_out, m_ln2_g, m_ln2_b, m_ffn2_w_gate, m_ffn2_w_up, m_ffn2_w_down, m_ln3_g, m_ln3_b, v_ln1_g, v_ln1_b, v_ffn1_w_gate, v_ffn1_w_up, v_ffn1_w_down, v_mix_w_in, v_pool_w, v_pool_scale, v_sconv_w, v_cconv_w, v_cconv_b, v_cnorm_g, v_cnorm_b, v_mix_w_out, v_ln2_g, v_ln2_b, v_ffn2_w_gate, v_ffn2_w_up, v_ffn2_w_down, v_ln3_g, v_ln3_b):
    names = ['ln1_g', 'ln1_b', 'ffn1_w_gate', 'ffn1_w_up', 'ffn1_w_down', 'mix_w_in', 'pool_w', 'pool_scale',
             'sconv_w', 'cconv_w', 'cconv_b', 'cnorm_g', 'cnorm_b', 'mix_w_out', 'ln2_g', 'ln2_b',
             'ffn2_w_gate', 'ffn2_w_up', 'ffn2_w_down', 'ln3_g', 'ln3_b']
    w = dict(zip(names, (ln1_g, ln1_b, ffn1_w_gate, ffn1_w_up, ffn1_w_down, mix_w_in, pool_w, pool_scale, sconv_w,
                         cconv_w, cconv_b, cnorm_g, cnorm_b, mix_w_out, ln2_g, ln2_b, ffn2_w_gate, ffn2_w_up,
                         ffn2_w_down, ln3_g, ln3_b)))
    mom_m = dict(zip(names, (m_ln1_g, m_ln1_b, m_ffn1_w_gate, m_ffn1_w_up, m_ffn1_w_down, m_mix_w_in, m_pool_w,
                             m_pool_scale, m_sconv_w, m_cconv_w, m_cconv_b, m_cnorm_g, m_cnorm_b, m_mix_w_out,
                             m_ln2_g, m_ln2_b, m_ffn2_w_gate, m_ffn2_w_up, m_ffn2_w_down, m_ln3_g, m_ln3_b)))
    mom_v = dict(zip(names, (v_ln1_g, v_ln1_b, v_ffn1_w_gate, v_ffn1_w_up, v_ffn1_w_down, v_mix_w_in, v_pool_w,
                             v_pool_scale, v_sconv_w, v_cconv_w, v_cconv_b, v_cnorm_g, v_cnorm_b, v_mix_w_out,
                             v_ln2_g, v_ln2_b, v_ffn2_w_gate, v_ffn2_w_up, v_ffn2_w_down, v_ln3_g, v_ln3_b)))
    big = ['ffn1_w_gate', 'ffn1_w_up', 'ffn1_w_down', 'mix_w_in', 'mix_w_out',
           'ffn2_w_gate', 'ffn2_w_up', 'ffn2_w_down']
    n_l = ln1_g.shape[0]
    d = x.shape[-1]
    fs = ffn1_w_gate.shape[-1]
    ws_in = mix_w_in.shape[-1]
    cs = sconv_w.shape[-1]
    chip = 2 * lax.axis_index("x") + lax.axis_index("y")

    keys_a = ['ffn2_w_gate', 'ffn2_w_up', 'ffn2_w_down', 'mix_w_in', 'mix_w_out']
    keys_b = ['ffn1_w_gate', 'ffn1_w_up', 'ffn1_w_down']

    def shards_b(l):
        return [w[k][l].astype(BF16) for k in keys_b]

    def shards_a(l):
        conv_loc = jnp.concatenate([sconv_w[l], cconv_w[l]], axis=0)
        return [w[k][l].astype(BF16) for k in keys_a] + [conv_loc]

    def cols(a):
        return jnp.transpose(a, (1, 0, 2)).reshape(a.shape[1], -1)

    def layer_weights(l, got_b, got_a):
        gw = dict(zip(keys_b + keys_a, list(got_b) + list(got_a[:-1])))
        conv_all = jnp.transpose(got_a[-1], (1, 0, 2)).reshape(SCONV_K + CCONV_K, N_CHIPS * cs)
        return dict(
            g1=gw['ffn1_w_gate'], u1=gw['ffn1_w_up'], d1=gw['ffn1_w_down'],
            g2=gw['ffn2_w_gate'], u2=gw['ffn2_w_up'], d2=gw['ffn2_w_down'],
            w_in=cols(gw['mix_w_in']),
            w_out=gw['mix_w_out'].reshape(-1, d),
            pw=_block_diag(pool_w[l]).astype(BF16),
            ps=pool_scale[l][None], sw=_pad_rows(conv_all[:SCONV_K], 8), cw=_pad_rows(conv_all[SCONV_K:], 32),
            cb=cconv_b[l][None], cg=cnorm_g[l][None], cbb=cnorm_b[l][None],
        )

    h = x[0]
    hb = h.astype(BF16)
    target = loss_target[0]
    saved = []
    layer_w = []
    got_b = run_comm(_Comm("gather", shards_b(0)))
    for l in range(n_l):
        (y1, y1b, z1), got_a = ffn_fwd(h, hb, got_b[0], got_b[1], got_b[2], ln1_g[l][None], ln1_b[l][None],
                                       _Comm("gather", shards_a(l)))
        lw = layer_weights(l, got_b, got_a)
        layer_w.append(lw)
        proj, ycat, a1 = mix_fwd(y1b, lw['w_in'], lw['pw'], lw['ps'], lw['sw'], lw['cw'], lw['cb'], lw['cg'],
                                 lw['cbb'])
        y2, y2b, z2 = out_proj_ln(ycat, lw['w_out'], y1, ln2_g[l][None], ln2_b[l][None])
        comm = _Comm("gather", shards_b(l + 1)) if l + 1 < n_l else None
        (y3, y3b, z3), got_b = ffn_fwd(y2, y2b, lw['g2'], lw['u2'], lw['d2'], ln3_g[l][None], ln3_b[l][None], comm)
        saved.append(dict(x0b=hb, z1=z1, y1b=y1b, proj=proj, ycat=ycat, a1=a1, z2=z2, y2b=y2b, z3=z3))
        h, hb = y3, y3b

    dy, loss_blk = loss_and_grad(h, target)
    loss = lax.psum(loss_blk[0, 0], ("x", "y", "c"))

    g_loc = {k: [None] * n_l for k in names if k not in big}
    lands = {k: lax.empty((N_DEV, n_l) + w[k].shape[1:], BF16) for k in big}

    def scatter(keys, arrs, layer):
        return _Comm("scatter", arrs, [lands[k] for k in keys], layer)

    pend_b = None
    parts = None
    for l in reversed(range(n_l)):
        lw, sv = layer_w[l], saved[l]
        dzb, dxr, g_loc['ln3_g'][l], g_loc['ln3_b'][l] = ln_bwd(dy, parts, sv['z3'], ln3_g[l][None])
        comm = None if pend_b is None else scatter(keys_b, pend_b, l + 1)
        (parts, dg2, du2, dd2), landed = ffn_bwd(sv['y2b'], dzb, lw['g2'], lw['u2'], lw['d2'], comm)
        if comm is not None:
            lands.update(zip(keys_b, landed))
        dzb, dxr, g_loc['ln2_g'][l], g_loc['ln2_b'][l] = ln_bwd(dxr, parts, sv['z2'], ln2_g[l][None])
        (dproj, dw_in, dw_out, dpw, g_loc['pool_scale'][l], dsw, dcw, g_loc['cconv_b'][l], g_loc['cnorm_g'][l],
         g_loc['cnorm_b'][l]) = mix_bwd(sv['proj'], dzb, lw['w_out'].T, sv['a1'], sv['y1b'], sv['ycat'],
                                        lw['pw'], lw['pw'].T, lw['ps'], lw['sw'], lw['cw'], lw['cg'], lw['cbb'])
        g_loc['pool_w'][l] = jnp.stack([dpw[64 * g:64 * g + 64, 64 * g:64 * g + 64] for g in range(4)])
        g_loc['sconv_w'][l] = dsw[:SCONV_K]
        g_loc['cconv_w'][l] = dcw[:CCONV_K]
        dw_in_c = jnp.transpose(dw_in.reshape(d, N_CHIPS, ws_in), (1, 0, 2)).astype(BF16)
        dw_out_c = dw_out.reshape(N_CHIPS, -1, d).astype(BF16)
        dzb, dxr, g_loc['ln1_g'][l], g_loc['ln1_b'][l] = ln_bwd(dxr, None, sv['z1'], ln1_g[l][None],
                                                                 mm=(dproj, lw['w_in'].T))
        comm = scatter(keys_a, [dg2, du2, dd2, dw_in_c, dw_out_c], l)
        (parts, dg1, du1, dd1), landed = ffn_bwd(sv['x0b'], dzb, lw['g1'], lw['u1'], lw['d1'], comm)
        lands.update(zip(keys_a, landed))
        pend_b = [dg1, du1, dd1]
        dy = dxr
    grad_x = add_parts(dy, parts)[None]
    lands.update(zip(keys_b, run_comm(scatter(keys_b, pend_b, 0))))
    parts_big = lands

    small = [k for k in names if k not in big]
    small_full = {}
    for k in small:
        a = jnp.stack(g_loc[k])
        small_full[k] = a.reshape(n_l, -1) if a.shape[1] == 1 else a
    flat = jnp.concatenate([small_full[k].reshape(-1) for k in small])
    n_flat = flat.shape[0]
    rows = -(-n_flat // (SMALL_ROWS * 128)) * SMALL_ROWS
    flat = jnp.pad(flat, (0, rows * 128 - n_flat)).reshape(rows, 128)
    parts_small = exchange([flat], per_chip=False)[0]

    out_g, out_d, out_m, out_v = {}, {}, {}, {}
    for k in big:
        shp = w[k].shape
        r = shp[0] * shp[1]
        res = adamw(parts_big[k].reshape(N_DEV, r, shp[2]), w[k].reshape(r, shp[2]),
                    mom_m[k].reshape(r, shp[2]), mom_v[k].reshape(r, shp[2]), name="adamw_" + k)
        out_g[k], out_d[k], out_m[k], out_v[k] = [o.reshape(shp) for o in res]

    zeros = jnp.zeros((rows, 128), F32)
    g_sum = adamw(parts_small, zeros, zeros, zeros, name="sum_small")[0].reshape(-1)
    off = 0
    for k in small:
        full = small_full[k]
        g = g_sum[off:off + full.size].reshape(full.shape)
        off += full.size
        if k in ('sconv_w', 'cconv_w'):
            g = lax.dynamic_slice_in_dim(g, chip * cs, cs, axis=2)
        out_g[k] = g.reshape(w[k].shape)

    def pack(dct):
        f = jnp.concatenate([dct[k].reshape(-1) for k in small])
        r2 = -(-f.shape[0] // (SMALL_ROWS * 128)) * SMALL_ROWS
        return jnp.pad(f, (0, r2 * 128 - f.shape[0])).reshape(r2, 128), f.shape[0]

    gp, n_small = pack(out_g)
    wp, _ = pack(w)
    mp, _ = pack(mom_m)
    vp, _ = pack(mom_v)
    _, dp, mo, vo = adamw(gp[None], wp, mp, vp, name="adamw_small")
    off = 0
    for k in small:
        sz = w[k].size
        out_d[k] = dp.reshape(-1)[off:off + sz].reshape(w[k].shape)
        out_m[k] = mo.reshape(-1)[off:off + sz].reshape(w[k].shape)
        out_v[k] = vo.reshape(-1)[off:off + sz].reshape(w[k].shape)
        off += sz

    return (loss, grad_x, *[out_g[k] for k in names], *[out_d[k] for k in names],
            *[out_m[k] for k in names], *[out_v[k] for k in names])
```

```python
import jax
import jax.numpy as jnp
from jax import lax
from jax.experimental import pallas as pl
from jax.experimental.pallas import tpu as pltpu

F32 = jnp.float32
BF16 = jnp.bfloat16

DEPTH = 4
ALPHA = (2.0 * DEPTH) ** 0.25
LN_EPS = 1e-5
POOL_W = 256
CONV_W = 384
SCONV_K = 3
CCONV_K = 31
C_POOL = (0, 256)
C_GB = (256, 640)
C_GC = (640, 1024)
C_V = (1024, 1408)
C_CV = (1408, 1792)
C_CG = (1792, 2176)

ADAM_LR = 0.001
ADAM_B1 = 0.9
ADAM_B2 = 0.999
ADAM_EPS = 1e-08
ADAM_WD = 0.01
ADAM_STEP = 10

N_CHIPS = 4
N_DEV = 8
MESH = pl.DeviceIdType.MESH

TM_FWD = 512
TM_BWD = 512
TM_MIX = 256
HALO = 32
SMALL_ROWS = 256
VMEM_LIMIT = 56 * 1024 * 1024


def _cparams():
    return pltpu.CompilerParams(vmem_limit_bytes=VMEM_LIMIT)


def _sigmoid(v):
    return 1.0 / (1.0 + jnp.exp(-v))


def _dot(a, b):
    return jnp.dot(a, b, preferred_element_type=F32)


def _dot_nt(a, b):
    return lax.dot_general(a, b, (((1,), (1,)), ((), ())), preferred_element_type=F32)


def _dot_tn(a, b):
    return lax.dot_general(a, b, (((0,), (0,)), ((), ())), preferred_element_type=F32)


def _ln_fwd(z, g, b):
    mu = jnp.mean(z, axis=-1, keepdims=True)
    zc = z - mu
    var = jnp.mean(zc * zc, axis=-1, keepdims=True)
    return zc * lax.rsqrt(var + LN_EPS) * g + b


def _ln_bwd(dy, z, g):
    mu = jnp.mean(z, axis=-1, keepdims=True)
    zc = z - mu
    var = jnp.mean(zc * zc, axis=-1, keepdims=True)
    rstd = lax.rsqrt(var + LN_EPS)
    xhat = zc * rstd
    dxh = dy * g
    m1 = jnp.mean(dxh, axis=-1, keepdims=True)
    m2 = jnp.mean(dxh * xhat, axis=-1, keepdims=True)
    return rstd * (dxh - m1 - xhat * m2), xhat


def _tile(t, tm):
    tm = min(tm, t)
    assert t % tm == 0, (t, tm)
    return tm


def ffn_fwd(x, xb, wg, wu, wd, ln_g, ln_b, comm=None):
    t, d = x.shape
    s_n, _, fs = wg.shape
    tm = _tile(t, TM_FWD)
    n_i = t // tm

    def body(*refs):
        ((x_ref, xb_ref, wg_ref, wu_ref, wd_ref, g_ref, b_ref), c_in, (y_ref, yb_ref, z_ref, gs_ref, us_ref), c_out,
         (acc_ref,), c_sem) = _split_refs(refs, 7, 5, 1, comm)
        i = pl.program_id(0)
        s = pl.program_id(1)
        if comm is not None:
            @pl.when(jnp.logical_and(i == 0, s == 0))
            def _():
                comm.start(c_in, c_out, c_sem)

        xb = xb_ref[...]
        g = _dot(xb, wg_ref[0])
        u = _dot(xb, wu_ref[0])
        gs_ref[0] = g.astype(BF16)
        us_ref[0] = u.astype(BF16)
        a = (g * _sigmoid(g) * u).astype(BF16)
        part = _dot(a, wd_ref[0])

        @pl.when(s == 0)
        def _():
            acc_ref[...] = part

        @pl.when(s > 0)
        def _():
            acc_ref[...] += part

        @pl.when(s == s_n - 1)
        def _():
            z = ALPHA * x_ref[...] + 0.5 * acc_ref[...]
            z_ref[...] = z
            y = _ln_fwd(z, g_ref[...], b_ref[...])
            y_ref[...] = y
            yb_ref[...] = y.astype(BF16)

        if comm is not None:
            @pl.when(jnp.logical_and(i == n_i - 1, s == s_n - 1))
            def _():
                comm.wait(c_in, c_out, c_sem)

    tok = lambda i, s: (i, 0)
    one = lambda i, s: (0, 0)
    return _call_with_comm(
        body, "ffn_fwd", (n_i, s_n),
        [pl.BlockSpec((tm, d), tok), pl.BlockSpec((tm, d), tok),
         pl.BlockSpec((1, d, fs), lambda i, s: (s, 0, 0)),
         pl.BlockSpec((1, d, fs), lambda i, s: (s, 0, 0)),
         pl.BlockSpec((1, fs, d), lambda i, s: (s, 0, 0)),
         pl.BlockSpec((1, d), one), pl.BlockSpec((1, d), one)],
        [pl.BlockSpec((tm, d), tok), pl.BlockSpec((tm, d), tok), pl.BlockSpec((tm, d), tok),
         pl.BlockSpec((1, tm, fs), lambda i, s: (s, i, 0)), pl.BlockSpec((1, tm, fs), lambda i, s: (s, i, 0))],
        [jax.ShapeDtypeStruct((t, d), F32), jax.ShapeDtypeStruct((t, d), BF16), jax.ShapeDtypeStruct((t, d), F32),
         jax.ShapeDtypeStruct((s_n, t, fs), BF16), jax.ShapeDtypeStruct((s_n, t, fs), BF16)],
        [pltpu.VMEM((tm, d), F32)],
        [x, xb, wg, wu, wd, ln_g, ln_b], comm)


def ffn_bwd(xb, dzb, gs, us, wg, wu, wd, comm=None):
    t, d = xb.shape
    s_n, _, fs = wg.shape
    tm = _tile(t, TM_BWD)
    n_i = t // tm

    def body(*refs):
        ((x_ref, dzb_ref, gs_ref, us_ref, wg_ref, wu_ref, wd_ref), c_in, (dx_ref, dwg_ref, dwu_ref, dwd_ref), c_out,
         (accg, accu, accd), c_sem) = _split_refs(refs, 7, 4, 3, comm)
        s = pl.program_id(0)
        i = pl.program_id(1)
        if comm is not None:
            @pl.when(jnp.logical_and(i == 0, s == 0))
            def _():
                comm.start(c_in, c_out, c_sem)

        @pl.when(i == 0)
        def _():
            accg[...] = jnp.zeros_like(accg)
            accu[...] = jnp.zeros_like(accu)
            accd[...] = jnp.zeros_like(accd)

        x_v = x_ref[...]
        g = gs_ref[0].astype(F32)
        u = us_ref[0].astype(F32)
        sg = _sigmoid(g)
        si = g * sg
        a = (si * u).astype(BF16)
        dfb = dzb_ref[...] * 0.5
        da = _dot_nt(dfb, wd_ref[0])
        dgate = (da * u * (sg * (1.0 + g * (1.0 - sg)))).astype(BF16)
        dup = (da * si).astype(BF16)
        dx_ref[0] = (_dot_nt(dgate, wg_ref[0]) + _dot_nt(dup, wu_ref[0])).astype(BF16)
        accg[...] += _dot_tn(x_v, dgate)
        accu[...] += _dot_tn(x_v, dup)
        accd[...] += _dot_tn(a, dfb)

        @pl.when(i == n_i - 1)
        def _():
            dwg_ref[0] = accg[...].astype(BF16)
            dwu_ref[0] = accu[...].astype(BF16)
            dwd_ref[0] = accd[...].astype(BF16)

        if comm is not None:
            @pl.when(jnp.logical_and(i == n_i - 1, s == s_n - 1))
            def _():
                comm.wait(c_in, c_out, c_sem)

    tok = lambda s, i: (i, 0)
    shard = lambda s, i: (s, 0, 0)
    return _call_with_comm(
        body, "ffn_bwd", (s_n, n_i),
        [pl.BlockSpec((tm, d), tok), pl.BlockSpec((tm, d), tok),
         pl.BlockSpec((1, tm, fs), lambda s, i: (s, i, 0)), pl.BlockSpec((1, tm, fs), lambda s, i: (s, i, 0)),
         pl.BlockSpec((1, d, fs), shard), pl.BlockSpec((1, d, fs), shard), pl.BlockSpec((1, fs, d), shard)],
        [pl.BlockSpec((1, tm, d), lambda s, i: (s, i, 0)),
         pl.BlockSpec((1, d, fs), shard), pl.BlockSpec((1, d, fs), shard), pl.BlockSpec((1, fs, d), shard)],
        [jax.ShapeDtypeStruct((s_n, t, d), BF16),
         jax.ShapeDtypeStruct((s_n, d, fs), BF16), jax.ShapeDtypeStruct((s_n, d, fs), BF16),
         jax.ShapeDtypeStruct((s_n, fs, d), BF16)],
        [pltpu.VMEM((d, fs), F32), pltpu.VMEM((d, fs), F32), pltpu.VMEM((fs, d), F32)],
        [xb, dzb, gs, us, wg, wu, wd], comm)


def _sum_parts(base_ref, parts_ref):
    v = base_ref[...]
    if parts_ref is not None:
        for p in range(parts_ref.shape[0]):
            v = v + parts_ref[p].astype(F32)
    return v


def ln_bwd(dy, parts, z, ln_g, mm=None):
    t, d = dy.shape
    tm = _tile(t, TM_FWD)

    def body(*refs):
        refs = list(refs)
        dy_ref = refs.pop(0)
        parts_ref = refs.pop(0) if parts is not None else None
        a_ref, w_ref = (refs.pop(0), refs.pop(0)) if mm is not None else (None, None)
        z_ref, g_ref, dzb_ref, dxr_ref, dg_ref, db_ref = refs
        i = pl.program_id(0)
        dy_v = _sum_parts(dy_ref, parts_ref)
        if mm is not None:
            dy_v = dy_v + _dot(a_ref[...], w_ref[...])
        dz, xhat = _ln_bwd(dy_v, z_ref[...], g_ref[...])
        dzb_ref[...] = dz.astype(BF16)
        dxr_ref[...] = ALPHA * dz

        @pl.when(i == 0)
        def _():
            dg_ref[...] = jnp.zeros_like(dg_ref)
            db_ref[...] = jnp.zeros_like(db_ref)

        dg_ref[...] += jnp.sum(dy_v * xhat, axis=0, keepdims=True)
        db_ref[...] += jnp.sum(dy_v, axis=0, keepdims=True)

    tok = lambda i: (i, 0)
    one = lambda i: (0, 0)
    in_specs = [pl.BlockSpec((tm, d), tok)]
    args = [dy]
    name = "ln_bwd"
    if parts is not None:
        in_specs.append(pl.BlockSpec((parts.shape[0], tm, d), lambda i: (0, i, 0)))
        args.append(parts)
        name += "_parts"
    if mm is not None:
        in_specs += [pl.BlockSpec((tm, mm[0].shape[1]), tok), pl.BlockSpec(mm[1].shape, one)]
        args += list(mm)
        name += "_mm"
    return pl.pallas_call(
        body, name=name,
        grid=(t // tm,),
        in_specs=in_specs + [pl.BlockSpec((tm, d), tok), pl.BlockSpec((1, d), one)],
        out_specs=[pl.BlockSpec((tm, d), tok), pl.BlockSpec((tm, d), tok),
                   pl.BlockSpec((1, d), one), pl.BlockSpec((1, d), one)],
        out_shape=[jax.ShapeDtypeStruct((t, d), BF16), jax.ShapeDtypeStruct((t, d), F32),
                   jax.ShapeDtypeStruct((1, d), F32), jax.ShapeDtypeStruct((1, d), F32)],
        compiler_params=_cparams(),
    )(*args, z, ln_g)


def add_parts(base, parts):
    t, d = base.shape
    tm = _tile(t, TM_FWD)

    def body(b_ref, p_ref, o_ref):
        o_ref[...] = _sum_parts(b_ref, p_ref)

    tok = lambda i: (i, 0)
    return pl.pallas_call(
        body, name="add_parts",
        grid=(t // tm,),
        in_specs=[pl.BlockSpec((tm, d), tok), pl.BlockSpec((parts.shape[0], tm, d), lambda i: (0, i, 0))],
        out_specs=pl.BlockSpec((tm, d), tok),
        out_shape=jax.ShapeDtypeStruct((t, d), F32),
        compiler_params=_cparams(),
    )(base, parts)


def loss_and_grad(y, target):
    t, d = y.shape
    tm = _tile(t, TM_FWD)

    def body(y_ref, t_ref, dy_ref, l_ref):
        i = pl.program_id(0)
        e = y_ref[...] - t_ref[...]
        dy_ref[...] = e * (1.0 / d)

        @pl.when(i == 0)
        def _():
            l_ref[...] = jnp.zeros_like(l_ref)

        l_ref[...] += (0.5 / d) * jnp.sum(e * e)

    tok = lambda i: (i, 0)
    return pl.pallas_call(
        body, name="loss",
        grid=(t // tm,),
        in_specs=[pl.BlockSpec((tm, d), tok), pl.BlockSpec((tm, d), tok)],
        out_specs=[pl.BlockSpec((tm, d), tok), pl.BlockSpec((8, 128), lambda i: (0, 0))],
        out_shape=[jax.ShapeDtypeStruct((t, d), F32), jax.ShapeDtypeStruct((8, 128), F32)],
        compiler_params=_cparams(),
    )(y, target)


def out_proj_ln(ycat, w_out, h, ln_g, ln_b):
    t, d = h.shape
    k = ycat.shape[1]
    tm = _tile(t, TM_FWD)

    def body(yc_ref, w_ref, h_ref, g_ref, b_ref, y_ref, yb_ref, z_ref):
        z = ALPHA * h_ref[...] + _dot(yc_ref[...], w_ref[...])
        z_ref[...] = z
        y = _ln_fwd(z, g_ref[...], b_ref[...])
        y_ref[...] = y
        yb_ref[...] = y.astype(BF16)

    tok = lambda i: (i, 0)
    one = lambda i: (0, 0)
    return pl.pallas_call(
        body, name="out_proj_ln",
        grid=(t // tm,),
        in_specs=[pl.BlockSpec((tm, k), tok), pl.BlockSpec((k, d), one), pl.BlockSpec((tm, d), tok),
                  pl.BlockSpec((1, d), one), pl.BlockSpec((1, d), one)],
        out_specs=[pl.BlockSpec((tm, d), tok), pl.BlockSpec((tm, d), tok), pl.BlockSpec((tm, d), tok)],
        out_shape=[jax.ShapeDtypeStruct((t, d), F32), jax.ShapeDtypeStruct((t, d), BF16),
                   jax.ShapeDtypeStruct((t, d), F32)],
        compiler_params=_cparams(),
    )(ycat, w_out, h, ln_g, ln_b)


def _halo_specs(tm, cols, n_rows):
    r = tm // HALO
    last = n_rows // HALO - 1
    return [pl.BlockSpec((HALO, cols), lambda i: (jnp.maximum(i * r - 1, 0), 0)),
            pl.BlockSpec((tm, cols), lambda i: (i, 0)),
            pl.BlockSpec((HALO, cols), lambda i: (jnp.minimum((i + 1) * r, last), 0))]


def _fill_ext(dst, prev_ref, main_ref, next_ref, i, n_i):
    tm = main_ref.shape[0]
    dst[0:HALO, :] = jnp.where(i > 0, prev_ref[...], 0.0)
    dst[HALO:HALO + tm, :] = main_ref[...]
    dst[HALO + tm:HALO + tm + HALO, :] = jnp.where(i < n_i - 1, next_ref[...], 0.0)


def _pool_lane_half():
    lane = lax.broadcasted_iota(jnp.int32, (1, POOL_W), 1)
    return jnp.left_shift(1, lane // 64)


def _pool_inv_count(t0, rows, seq):
    half = _pool_lane_half()
    tpos = t0 + lax.broadcasted_iota(jnp.int32, (rows, 1), 0)
    lo = jnp.maximum(tpos - half, 0)
    hi = jnp.minimum(tpos + half, seq)
    cnt = jnp.maximum(hi - lo, 1)
    return 1.0 / cnt.astype(F32)


def _taps(ext_ref, cols, offsets, tm, tmp_ref):
    e = ext_ref.shape[0]
    width = len(range(*cols.indices(ext_ref.shape[1])))
    by_phase = {}
    for o in offsets:
        by_phase.setdefault((HALO + o) % 8, []).append(o)
    for r, group in by_phase.items():
        if r == 0:
            for o in group:
                yield o, ext_ref[HALO + o:HALO + o + tm, cols]
            continue
        tmp_ref[:, 0:width] = ext_ref[r:r + e - 8, cols]
        for o in group:
            start = HALO + o - r
            yield o, tmp_ref[start:start + tm, 0:width]


def _pool_forward(p_ext, tm, t0, seq, tmp_ref):
    half = _pool_lane_half()
    total = jnp.zeros((tm, POOL_W), F32)
    for o, win in _taps(p_ext, slice(C_POOL[0], C_POOL[1]), range(-8, 8), tm, tmp_ref):
        m = ((o >= -half) & (o < half)).astype(F32)
        total = total + m * win
    u = p_ext[HALO:HALO + tm, C_POOL[0]:C_POOL[1]]
    return total * _pool_inv_count(t0, tm, seq) - u


CONV_ROWS = 64


def _fill_phases(ph_ref, ext_ref):
    e = ext_ref.shape[0]
    for r in range(8):
        ph_ref[r] = ext_ref[r:r + e - 8, :]


def _phase_win(ph_ref, o, row0, rows):
    r = (HALO + o) % 8
    start = HALO + o - r + row0
    return ph_ref[r, start:start + rows, :]


def mix_fwd(yb, w_in, pw_bd, pool_scale, sconv_w, cconv_w, cconv_b, cnorm_g, cnorm_b):
    t, d = yb.shape
    pc = w_in.shape[1]
    tm = _tile(t, TM_MIX)
    n_i = t // tm
    e = tm + 2 * HALO

    def body(yp_ref, ym_ref, yn_ref, win_ref, pw_ref, ps_ref, sw_ref, cw_ref, cb_ref, cg_ref, cbb_ref,
             proj_ref, yc_ref, a1_ref, yb_ext, p_ext, q_ext, a0_ext, tmp, ph, a1_s):
        i = pl.program_id(0)
        _fill_ext(yb_ext, yp_ref, ym_ref, yn_ref, i, n_i)
        p_ext[...] = _dot(yb_ext[...], win_ref[...])
        proj_ref[...] = p_ext[HALO:HALO + tm, :]
        pooled = _pool_forward(p_ext, tm, i * tm, t, tmp)
        y_a = _dot(pooled.astype(BF16), pw_ref[...]) * ps_ref[...]
        yc_ref[:, 0:256] = y_a.astype(BF16)
        q_ext[...] = p_ext[:, C_GC[0]:C_GC[1]] * p_ext[:, C_V[0]:C_V[1]]
        conv = jnp.zeros((tm, CONV_W), F32)
        for k in range(SCONV_K):
            conv = conv + sw_ref[k:k + 1, :] * q_ext[HALO + k - 1:HALO + k - 1 + tm, :]
        y_b = p_ext[HALO:HALO + tm, C_GB[0]:C_GB[1]] * conv
        yc_ref[:, 256:640] = y_b.astype(BF16)
        a0_ext[...] = p_ext[:, C_CV[0]:C_CV[1]] * _sigmoid(p_ext[:, C_CG[0]:C_CG[1]])
        _fill_phases(ph, a0_ext)
        for c0 in range(0, tm, CONV_ROWS):
            acc = jnp.zeros((CONV_ROWS, CONV_W), F32) + cb_ref[...]
            for o in range(-15, 16):
                acc = acc + cw_ref[o + 15:o + 16, :] * _phase_win(ph, o, c0, CONV_ROWS)
            a1_s[c0:c0 + CONV_ROWS, :] = acc
        a1 = a1_s[...]
        a1_ref[...] = a1
        ln = _ln_fwd(a1, cg_ref[...], cbb_ref[...])
        yc_ref[:, 640:1024] = (ln * _sigmoid(ln)).astype(BF16)

    one = lambda i: (0, 0)
    tok = lambda i: (i, 0)
    return pl.pallas_call(
        body, name="mix_fwd",
        grid=(n_i,),
        in_specs=_halo_specs(tm, d, t) + [
            pl.BlockSpec((d, pc), one),
            pl.BlockSpec((POOL_W, POOL_W), one), pl.BlockSpec((1, POOL_W), one),
            pl.BlockSpec((8, CONV_W), one), pl.BlockSpec((32, CONV_W), one),
            pl.BlockSpec((1, CONV_W), one), pl.BlockSpec((1, CONV_W), one), pl.BlockSpec((1, CONV_W), one)],
        out_specs=[pl.BlockSpec((tm, pc), tok), pl.BlockSpec((tm, 1024), tok), pl.BlockSpec((tm, CONV_W), tok)],
        out_shape=[jax.ShapeDtypeStruct((t, pc), F32), jax.ShapeDtypeStruct((t, 1024), BF16),
                   jax.ShapeDtypeStruct((t, CONV_W), F32)],
        scratch_shapes=[pltpu.VMEM((e, d), BF16), pltpu.VMEM((e, pc), F32), pltpu.VMEM((e, CONV_W), F32),
                        pltpu.VMEM((e, CONV_W), F32), pltpu.VMEM((e - 8, CONV_W), F32),
                        pltpu.VMEM((8, e - 8, CONV_W), F32), pltpu.VMEM((tm, CONV_W), F32)],
        compiler_params=_cparams(),
    )(yb, yb, yb, w_in, pw_bd, pool_scale, sconv_w, cconv_w, cconv_b, cnorm_g, cnorm_b)


def mix_bwd(proj, dzb, w_out_t, a1, yb, ycat, pw_bd, pw_bd_t, pool_scale, sconv_w, cconv_w, cnorm_g, cnorm_b):
    t, pc = proj.shape
    d = yb.shape[1]
    tm = _tile(t, TM_MIX)
    n_i = t // tm
    e = tm + 2 * HALO

    def body(pp_ref, pm_ref, pn_ref, dp_ref, dm_ref, dn_ref, ap_ref, am_ref, an_ref, yb_ref, yc_ref,
             wot_ref, pw_ref, pwt_ref, ps_ref, sw_ref, cw_ref, cg_ref, cbb_ref,
             dproj_ref, dwin_ref, dwout_ref, dpw_ref, dps_ref, dsw_ref, dcw_ref, dcb_ref, dcg_ref, dcbb_ref,
             p_ext, dz_ext, dy_ext, a1_ext, a0_ext, da1_ext, q_ext, dc_ext, dpn_ext, tmp, ph, da0_s):
        i = pl.program_id(0)
        main = slice(HALO, HALO + tm)

        @pl.when(i == 0)
        def _():
            for r in (dwin_ref, dwout_ref, dpw_ref, dps_ref, dsw_ref, dcw_ref, dcb_ref, dcg_ref, dcbb_ref):
                r[...] = jnp.zeros_like(r)

        _fill_ext(p_ext, pp_ref, pm_ref, pn_ref, i, n_i)
        _fill_ext(dz_ext, dp_ref, dm_ref, dn_ref, i, n_i)
        dy_ext[...] = _dot(dz_ext[...], wot_ref[...])
        _fill_ext(a1_ext, ap_ref, am_ref, an_ref, i, n_i)

        sig_cg = _sigmoid(p_ext[:, C_CG[0]:C_CG[1]])
        a0_ext[...] = p_ext[:, C_CV[0]:C_CV[1]] * sig_cg
        a1_v = a1_ext[...]
        mu = jnp.mean(a1_v, axis=-1, keepdims=True)
        zc = a1_v - mu
        var = jnp.mean(zc * zc, axis=-1, keepdims=True)
        rstd = lax.rsqrt(var + LN_EPS)
        xhat = zc * rstd
        ln = xhat * cg_ref[...] + cbb_ref[...]
        sl = _sigmoid(ln)
        dln = dy_ext[:, 640:1024] * (sl * (1.0 + ln * (1.0 - sl)))
        dcg_ref[...] += jnp.sum((dln * xhat)[main], axis=0, keepdims=True)
        dcbb_ref[...] += jnp.sum(dln[main], axis=0, keepdims=True)
        dxh = dln * cg_ref[...]
        m1 = jnp.mean(dxh, axis=-1, keepdims=True)
        m2 = jnp.mean(dxh * xhat, axis=-1, keepdims=True)
        da1 = rstd * (dxh - m1 - xhat * m2)
        da1_ext[...] = da1
        da1_m = da1[main]
        dcb_ref[...] += jnp.sum(da1_m, axis=0, keepdims=True)
        _fill_phases(ph, da1_ext)
        for c0 in range(0, tm, CONV_ROWS):
            acc = jnp.zeros((CONV_ROWS, CONV_W), F32)
            for o in range(-15, 16):
                acc = acc + cw_ref[15 - o:16 - o, :] * _phase_win(ph, o, c0, CONV_ROWS)
            da0_s[c0:c0 + CONV_ROWS, :] = acc
        da0 = da0_s[...]
        _fill_phases(ph, a0_ext)
        for c0 in range(0, tm, CONV_ROWS):
            da1_c = da1_ext[HALO + c0:HALO + c0 + CONV_ROWS, :]
            for o in range(-15, 16):
                dcw_ref[o + 15:o + 16, :] += jnp.sum(da1_c * _phase_win(ph, o, c0, CONV_ROWS), axis=0, keepdims=True)
        sig_m = sig_cg[main]
        cv_m = p_ext[main, C_CV[0]:C_CV[1]]
        dproj_ref[:, C_CV[0]:C_CV[1]] = (da0 * sig_m).astype(BF16)
        dproj_ref[:, C_CG[0]:C_CG[1]] = (da0 * cv_m * sig_m * (1.0 - sig_m)).astype(BF16)

        q_ext[...] = p_ext[:, C_GC[0]:C_GC[1]] * p_ext[:, C_V[0]:C_V[1]]
        dc_ext[...] = dy_ext[:, 256:640] * p_ext[:, C_GB[0]:C_GB[1]]
        dc_m = dc_ext[main, :]
        conv = jnp.zeros((tm, CONV_W), F32)
        dq = jnp.zeros((tm, CONV_W), F32)
        for k in range(SCONV_K):
            q_k = q_ext[HALO + k - 1:HALO + k - 1 + tm, :]
            conv = conv + sw_ref[k:k + 1, :] * q_k
            dq = dq + sw_ref[k:k + 1, :] * dc_ext[HALO - k + 1:HALO - k + 1 + tm, :]
            dsw_ref[k:k + 1, :] += jnp.sum(dc_m * q_k, axis=0, keepdims=True)
        dproj_ref[:, C_GB[0]:C_GB[1]] = (dy_ext[main, 256:640] * conv).astype(BF16)
        dproj_ref[:, C_GC[0]:C_GC[1]] = (dq * p_ext[main, C_V[0]:C_V[1]]).astype(BF16)
        dproj_ref[:, C_V[0]:C_V[1]] = (dq * p_ext[main, C_GC[0]:C_GC[1]]).astype(BF16)

        t0 = i * tm
        dya = dy_ext[:, 0:256] * ps_ref[...]
        dpooled = _dot(dya.astype(BF16), pwt_ref[...])
        dpn_ext[...] = dpooled * _pool_inv_count(t0 - HALO, e, t)
        half = _pool_lane_half()
        du = jnp.zeros((tm, POOL_W), F32)
        for o, win in _taps(dpn_ext, slice(None), range(-7, 9), tm, tmp):
            m = ((o > -half) & (o <= half)).astype(F32)
            du = du + m * win
        dproj_ref[:, C_POOL[0]:C_POOL[1]] = (du - dpooled[main]).astype(BF16)
        pooled = _pool_forward(p_ext, tm, t0, t, tmp)
        pooled_b = pooled.astype(BF16)
        ya_pre = _dot(pooled_b, pw_ref[...])
        dps_ref[...] += jnp.sum(dy_ext[main, 0:256] * ya_pre, axis=0, keepdims=True)
        dpw_ref[...] += _dot_tn(pooled_b, dya[main].astype(BF16))

        dwin_ref[...] += _dot_tn(yb_ref[...], dproj_ref[...])
        dwout_ref[...] += _dot_tn(yc_ref[...], dm_ref[...])

    one = lambda i: (0, 0)
    tok = lambda i: (i, 0)
    small = [((d, pc), F32), ((ycat.shape[1], d), F32),
             ((POOL_W, POOL_W), F32), ((1, POOL_W), F32), ((8, CONV_W), F32), ((32, CONV_W), F32),
             ((1, CONV_W), F32), ((1, CONV_W), F32), ((1, CONV_W), F32)]
    return pl.pallas_call(
        body, name="mix_bwd",
        grid=(n_i,),
        in_specs=_halo_specs(tm, pc, t) + _halo_specs(tm, dzb.shape[1], t) + _halo_specs(tm, CONV_W, t) + [
            pl.BlockSpec((tm, d), tok), pl.BlockSpec((tm, ycat.shape[1]), tok),
            pl.BlockSpec(w_out_t.shape, one),
            pl.BlockSpec((POOL_W, POOL_W), one), pl.BlockSpec((POOL_W, POOL_W), one), pl.BlockSpec((1, POOL_W), one),
            pl.BlockSpec((8, CONV_W), one), pl.BlockSpec((32, CONV_W), one),
            pl.BlockSpec((1, CONV_W), one), pl.BlockSpec((1, CONV_W), one)],
        out_specs=[pl.BlockSpec((tm, pc), lambda i: (i, 0))] + [pl.BlockSpec(s, one) for s, _ in small],
        out_shape=[jax.ShapeDtypeStruct((t, pc), BF16)] + [jax.ShapeDtypeStruct(s, dt) for s, dt in small],
        scratch_shapes=[pltpu.VMEM((e, pc), F32), pltpu.VMEM((e, dzb.shape[1]), BF16), pltpu.VMEM((e, 1024), F32),
                        pltpu.VMEM((e, CONV_W), F32),
                        pltpu.VMEM((e, CONV_W), F32), pltpu.VMEM((e, CONV_W), F32), pltpu.VMEM((e, CONV_W), F32),
                        pltpu.VMEM((e, CONV_W), F32), pltpu.VMEM((e, POOL_W), F32),
                        pltpu.VMEM((e - 8, CONV_W), F32),
                        pltpu.VMEM((8, e - 8, CONV_W), F32), pltpu.VMEM((tm, CONV_W), F32)],
        compiler_params=_cparams(),
    )(proj, proj, proj, dzb, dzb, dzb, a1, a1, a1, yb, ycat,
      w_out_t, pw_bd, pw_bd_t, pool_scale, sconv_w, cconv_w, cnorm_g, cnorm_b)


def _mesh_pos():
    return lax.axis_index("x"), lax.axis_index("y"), lax.axis_index("c")


def _flip(v, f):
    return 1 - v if f else v


class _Comm:
    def __init__(self, kind, arrs, lands=None, layer=0):
        self.kind = kind
        self.arrs = list(arrs)
        self.n = len(self.arrs)
        self.lands = None if lands is None else list(lands)
        self.layer = layer
        if kind == "gather":
            self.flips = [(1, 0, 0), (0, 1, 0), (1, 1, 0)]
            self.out_shape = [jax.ShapeDtypeStruct((N_CHIPS,) + a.shape, a.dtype) for a in self.arrs]
        else:
            self.flips = [(fx, fy, fc) for fx in (0, 1) for fy in (0, 1) for fc in (0, 1)][1:]
            self.out_shape = [jax.ShapeDtypeStruct(b.shape, b.dtype) for b in self.lands]
        self.n_peer = len(self.flips)

    def operands(self):
        return self.arrs + (self.lands or [])

    def scratch(self):
        return [pltpu.SemaphoreType.DMA((self.n * self.n_peer,)), pltpu.SemaphoreType.DMA((self.n * self.n_peer,)),
                pltpu.SemaphoreType.DMA((self.n,))]

    def aliases(self, in_off, out_off):
        if self.lands is None:
            return {}
        return {in_off + self.n + j: out_off + j for j in range(self.n)}

    def _copies(self, ins, outs, sems):
        send_sems, recv_sems, loc_sems = sems
        x, y, c = _mesh_pos()
        local, sends, recvs = [], [], []
        for a in range(self.n):
            def src(px, py):
                return ins[a] if self.kind == "gather" else ins[a].at[2 * px + py]

            def dst(px, py, pc):
                if self.kind == "gather":
                    return outs[a].at[2 * px + py]
                return outs[a].at[4 * px + 2 * py + pc, self.layer]

            local.append(pltpu.make_async_copy(src(x, y), dst(x, y, c), loc_sems.at[a]))
            for k, (fx, fy, fc) in enumerate(self.flips):
                px, py, pc = _flip(x, fx), _flip(y, fy), _flip(c, fc)
                sem = a * self.n_peer + k
                sends.append(pltpu.make_async_remote_copy(
                    src_ref=src(px, py), dst_ref=dst(x, y, c),
                    send_sem=send_sems.at[sem], recv_sem=recv_sems.at[sem],
                    device_id=(px, py, pc), device_id_type=MESH))
                recvs.append(pltpu.make_async_remote_copy(
                    src_ref=src(px, py), dst_ref=dst(px, py, pc),
                    send_sem=send_sems.at[sem], recv_sem=recv_sems.at[sem],
                    device_id=(px, py, pc), device_id_type=MESH))
        return local, sends, recvs

    def start(self, ins, outs, sems):
        local, sends, _ = self._copies(ins, outs, sems)
        for cp in local + sends:
            cp.start()

    def wait(self, ins, outs, sems):
        local, _, recvs = self._copies(ins, outs, sems)
        for cp in local:
            cp.wait()
        for cp in recvs:
            cp.wait()


def _split_refs(refs, n_in, n_out, n_scr, comm):
    c_in = len(comm.operands()) if comm is not None else 0
    c_out = comm.n if comm is not None else 0
    cuts = [n_in, c_in, n_out, c_out, n_scr]
    out, pos = [], 0
    for m in cuts:
        out.append(refs[pos:pos + m])
        pos += m
    out.append(refs[pos:])
    return out


def _call_with_comm(body, name, grid, in_specs, out_specs, out_shape, scratch, args, comm):
    hbm = pl.BlockSpec(memory_space=pl.ANY)
    aliases = {}
    if comm is not None:
        aliases = comm.aliases(len(in_specs), len(out_specs))
        in_specs = in_specs + [hbm] * len(comm.operands())
        out_specs = out_specs + [hbm] * comm.n
        out_shape = out_shape + comm.out_shape
        scratch = scratch + comm.scratch()
        args = args + comm.operands()
        name = name + "_" + comm.kind
    res = pl.pallas_call(
        body, name=name, grid=grid, in_specs=in_specs, out_specs=out_specs, out_shape=out_shape,
        scratch_shapes=scratch, input_output_aliases=aliases, compiler_params=_cparams(),
    )(*args)
    if comm is None:
        return res, None
    return res[:len(res) - comm.n], res[len(res) - comm.n:]


def run_comm(comm):
    def body(*refs):
        _, c_in, _, c_out, _, c_sem = _split_refs(refs, 0, 0, 0, comm)
        comm.start(c_in, c_out, c_sem)
        comm.wait(c_in, c_out, c_sem)

    hbm = pl.BlockSpec(memory_space=pl.ANY)
    return pl.pallas_call(
        body, name="comm_" + comm.kind,
        in_specs=[hbm] * len(comm.operands()), out_specs=[hbm] * comm.n, out_shape=comm.out_shape,
        scratch_shapes=comm.scratch(), input_output_aliases=comm.aliases(0, 0),
    )(*comm.operands())


def exchange(arrs, per_chip):
    n = len(arrs)
    flips = [(fx, fy, fc) for fx in (0, 1) for fy in (0, 1) for fc in (0, 1)][1:]

    def body(*refs):
        ins, outs = refs[:n], refs[n:2 * n]
        send_sems, recv_sems, loc_sems = refs[2 * n:]
        x, y, c = _mesh_pos()
        me = 4 * x + 2 * y + c
        local = []
        remote = []
        for a in range(n):
            def part(px, py):
                return ins[a].at[2 * px + py] if per_chip else ins[a]

            cp = pltpu.make_async_copy(part(x, y), outs[a].at[me], loc_sems.at[a])
            cp.start()
            local.append(cp)
            for k, (fx, fy, fc) in enumerate(flips):
                px, py, pc = _flip(x, fx), _flip(y, fy), _flip(c, fc)
                sem = a * 7 + k
                rc = pltpu.make_async_remote_copy(
                    src_ref=part(px, py), dst_ref=outs[a].at[me],
                    send_sem=send_sems.at[sem], recv_sem=recv_sems.at[sem],
                    device_id=(px, py, pc), device_id_type=MESH)
                rc.start()
                remote.append(pltpu.make_async_remote_copy(
                    src_ref=part(px, py), dst_ref=outs[a].at[4 * px + 2 * py + pc],
                    send_sem=send_sems.at[sem], recv_sem=recv_sems.at[sem],
                    device_id=(px, py, pc), device_id_type=MESH))
        for cp in local:
            cp.wait()
        for rc in remote:
            rc.wait()

    hbm = pl.BlockSpec(memory_space=pl.ANY)
    shapes = [a.shape[1:] if per_chip else a.shape for a in arrs]
    return pl.pallas_call(
        body, name="exchange_per_chip" if per_chip else "exchange_all",
        in_specs=[hbm] * n, out_specs=[hbm] * n,
        out_shape=[jax.ShapeDtypeStruct((N_DEV,) + s, a.dtype) for s, a in zip(shapes, arrs)],
        scratch_shapes=[pltpu.SemaphoreType.DMA((7 * n,)), pltpu.SemaphoreType.DMA((7 * n,)),
                        pltpu.SemaphoreType.DMA((n,))],
    )(*arrs)


def adamw(parts, w, m, v, name):
    k_n, r, c = parts.shape
    tr = r
    for cand in (512, 256, 128, 64, 32, 16, 8):
        if r % cand == 0:
            tr = cand
            break

    def body(p_ref, w_ref, m_ref, v_ref, g_ref, d_ref, mo_ref, vo_ref):
        g = p_ref[0].astype(F32)
        for k in range(1, k_n):
            g = g + p_ref[k].astype(F32)
        m_new = ADAM_B1 * m_ref[...] + (1.0 - ADAM_B1) * g
        v_new = ADAM_B2 * v_ref[...] + (1.0 - ADAM_B2) * (g * g)
        m_hat = m_new / (1.0 - ADAM_B1 ** ADAM_STEP)
        v_hat = v_new / (1.0 - ADAM_B2 ** ADAM_STEP)
        g_ref[...] = g
        d_ref[...] = -ADAM_LR * (m_hat / (jnp.sqrt(v_hat) + ADAM_EPS) + ADAM_WD * w_ref[...])
        mo_ref[...] = m_new
        vo_ref[...] = v_new

    blk = pl.BlockSpec((tr, c), lambda i: (i, 0))
    return pl.pallas_call(
        body, name=name,
        grid=(r // tr,),
        in_specs=[pl.BlockSpec((k_n, tr, c), lambda i: (0, i, 0)), blk, blk, blk],
        out_specs=[blk, blk, blk, blk],
        out_shape=[jax.ShapeDtypeStruct((r, c), F32)] * 4,
        compiler_params=_cparams(),
    )(parts, w, m, v)


def _block_diag(pool_w):
    out = jnp.zeros((POOL_W, POOL_W), pool_w.dtype)
    for g in range(4):
        out = lax.dynamic_update_slice(out, pool_w[g], (64 * g, 64 * g))
    return out


def _pad_rows(a, rows):
    return jnp.pad(a, ((0, rows - a.shape[0]), (0, 0)))


def kernel(x, ln1_g, ln1_b, ffn1_w_gate, ffn1_w_up, ffn1_w_down, mix_w_in, pool_w, pool_scale, sconv_w, cconv_w, cconv_b, cnorm_g, cnorm_b, mix_w_out, ln2_g, ln2_b, ffn2_w_gate, ffn2_w_up, ffn2_w_down, ln3_g, ln3_b, loss_target, m_ln1_g, m_ln1_b, m_ffn1_w_gate, m_ffn1_w_up, m_ffn1_w_down, m_mix_w_in, m_pool_w, m_pool_scale, m_sconv_w, m_cconv_w, m_cconv_b, m_cnorm_g, m_cnorm_b, m_mix_w_out, m_ln2_g, m_ln2_b, m_ffn2_w_gate, m_ffn2_w_up, m_ffn2_w_down, m_ln3_g, m_ln3_b, v_ln1_g, v_ln1_b, v_ffn1_w_gate, v_ffn1_w_up, v_ffn1_w_down, v_mix_w_in, v_pool_w, v_pool_scale, v_sconv_w, v_cconv_w, v_cconv_b, v_cnorm_g, v_cnorm_b, v_mix_w_out, v_ln2_g, v_ln2_b, v_ffn2_w_gate, v_ffn2_w_up, v_ffn2_w_down, v_ln3_g, v_ln3_b):
    names = ['ln1_g', 'ln1_b', 'ffn1_w_gate', 'ffn1_w_up', 'ffn1_w_down', 'mix_w_in', 'pool_w', 'pool_scale',
             'sconv_w', 'cconv_w', 'cconv_b', 'cnorm_g', 'cnorm_b', 'mix_w_out', 'ln2_g', 'ln2_b',
             'ffn2_w_gate', 'ffn2_w_up', 'ffn2_w_down', 'ln3_g', 'ln3_b']
    w = dict(zip(names, (ln1_g, ln1_b, ffn1_w_gate, ffn1_w_up, ffn1_w_down, mix_w_in, pool_w, pool_scale, sconv_w,
                         cconv_w, cconv_b, cnorm_g, cnorm_b, mix_w_out, ln2_g, ln2_b, ffn2_w_gate, ffn2_w_up,
                         ffn2_w_down, ln3_g, ln3_b)))
    mom_m = dict(zip(names, (m_ln1_g, m_ln1_b, m_ffn1_w_gate, m_ffn1_w_up, m_ffn1_w_down, m_mix_w_in, m_pool_w,
                             m_pool_scale, m_sconv_w, m_cconv_w, m_cconv_b, m_cnorm_g, m_cnorm_b, m_mix_w_out,
                             m_ln2_g, m_ln2_b, m_ffn2_w_gate, m_ffn2_w_up, m_ffn2_w_down, m_ln3_g, m_ln3_b)))
    mom_v = dict(zip(names, (v_ln1_g, v_ln1_b, v_ffn1_w_gate, v_ffn1_w_up, v_ffn1_w_down, v_mix_w_in, v_pool_w,
                             v_pool_scale, v_sconv_w, v_cconv_w, v_cconv_b, v_cnorm_g, v_cnorm_b, v_mix_w_out,
                             v_ln2_g, v_ln2_b, v_ffn2_w_gate, v_ffn2_w_up, v_ffn2_w_down, v_ln3_g, v_ln3_b)))
    big = ['ffn1_w_gate', 'ffn1_w_up', 'ffn1_w_down', 'mix_w_in', 'mix_w_out',
           'ffn2_w_gate', 'ffn2_w_up', 'ffn2_w_down']
    n_l = ln1_g.shape[0]
    d = x.shape[-1]
    fs = ffn1_w_gate.shape[-1]
    ws_in = mix_w_in.shape[-1]
    cs = sconv_w.shape[-1]
    chip = 2 * lax.axis_index("x") + lax.axis_index("y")

    keys_a = ['ffn2_w_gate', 'ffn2_w_up', 'ffn2_w_down', 'mix_w_in', 'mix_w_out']
    keys_b = ['ffn1_w_gate', 'ffn1_w_up', 'ffn1_w_down']

    def shards_b(l):
        return [w[k][l].astype(BF16) for k in keys_b]

    def shards_a(l):
        conv_loc = jnp.concatenate([sconv_w[l], cconv_w[l]], axis=0)
        return [w[k][l].astype(BF16) for k in keys_a] + [conv_loc]

    def cols(a):
        return jnp.transpose(a, (1, 0, 2)).reshape(a.shape[1], -1)

    def layer_weights(l, got_b, got_a):
        gw = dict(zip(keys_b + keys_a, list(got_b) + list(got_a[:-1])))
        conv_all = jnp.transpose(got_a[-1], (1, 0, 2)).reshape(SCONV_K + CCONV_K, N_CHIPS * cs)
        return dict(
            g1=gw['ffn1_w_gate'], u1=gw['ffn1_w_up'], d1=gw['ffn1_w_down'],
            g2=gw['ffn2_w_gate'], u2=gw['ffn2_w_up'], d2=gw['ffn2_w_down'],
            w_in=cols(gw['mix_w_in']),
            w_out=gw['mix_w_out'].reshape(-1, d),
            pw=_block_diag(pool_w[l]).astype(BF16),
            ps=pool_scale[l][None], sw=_pad_rows(conv_all[:SCONV_K], 8), cw=_pad_rows(conv_all[SCONV_K:], 32),
            cb=cconv_b[l][None], cg=cnorm_g[l][None], cbb=cnorm_b[l][None],
        )

    h = x[0]
    hb = h.astype(BF16)
    target = loss_target[0]
    saved = []
    layer_w = []
    got_b = run_comm(_Comm("gather", shards_b(0)))
    for l in range(n_l):
        (y1, y1b, z1, gs1, us1), got_a = ffn_fwd(h, hb, got_b[0], got_b[1], got_b[2], ln1_g[l][None],
                                                 ln1_b[l][None], _Comm("gather", shards_a(l)))
        lw = layer_weights(l, got_b, got_a)
        layer_w.append(lw)
        proj, ycat, a1 = mix_fwd(y1b, lw['w_in'], lw['pw'], lw['ps'], lw['sw'], lw['cw'], lw['cb'], lw['cg'],
                                 lw['cbb'])
        y2, y2b, z2 = out_proj_ln(ycat, lw['w_out'], y1, ln2_g[l][None], ln2_b[l][None])
        comm = _Comm("gather", shards_b(l + 1)) if l + 1 < n_l else None
        (y3, y3b, z3, gs2, us2), got_b = ffn_fwd(y2, y2b, lw['g2'], lw['u2'], lw['d2'], ln3_g[l][None],
                                                 ln3_b[l][None], comm)
        saved.append(dict(x0b=hb, z1=z1, y1b=y1b, proj=proj, ycat=ycat, a1=a1, z2=z2, y2b=y2b, z3=z3,
                          gs1=gs1, us1=us1, gs2=gs2, us2=us2))
        h, hb = y3, y3b

    dy, loss_blk = loss_and_grad(h, target)
    loss = lax.psum(loss_blk[0, 0], ("x", "y", "c"))

    g_loc = {k: [None] * n_l for k in names if k not in big}
    lands = {k: lax.empty((N_DEV, n_l) + w[k].shape[1:], BF16) for k in big}

    def scatter(keys, arrs, layer):
        return _Comm("scatter", arrs, [lands[k] for k in keys], layer)

    pend_b = None
    parts = None
    for l in reversed(range(n_l)):
        lw, sv = layer_w[l], saved[l]
        dzb, dxr, g_loc['ln3_g'][l], g_loc['ln3_b'][l] = ln_bwd(dy, parts, sv['z3'], ln3_g[l][None])
        comm = None if pend_b is None else scatter(keys_b, pend_b, l + 1)
        (parts, dg2, du2, dd2), landed = ffn_bwd(sv['y2b'], dzb, sv['gs2'], sv['us2'], lw['g2'], lw['u2'],
                                                         lw['d2'], comm)
        if comm is not None:
            lands.update(zip(keys_b, landed))
        dzb, dxr, g_loc['ln2_g'][l], g_loc['ln2_b'][l] = ln_bwd(dxr, parts, sv['z2'], ln2_g[l][None])
        (dproj, dw_in, dw_out, dpw, g_loc['pool_scale'][l], dsw, dcw, g_loc['cconv_b'][l], g_loc['cnorm_g'][l],
         g_loc['cnorm_b'][l]) = mix_bwd(sv['proj'], dzb, lw['w_out'].T, sv['a1'], sv['y1b'], sv['ycat'],
                                        lw['pw'], lw['pw'].T, lw['ps'], lw['sw'], lw['cw'], lw['cg'], lw['cbb'])
        g_loc['pool_w'][l] = jnp.stack([dpw[64 * g:64 * g + 64, 64 * g:64 * g + 64] for g in range(4)])
        g_loc['sconv_w'][l] = dsw[:SCONV_K]
        g_loc['cconv_w'][l] = dcw[:CCONV_K]
        dw_in_c = jnp.transpose(dw_in.reshape(d, N_CHIPS, ws_in), (1, 0, 2)).astype(BF16)
        dw_out_c = dw_out.reshape(N_CHIPS, -1, d).astype(BF16)
        dzb, dxr, g_loc['ln1_g'][l], g_loc['ln1_b'][l] = ln_bwd(dxr, None, sv['z1'], ln1_g[l][None],
                                                                 mm=(dproj, lw['w_in'].T))
        comm = scatter(keys_a, [dg2, du2, dd2, dw_in_c, dw_out_c], l)
        (parts, dg1, du1, dd1), landed = ffn_bwd(sv['x0b'], dzb, sv['gs1'], sv['us1'], lw['g1'], lw['u1'],
                                                         lw['d1'], comm)
        lands.update(zip(keys_a, landed))
        pend_b = [dg1, du1, dd1]
        dy = dxr
    grad_x = add_parts(dy, parts)[None]
    lands.update(zip(keys_b, run_comm(scatter(keys_b, pend_b, 0))))
    parts_big = lands

    small = [k for k in names if k not in big]
    small_full = {}
    for k in small:
        a = jnp.stack(g_loc[k])
        small_full[k] = a.reshape(n_l, -1) if a.shape[1] == 1 else a
    flat = jnp.concatenate([small_full[k].reshape(-1) for k in small])
    n_flat = flat.shape[0]
    rows = -(-n_flat // (SMALL_ROWS * 128)) * SMALL_ROWS
    flat = jnp.pad(flat, (0, rows * 128 - n_flat)).reshape(rows, 128)
    parts_small = exchange([flat], per_chip=False)[0]

    out_g, out_d, out_m, out_v = {}, {}, {}, {}
    for k in big:
        shp = w[k].shape
        r = shp[0] * shp[1]
        res = adamw(parts_big[k].reshape(N_DEV, r, shp[2]), w[k].reshape(r, shp[2]),
                    mom_m[k].reshape(r, shp[2]), mom_v[k].reshape(r, shp[2]), name="adamw_" + k)
        out_g[k], out_d[k], out_m[k], out_v[k] = [o.reshape(shp) for o in res]

    zeros = jnp.zeros((rows, 128), F32)
    g_sum = adamw(parts_small, zeros, zeros, zeros, name="sum_small")[0].reshape(-1)
    off = 0
    for k in small:
        full = small_full[k]
        g = g_sum[off:off + full.size].reshape(full.shape)
        off += full.size
        if k in ('sconv_w', 'cconv_w'):
            g = lax.dynamic_slice_in_dim(g, chip * cs, cs, axis=2)
        out_g[k] = g.reshape(w[k].shape)

    def pack(dct):
        f = jnp.concatenate([dct[k].reshape(-1) for k in small])
        r2 = -(-f.shape[0] // (SMALL_ROWS * 128)) * SMALL_ROWS
        return jnp.pad(f, (0, r2 * 128 - f.shape[0])).reshape(r2, 128), f.shape[0]

    gp, n_small = pack(out_g)
    wp, _ = pack(w)
    mp, _ = pack(mom_m)
    vp, _ = pack(mom_v)
    _, dp, mo, vo = adamw(gp[None], wp, mp, vp, name="adamw_small")
    off = 0
    for k in small:
        sz = w[k].size
        out_d[k] = dp.reshape(-1)[off:off + sz].reshape(w[k].shape)
        out_m[k] = mo.reshape(-1)[off:off + sz].reshape(w[k].shape)
        out_v[k] = vo.reshape(-1)[off:off + sz].reshape(w[k].shape)
        off += sz

    return (loss, grad_x, *[out_g[k] for k in names], *[out_d[k] for k in names],
            *[out_m[k] for k in names], *[out_v[k] for k in names])
```

```python
import jax
import jax.numpy as jnp
from jax import lax
from jax.experimental import pallas as pl
from jax.experimental.pallas import tpu as pltpu

F32 = jnp.float32
BF16 = jnp.bfloat16

DEPTH = 4
ALPHA = (2.0 * DEPTH) ** 0.25
LN_EPS = 1e-5
POOL_W = 256
CONV_W = 384
SCONV_K = 3
CCONV_K = 31
C_POOL = (0, 256)
C_GB = (256, 640)
C_GC = (640, 1024)
C_V = (1024, 1408)
C_CV = (1408, 1792)
C_CG = (1792, 2176)

ADAM_LR = 0.001
ADAM_B1 = 0.9
ADAM_B2 = 0.999
ADAM_EPS = 1e-08
ADAM_WD = 0.01
ADAM_STEP = 10

N_CHIPS = 4
N_DEV = 8
MESH = pl.DeviceIdType.MESH

TM_FWD = 512
TM_BWD = 512
TM_MIX = 256
HALO = 32
SMALL_ROWS = 256
VMEM_LIMIT = 56 * 1024 * 1024


def _cparams():
    return pltpu.CompilerParams(vmem_limit_bytes=VMEM_LIMIT)


def _sigmoid(v):
    return 1.0 / (1.0 + jnp.exp(-v))


def _dot(a, b):
    return jnp.dot(a, b, preferred_element_type=F32)


def _dot_nt(a, b):
    return lax.dot_general(a, b, (((1,), (1,)), ((), ())), preferred_element_type=F32)


def _dot_tn(a, b):
    return lax.dot_general(a, b, (((0,), (0,)), ((), ())), preferred_element_type=F32)


def _ln_fwd(z, g, b):
    mu = jnp.mean(z, axis=-1, keepdims=True)
    zc = z - mu
    var = jnp.mean(zc * zc, axis=-1, keepdims=True)
    return zc * lax.rsqrt(var + LN_EPS) * g + b


def _ln_bwd(dy, z, g):
    mu = jnp.mean(z, axis=-1, keepdims=True)
    zc = z - mu
    var = jnp.mean(zc * zc, axis=-1, keepdims=True)
    rstd = lax.rsqrt(var + LN_EPS)
    xhat = zc * rstd
    dxh = dy * g
    m1 = jnp.mean(dxh, axis=-1, keepdims=True)
    m2 = jnp.mean(dxh * xhat, axis=-1, keepdims=True)
    return rstd * (dxh - m1 - xhat * m2), xhat


def _tile(t, tm):
    tm = min(tm, t)
    assert t % tm == 0, (t, tm)
    return tm


def ffn_fwd(x, xb, wg, wu, wd, ln_g, ln_b, comm=None):
    t, d = x.shape
    s_n, _, fs = wg.shape
    tm = _tile(t, TM_FWD)
    n_i = t // tm

    def body(*refs):
        ((x_ref, xb_ref, wg_ref, wu_ref, wd_ref, g_ref, b_ref), c_in, (y_ref, yb_ref, z_ref, gs_ref, us_ref), c_out,
         (acc_ref,), c_sem) = _split_refs(refs, 7, 5, 1, comm)
        i = pl.program_id(0)
        s = pl.program_id(1)
        if comm is not None:
            @pl.when(jnp.logical_and(i == 0, s == 0))
            def _():
                comm.start(c_in, c_out, c_sem)

        xb = xb_ref[...]
        g = _dot(xb, wg_ref[0])
        u = _dot(xb, wu_ref[0])
        gs_ref[0] = g.astype(BF16)
        us_ref[0] = u.astype(BF16)
        a = (g * _sigmoid(g) * u).astype(BF16)
        part = _dot(a, wd_ref[0])

        @pl.when(s == 0)
        def _():
            acc_ref[...] = part

        @pl.when(s > 0)
        def _():
            acc_ref[...] += part

        @pl.when(s == s_n - 1)
        def _():
            z = ALPHA * x_ref[...] + 0.5 * acc_ref[...]
            z_ref[...] = z
            y = _ln_fwd(z, g_ref[...], b_ref[...])
            y_ref[...] = y
            yb_ref[...] = y.astype(BF16)

        if comm is not None:
            @pl.when(jnp.logical_and(i == n_i - 1, s == s_n - 1))
            def _():
                comm.wait(c_in, c_out, c_sem)

    tok = lambda i, s: (i, 0)
    one = lambda i, s: (0, 0)
    return _call_with_comm(
        body, "ffn_fwd", (n_i, s_n),
        [pl.BlockSpec((tm, d), tok), pl.BlockSpec((tm, d), tok),
         pl.BlockSpec((1, d, fs), lambda i, s: (s, 0, 0)),
         pl.BlockSpec((1, d, fs), lambda i, s: (s, 0, 0)),
         pl.BlockSpec((1, fs, d), lambda i, s: (s, 0, 0)),
         pl.BlockSpec((1, d), one), pl.BlockSpec((1, d), one)],
        [pl.BlockSpec((tm, d), tok), pl.BlockSpec((tm, d), tok), pl.BlockSpec((tm, d), tok),
         pl.BlockSpec((1, tm, fs), lambda i, s: (s, i, 0)), pl.BlockSpec((1, tm, fs), lambda i, s: (s, i, 0))],
        [jax.ShapeDtypeStruct((t, d), F32), jax.ShapeDtypeStruct((t, d), BF16), jax.ShapeDtypeStruct((t, d), F32),
         jax.ShapeDtypeStruct((s_n, t, fs), BF16), jax.ShapeDtypeStruct((s_n, t, fs), BF16)],
        [pltpu.VMEM((tm, d), F32)],
        [x, xb, wg, wu, wd, ln_g, ln_b], comm)


def ffn_bwd(xb, dzb, gs, us, wg, wu, wd, comm=None):
    t, d = xb.shape
    s_n, _, fs = wg.shape
    tm = _tile(t, TM_BWD)
    n_i = t // tm

    def body(*refs):
        ((x_ref, dzb_ref, gs_ref, us_ref, wg_ref, wu_ref, wd_ref), c_in, (dx_ref, dwg_ref, dwu_ref, dwd_ref), c_out,
         (accg, accu, accd), c_sem) = _split_refs(refs, 7, 4, 3, comm)
        s = pl.program_id(0)
        i = pl.program_id(1)
        if comm is not None:
            @pl.when(jnp.logical_and(i == 0, s == 0))
            def _():
                comm.start(c_in, c_out, c_sem)

        @pl.when(i == 0)
        def _():
            accg[...] = jnp.zeros_like(accg)
            accu[...] = jnp.zeros_like(accu)
            accd[...] = jnp.zeros_like(accd)

        x_v = x_ref[...]
        g = gs_ref[0].astype(F32)
        u = us_ref[0].astype(F32)
        sg = _sigmoid(g)
        si = g * sg
        a = (si * u).astype(BF16)
        dfb = dzb_ref[...] * 0.5
        da = _dot_nt(dfb, wd_ref[0])
        dgate = (da * u * (sg * (1.0 + g * (1.0 - sg)))).astype(BF16)
        dup = (da * si).astype(BF16)
        dx_ref[0] = (_dot_nt(dgate, wg_ref[0]) + _dot_nt(dup, wu_ref[0])).astype(BF16)
        accg[...] += _dot_tn(x_v, dgate)
        accu[...] += _dot_tn(x_v, dup)
        accd[...] += _dot_tn(a, dfb)

        @pl.when(i == n_i - 1)
        def _():
            dwg_ref[0] = accg[...].astype(BF16)
            dwu_ref[0] = accu[...].astype(BF16)
            dwd_ref[0] = accd[...].astype(BF16)

        if comm is not None:
            @pl.when(jnp.logical_and(i == n_i - 1, s == s_n - 1))
            def _():
                comm.wait(c_in, c_out, c_sem)

    tok = lambda s, i: (i, 0)
    shard = lambda s, i: (s, 0, 0)
    return _call_with_comm(
        body, "ffn_bwd", (s_n, n_i),
        [pl.BlockSpec((tm, d), tok), pl.BlockSpec((tm, d), tok),
         pl.BlockSpec((1, tm, fs), lambda s, i: (s, i, 0)), pl.BlockSpec((1, tm, fs), lambda s, i: (s, i, 0)),
         pl.BlockSpec((1, d, fs), shard), pl.BlockSpec((1, d, fs), shard), pl.BlockSpec((1, fs, d), shard)],
        [pl.BlockSpec((1, tm, d), lambda s, i: (s, i, 0)),
         pl.BlockSpec((1, d, fs), shard), pl.BlockSpec((1, d, fs), shard), pl.BlockSpec((1, fs, d), shard)],
        [jax.ShapeDtypeStruct((s_n, t, d), BF16),
         jax.ShapeDtypeStruct((s_n, d, fs), BF16), jax.ShapeDtypeStruct((s_n, d, fs), BF16),
         jax.ShapeDtypeStruct((s_n, fs, d), BF16)],
        [pltpu.VMEM((d, fs), F32), pltpu.VMEM((d, fs), F32), pltpu.VMEM((fs, d), F32)],
        [xb, dzb, gs, us, wg, wu, wd], comm)


def _sum_parts(base_ref, parts_ref):
    v = base_ref[...]
    if parts_ref is not None:
        for p in range(parts_ref.shape[0]):
            v = v + parts_ref[p].astype(F32)
    return v


def ln_bwd(dy, parts, z, ln_g, mm=None):
    t, d = dy.shape
    tm = _tile(t, TM_FWD)

    def body(*refs):
        refs = list(refs)
        dy_ref = refs.pop(0)
        parts_ref = refs.pop(0) if parts is not None else None
        a_ref, w_ref = (refs.pop(0), refs.pop(0)) if mm is not None else (None, None)
        z_ref, g_ref, dzb_ref, dxr_ref, dg_ref, db_ref = refs
        i = pl.program_id(0)
        dy_v = _sum_parts(dy_ref, parts_ref)
        if mm is not None:
            dy_v = dy_v + _dot(a_ref[...], w_ref[...])
        dz, xhat = _ln_bwd(dy_v, z_ref[...], g_ref[...])
        dzb_ref[...] = dz.astype(BF16)
        dxr_ref[...] = ALPHA * dz

        @pl.when(i == 0)
        def _():
            dg_ref[...] = jnp.zeros_like(dg_ref)
            db_ref[...] = jnp.zeros_like(db_ref)

        dg_ref[...] += jnp.sum(dy_v * xhat, axis=0, keepdims=True)
        db_ref[...] += jnp.sum(dy_v, axis=0, keepdims=True)

    tok = lambda i: (i, 0)
    one = lambda i: (0, 0)
    in_specs = [pl.BlockSpec((tm, d), tok)]
    args = [dy]
    name = "ln_bwd"
    if parts is not None:
        in_specs.append(pl.BlockSpec((parts.shape[0], tm, d), lambda i: (0, i, 0)))
        args.append(parts)
        name += "_parts"
    if mm is not None:
        in_specs += [pl.BlockSpec((tm, mm[0].shape[1]), tok), pl.BlockSpec(mm[1].shape, one)]
        args += list(mm)
        name += "_mm"
    return pl.pallas_call(
        body, name=name,
        grid=(t // tm,),
        in_specs=in_specs + [pl.BlockSpec((tm, d), tok), pl.BlockSpec((1, d), one)],
        out_specs=[pl.BlockSpec((tm, d), tok), pl.BlockSpec((tm, d), tok),
                   pl.BlockSpec((1, d), one), pl.BlockSpec((1, d), one)],
        out_shape=[jax.ShapeDtypeStruct((t, d), BF16), jax.ShapeDtypeStruct((t, d), F32),
                   jax.ShapeDtypeStruct((1, d), F32), jax.ShapeDtypeStruct((1, d), F32)],
        compiler_params=_cparams(),
    )(*args, z, ln_g)


def add_parts(base, parts):
    t, d = base.shape
    tm = _tile(t, TM_FWD)

    def body(b_ref, p_ref, o_ref):
        o_ref[...] = _sum_parts(b_ref, p_ref)

    tok = lambda i: (i, 0)
    return pl.pallas_call(
        body, name="add_parts",
        grid=(t // tm,),
        in_specs=[pl.BlockSpec((tm, d), tok), pl.BlockSpec((parts.shape[0], tm, d), lambda i: (0, i, 0))],
        out_specs=pl.BlockSpec((tm, d), tok),
        out_shape=jax.ShapeDtypeStruct((t, d), F32),
        compiler_params=_cparams(),
    )(base, parts)


def loss_and_grad(y, target):
    t, d = y.shape
    tm = _tile(t, TM_FWD)

    def body(y_ref, t_ref, dy_ref, l_ref):
        i = pl.program_id(0)
        e = y_ref[...] - t_ref[...]
        dy_ref[...] = e * (1.0 / d)

        @pl.when(i == 0)
        def _():
            l_ref[...] = jnp.zeros_like(l_ref)

        l_ref[...] += (0.5 / d) * jnp.sum(e * e)

    tok = lambda i: (i, 0)
    return pl.pallas_call(
        body, name="loss",
        grid=(t // tm,),
        in_specs=[pl.BlockSpec((tm, d), tok), pl.BlockSpec((tm, d), tok)],
        out_specs=[pl.BlockSpec((tm, d), tok), pl.BlockSpec((8, 128), lambda i: (0, 0))],
        out_shape=[jax.ShapeDtypeStruct((t, d), F32), jax.ShapeDtypeStruct((8, 128), F32)],
        compiler_params=_cparams(),
    )(y, target)


def _halo_specs(tm, cols, n_rows):
    r = tm // HALO
    last = n_rows // HALO - 1
    return [pl.BlockSpec((HALO, cols), lambda i: (jnp.maximum(i * r - 1, 0), 0)),
            pl.BlockSpec((tm, cols), lambda i: (i, 0)),
            pl.BlockSpec((HALO, cols), lambda i: (jnp.minimum((i + 1) * r, last), 0))]


def _fill_ext(dst, prev_ref, main_ref, next_ref, i, n_i):
    tm = main_ref.shape[0]
    dst[0:HALO, :] = jnp.where(i > 0, prev_ref[...], 0.0)
    dst[HALO:HALO + tm, :] = main_ref[...]
    dst[HALO + tm:HALO + tm + HALO, :] = jnp.where(i < n_i - 1, next_ref[...], 0.0)


def _pool_lane_half():
    lane = lax.broadcasted_iota(jnp.int32, (1, POOL_W), 1)
    return jnp.left_shift(1, lane // 64)


def _pool_inv_count(t0, rows, seq):
    half = _pool_lane_half()
    tpos = t0 + lax.broadcasted_iota(jnp.int32, (rows, 1), 0)
    lo = jnp.maximum(tpos - half, 0)
    hi = jnp.minimum(tpos + half, seq)
    cnt = jnp.maximum(hi - lo, 1)
    return 1.0 / cnt.astype(F32)


def _taps(ext_ref, cols, offsets, tm, tmp_ref):
    e = ext_ref.shape[0]
    width = len(range(*cols.indices(ext_ref.shape[1])))
    by_phase = {}
    for o in offsets:
        by_phase.setdefault((HALO + o) % 8, []).append(o)
    for r, group in by_phase.items():
        if r == 0:
            for o in group:
                yield o, ext_ref[HALO + o:HALO + o + tm, cols]
            continue
        tmp_ref[:, 0:width] = ext_ref[r:r + e - 8, cols]
        for o in group:
            start = HALO + o - r
            yield o, tmp_ref[start:start + tm, 0:width]


def _pool_forward(p_ext, tm, t0, seq, tmp_ref):
    half = _pool_lane_half()
    total = jnp.zeros((tm, POOL_W), F32)
    for o, win in _taps(p_ext, slice(C_POOL[0], C_POOL[1]), range(-8, 8), tm, tmp_ref):
        m = ((o >= -half) & (o < half)).astype(F32)
        total = total + m * win
    u = p_ext[HALO:HALO + tm, C_POOL[0]:C_POOL[1]]
    return total * _pool_inv_count(t0, tm, seq) - u


CONV_ROWS = 64


def _fill_phases(ph_ref, ext_ref):
    e = ext_ref.shape[0]
    for r in range(8):
        ph_ref[r] = ext_ref[r:r + e - 8, :]


def _phase_win(ph_ref, o, row0, rows):
    r = (HALO + o) % 8
    start = HALO + o - r + row0
    return ph_ref[r, start:start + rows, :]


def mix_fwd(y, yb, w_in, w_out, ln_g, ln_b, pw_bd, pool_scale, sconv_w, cconv_w, cconv_b, cnorm_g, cnorm_b):
    t, d = yb.shape
    pc = w_in.shape[1]
    tm = _tile(t, TM_MIX)
    n_i = t // tm
    e = tm + 2 * HALO

    def body(yp_ref, ym_ref, yn_ref, y_ref, win_ref, wout_ref, lg_ref, lb_ref,
             pw_ref, ps_ref, sw_ref, cw_ref, cb_ref, cg_ref, cbb_ref,
             proj_ref, yc_ref, a1_ref, y2_ref, y2b_ref, z2_ref, yb_ext, p_ext, q_ext, a0_ext, tmp, ph, a1_s):
        i = pl.program_id(0)
        _fill_ext(yb_ext, yp_ref, ym_ref, yn_ref, i, n_i)
        p_ext[...] = _dot(yb_ext[...], win_ref[...])
        proj_ref[...] = p_ext[HALO:HALO + tm, :]
        pooled = _pool_forward(p_ext, tm, i * tm, t, tmp)
        y_a = _dot(pooled.astype(BF16), pw_ref[...]) * ps_ref[...]
        yc_ref[:, 0:256] = y_a.astype(BF16)
        q_ext[...] = p_ext[:, C_GC[0]:C_GC[1]] * p_ext[:, C_V[0]:C_V[1]]
        conv = jnp.zeros((tm, CONV_W), F32)
        for k in range(SCONV_K):
            conv = conv + sw_ref[k:k + 1, :] * q_ext[HALO + k - 1:HALO + k - 1 + tm, :]
        y_b = p_ext[HALO:HALO + tm, C_GB[0]:C_GB[1]] * conv
        yc_ref[:, 256:640] = y_b.astype(BF16)
        a0_ext[...] = p_ext[:, C_CV[0]:C_CV[1]] * _sigmoid(p_ext[:, C_CG[0]:C_CG[1]])
        _fill_phases(ph, a0_ext)
        for c0 in range(0, tm, CONV_ROWS):
            acc = jnp.zeros((CONV_ROWS, CONV_W), F32) + cb_ref[...]
            for o in range(-15, 16):
                acc = acc + cw_ref[o + 15:o + 16, :] * _phase_win(ph, o, c0, CONV_ROWS)
            a1_s[c0:c0 + CONV_ROWS, :] = acc
        a1 = a1_s[...]
        a1_ref[...] = a1
        ln = _ln_fwd(a1, cg_ref[...], cbb_ref[...])
        yc_ref[:, 640:1024] = (ln * _sigmoid(ln)).astype(BF16)
        z = ALPHA * y_ref[...] + _dot(yc_ref[...], wout_ref[...])
        z2_ref[...] = z
        y2 = _ln_fwd(z, lg_ref[...], lb_ref[...])
        y2_ref[...] = y2
        y2b_ref[...] = y2.astype(BF16)

    one = lambda i: (0, 0)
    tok = lambda i: (i, 0)
    return pl.pallas_call(
        body, name="mix_fwd",
        grid=(n_i,),
        in_specs=_halo_specs(tm, d, t) + [
            pl.BlockSpec((tm, d), tok), pl.BlockSpec((d, pc), one), pl.BlockSpec(w_out.shape, one),
            pl.BlockSpec((1, d), one), pl.BlockSpec((1, d), one),
            pl.BlockSpec((POOL_W, POOL_W), one), pl.BlockSpec((1, POOL_W), one),
            pl.BlockSpec((8, CONV_W), one), pl.BlockSpec((32, CONV_W), one),
            pl.BlockSpec((1, CONV_W), one), pl.BlockSpec((1, CONV_W), one), pl.BlockSpec((1, CONV_W), one)],
        out_specs=[pl.BlockSpec((tm, pc), tok), pl.BlockSpec((tm, 1024), tok), pl.BlockSpec((tm, CONV_W), tok),
                   pl.BlockSpec((tm, d), tok), pl.BlockSpec((tm, d), tok), pl.BlockSpec((tm, d), tok)],
        out_shape=[jax.ShapeDtypeStruct((t, pc), F32), jax.ShapeDtypeStruct((t, 1024), BF16),
                   jax.ShapeDtypeStruct((t, CONV_W), F32),
                   jax.ShapeDtypeStruct((t, d), F32), jax.ShapeDtypeStruct((t, d), BF16),
                   jax.ShapeDtypeStruct((t, d), F32)],
        scratch_shapes=[pltpu.VMEM((e, d), BF16), pltpu.VMEM((e, pc), F32), pltpu.VMEM((e, CONV_W), F32),
                        pltpu.VMEM((e, CONV_W), F32), pltpu.VMEM((e - 8, CONV_W), F32),
                        pltpu.VMEM((8, e - 8, CONV_W), F32), pltpu.VMEM((tm, CONV_W), F32)],
        compiler_params=_cparams(),
    )(yb, yb, yb, y, w_in, w_out, ln_g, ln_b, pw_bd, pool_scale, sconv_w, cconv_w, cconv_b, cnorm_g, cnorm_b)


def mix_bwd(proj, dzb, w_out_t, a1, yb, ycat, pw_bd, pw_bd_t, pool_scale, sconv_w, cconv_w, cnorm_g, cnorm_b):
    t, pc = proj.shape
    d = yb.shape[1]
    tm = _tile(t, TM_MIX)
    n_i = t // tm
    e = tm + 2 * HALO

    def body(pp_ref, pm_ref, pn_ref, dp_ref, dm_ref, dn_ref, ap_ref, am_ref, an_ref, yb_ref, yc_ref,
             wot_ref, pw_ref, pwt_ref, ps_ref, sw_ref, cw_ref, cg_ref, cbb_ref,
             dproj_ref, dwin_ref, dwout_ref, dpw_ref, dps_ref, dsw_ref, dcw_ref, dcb_ref, dcg_ref, dcbb_ref,
             p_ext, dz_ext, dy_ext, a1_ext, a0_ext, da1_ext, q_ext, dc_ext, dpn_ext, tmp, ph, da0_s):
        i = pl.program_id(0)
        main = slice(HALO, HALO + tm)

        @pl.when(i == 0)
        def _():
            for r in (dwin_ref, dwout_ref, dpw_ref, dps_ref, dsw_ref, dcw_ref, dcb_ref, dcg_ref, dcbb_ref):
                r[...] = jnp.zeros_like(r)

        _fill_ext(p_ext, pp_ref, pm_ref, pn_ref, i, n_i)
        _fill_ext(dz_ext, dp_ref, dm_ref, dn_ref, i, n_i)
        dy_ext[...] = _dot(dz_ext[...], wot_ref[...])
        _fill_ext(a1_ext, ap_ref, am_ref, an_ref, i, n_i)

        sig_cg = _sigmoid(p_ext[:, C_CG[0]:C_CG[1]])
        a0_ext[...] = p_ext[:, C_CV[0]:C_CV[1]] * sig_cg
        a1_v = a1_ext[...]
        mu = jnp.mean(a1_v, axis=-1, keepdims=True)
        zc = a1_v - mu
        var = jnp.mean(zc * zc, axis=-1, keepdims=True)
        rstd = lax.rsqrt(var + LN_EPS)
        xhat = zc * rstd
        ln = xhat * cg_ref[...] + cbb_ref[...]
        sl = _sigmoid(ln)
        dln = dy_ext[:, 640:1024] * (sl * (1.0 + ln * (1.0 - sl)))
        dcg_ref[...] += jnp.sum((dln * xhat)[main], axis=0, keepdims=True)
        dcbb_ref[...] += jnp.sum(dln[main], axis=0, keepdims=True)
        dxh = dln * cg_ref[...]
        m1 = jnp.mean(dxh, axis=-1, keepdims=True)
        m2 = jnp.mean(dxh * xhat, axis=-1, keepdims=True)
        da1 = rstd * (dxh - m1 - xhat * m2)
        da1_ext[...] = da1
        da1_m = da1[main]
        dcb_ref[...] += jnp.sum(da1_m, axis=0, keepdims=True)
        _fill_phases(ph, da1_ext)
        for c0 in range(0, tm, CONV_ROWS):
            acc = jnp.zeros((CONV_ROWS, CONV_W), F32)
            for o in range(-15, 16):
                acc = acc + cw_ref[15 - o:16 - o, :] * _phase_win(ph, o, c0, CONV_ROWS)
            da0_s[c0:c0 + CONV_ROWS, :] = acc
        da0 = da0_s[...]
        _fill_phases(ph, a0_ext)
        for c0 in range(0, tm, CONV_ROWS):
            da1_c = da1_ext[HALO + c0:HALO + c0 + CONV_ROWS, :]
            for o in range(-15, 16):
                dcw_ref[o + 15:o + 16, :] += jnp.sum(da1_c * _phase_win(ph, o, c0, CONV_ROWS), axis=0, keepdims=True)
        sig_m = sig_cg[main]
        cv_m = p_ext[main, C_CV[0]:C_CV[1]]
        dproj_ref[:, C_CV[0]:C_CV[1]] = (da0 * sig_m).astype(BF16)
        dproj_ref[:, C_CG[0]:C_CG[1]] = (da0 * cv_m * sig_m * (1.0 - sig_m)).astype(BF16)

        q_ext[...] = p_ext[:, C_GC[0]:C_GC[1]] * p_ext[:, C_V[0]:C_V[1]]
        dc_ext[...] = dy_ext[:, 256:640] * p_ext[:, C_GB[0]:C_GB[1]]
        dc_m = dc_ext[main, :]
        conv = jnp.zeros((tm, CONV_W), F32)
        dq = jnp.zeros((tm, CONV_W), F32)
        for k in range(SCONV_K):
            q_k = q_ext[HALO + k - 1:HALO + k - 1 + tm, :]
            conv = conv + sw_ref[k:k + 1, :] * q_k
            dq = dq + sw_ref[k:k + 1, :] * dc_ext[HALO - k + 1:HALO - k + 1 + tm, :]
            dsw_ref[k:k + 1, :] += jnp.sum(dc_m * q_k, axis=0, keepdims=True)
        dproj_ref[:, C_GB[0]:C_GB[1]] = (dy_ext[main, 256:640] * conv).astype(BF16)
        dproj_ref[:, C_GC[0]:C_GC[1]] = (dq * p_ext[main, C_V[0]:C_V[1]]).astype(BF16)
        dproj_ref[:, C_V[0]:C_V[1]] = (dq * p_ext[main, C_GC[0]:C_GC[1]]).astype(BF16)

        t0 = i * tm
        dya = dy_ext[:, 0:256] * ps_ref[...]
        dpooled = _dot(dya.astype(BF16), pwt_ref[...])
        dpn_ext[...] = dpooled * _pool_inv_count(t0 - HALO, e, t)
        half = _pool_lane_half()
        du = jnp.zeros((tm, POOL_W), F32)
        for o, win in _taps(dpn_ext, slice(None), range(-7, 9), tm, tmp):
            m = ((o > -half) & (o <= half)).astype(F32)
            du = du + m * win
        dproj_ref[:, C_POOL[0]:C_POOL[1]] = (du - dpooled[main]).astype(BF16)
        pooled = _pool_forward(p_ext, tm, t0, t, tmp)
        pooled_b = pooled.astype(BF16)
        ya_pre = _dot(pooled_b, pw_ref[...])
        dps_ref[...] += jnp.sum(dy_ext[main, 0:256] * ya_pre, axis=0, keepdims=True)
        dpw_ref[...] += _dot_tn(pooled_b, dya[main].astype(BF16))

        dwin_ref[...] += _dot_tn(yb_ref[...], dproj_ref[...])
        dwout_ref[...] += _dot_tn(yc_ref[...], dm_ref[...])

    one = lambda i: (0, 0)
    tok = lambda i: (i, 0)
    small = [((d, pc), F32), ((ycat.shape[1], d), F32),
             ((POOL_W, POOL_W), F32), ((1, POOL_W), F32), ((8, CONV_W), F32), ((32, CONV_W), F32),
             ((1, CONV_W), F32), ((1, CONV_W), F32), ((1, CONV_W), F32)]
    return pl.pallas_call(
        body, name="mix_bwd",
        grid=(n_i,),
        in_specs=_halo_specs(tm, pc, t) + _halo_specs(tm, dzb.shape[1], t) + _halo_specs(tm, CONV_W, t) + [
            pl.BlockSpec((tm, d), tok), pl.BlockSpec((tm, ycat.shape[1]), tok),
            pl.BlockSpec(w_out_t.shape, one),
            pl.BlockSpec((POOL_W, POOL_W), one), pl.BlockSpec((POOL_W, POOL_W), one), pl.BlockSpec((1, POOL_W), one),
            pl.BlockSpec((8, CONV_W), one), pl.BlockSpec((32, CONV_W), one),
            pl.BlockSpec((1, CONV_W), one), pl.BlockSpec((1, CONV_W), one)],
        out_specs=[pl.BlockSpec((tm, pc), lambda i: (i, 0))] + [pl.BlockSpec(s, one) for s, _ in small],
        out_shape=[jax.ShapeDtypeStruct((t, pc), BF16)] + [jax.ShapeDtypeStruct(s, dt) for s, dt in small],
        scratch_shapes=[pltpu.VMEM((e, pc), F32), pltpu.VMEM((e, dzb.shape[1]), BF16), pltpu.VMEM((e, 1024), F32),
                        pltpu.VMEM((e, CONV_W), F32),
                        pltpu.VMEM((e, CONV_W), F32), pltpu.VMEM((e, CONV_W), F32), pltpu.VMEM((e, CONV_W), F32),
                        pltpu.VMEM((e, CONV_W), F32), pltpu.VMEM((e, POOL_W), F32),
                        pltpu.VMEM((e - 8, CONV_W), F32),
                        pltpu.VMEM((8, e - 8, CONV_W), F32), pltpu.VMEM((tm, CONV_W), F32)],
        compiler_params=_cparams(),
    )(proj, proj, proj, dzb, dzb, dzb, a1, a1, a1, yb, ycat,
      w_out_t, pw_bd, pw_bd_t, pool_scale, sconv_w, cconv_w, cnorm_g, cnorm_b)


def _mesh_pos():
    return lax.axis_index("x"), lax.axis_index("y"), lax.axis_index("c")


def _flip(v, f):
    return 1 - v if f else v


class _Comm:
    def __init__(self, kind, arrs, lands=None, layer=0):
        self.kind = kind
        self.arrs = list(arrs)
        self.n = len(self.arrs)
        self.lands = None if lands is None else list(lands)
        self.layer = layer
        if kind == "gather":
            self.flips = [(1, 0, 0), (0, 1, 0), (1, 1, 0)]
            self.out_shape = [jax.ShapeDtypeStruct((N_CHIPS,) + a.shape, a.dtype) for a in self.arrs]
        else:
            self.flips = [(fx, fy, fc) for fx in (0, 1) for fy in (0, 1) for fc in (0, 1)][1:]
            self.out_shape = [jax.ShapeDtypeStruct(b.shape, b.dtype) for b in self.lands]
        self.n_peer = len(self.flips)

    def operands(self):
        return self.arrs + (self.lands or [])

    def scratch(self):
        return [pltpu.SemaphoreType.DMA((self.n * self.n_peer,)), pltpu.SemaphoreType.DMA((self.n * self.n_peer,)),
                pltpu.SemaphoreType.DMA((self.n,))]

    def aliases(self, in_off, out_off):
        if self.lands is None:
            return {}
        return {in_off + self.n + j: out_off + j for j in range(self.n)}

    def _copies(self, ins, outs, sems):
        send_sems, recv_sems, loc_sems = sems
        x, y, c = _mesh_pos()
        local, sends, recvs = [], [], []
        for a in range(self.n):
            def src(px, py):
                return ins[a] if self.kind == "gather" else ins[a].at[2 * px + py]

            def dst(px, py, pc):
                if self.kind == "gather":
                    return outs[a].at[2 * px + py]
                return outs[a].at[4 * px + 2 * py + pc, self.layer]

            local.append(pltpu.make_async_copy(src(x, y), dst(x, y, c), loc_sems.at[a]))
            for k, (fx, fy, fc) in enumerate(self.flips):
                px, py, pc = _flip(x, fx), _flip(y, fy), _flip(c, fc)
                sem = a * self.n_peer + k
                sends.append(pltpu.make_async_remote_copy(
                    src_ref=src(px, py), dst_ref=dst(x, y, c),
                    send_sem=send_sems.at[sem], recv_sem=recv_sems.at[sem],
                    device_id=(px, py, pc), device_id_type=MESH))
                recvs.append(pltpu.make_async_remote_copy(
                    src_ref=src(px, py), dst_ref=dst(px, py, pc),
                    send_sem=send_sems.at[sem], recv_sem=recv_sems.at[sem],
                    device_id=(px, py, pc), device_id_type=MESH))
        return local, sends, recvs

    def start(self, ins, outs, sems):
        local, sends, _ = self._copies(ins, outs, sems)
        for cp in local + sends:
            cp.start()

    def wait(self, ins, outs, sems):
        local, _, recvs = self._copies(ins, outs, sems)
        for cp in local:
            cp.wait()
        for cp in recvs:
            cp.wait()


def _split_refs(refs, n_in, n_out, n_scr, comm):
    c_in = len(comm.operands()) if comm is not None else 0
    c_out = comm.n if comm is not None else 0
    cuts = [n_in, c_in, n_out, c_out, n_scr]
    out, pos = [], 0
    for m in cuts:
        out.append(refs[pos:pos + m])
        pos += m
    out.append(refs[pos:])
    return out


def _call_with_comm(body, name, grid, in_specs, out_specs, out_shape, scratch, args, comm):
    hbm = pl.BlockSpec(memory_space=pl.ANY)
    aliases = {}
    if comm is not None:
        aliases = comm.aliases(len(in_specs), len(out_specs))
        in_specs = in_specs + [hbm] * len(comm.operands())
        out_specs = out_specs + [hbm] * comm.n
        out_shape = out_shape + comm.out_shape
        scratch = scratch + comm.scratch()
        args = args + comm.operands()
        name = name + "_" + comm.kind
    res = pl.pallas_call(
        body, name=name, grid=grid, in_specs=in_specs, out_specs=out_specs, out_shape=out_shape,
        scratch_shapes=scratch, input_output_aliases=aliases, compiler_params=_cparams(),
    )(*args)
    if comm is None:
        return res, None
    return res[:len(res) - comm.n], res[len(res) - comm.n:]


def run_comm(comm):
    def body(*refs):
        _, c_in, _, c_out, _, c_sem = _split_refs(refs, 0, 0, 0, comm)
        comm.start(c_in, c_out, c_sem)
        comm.wait(c_in, c_out, c_sem)

    hbm = pl.BlockSpec(memory_space=pl.ANY)
    return pl.pallas_call(
        body, name="comm_" + comm.kind,
        in_specs=[hbm] * len(comm.operands()), out_specs=[hbm] * comm.n, out_shape=comm.out_shape,
        scratch_shapes=comm.scratch(), input_output_aliases=comm.aliases(0, 0),
    )(*comm.operands())


def exchange(arrs, per_chip):
    n = len(arrs)
    flips = [(fx, fy, fc) for fx in (0, 1) for fy in (0, 1) for fc in (0, 1)][1:]

    def body(*refs):
        ins, outs = refs[:n], refs[n:2 * n]
        send_sems, recv_sems, loc_sems = refs[2 * n:]
        x, y, c = _mesh_pos()
        me = 4 * x + 2 * y + c
        local = []
        remote = []
        for a in range(n):
            def part(px, py):
                return ins[a].at[2 * px + py] if per_chip else ins[a]

            cp = pltpu.make_async_copy(part(x, y), outs[a].at[me], loc_sems.at[a])
            cp.start()
            local.append(cp)
            for k, (fx, fy, fc) in enumerate(flips):
                px, py, pc = _flip(x, fx), _flip(y, fy), _flip(c, fc)
                sem = a * 7 + k
                rc = pltpu.make_async_remote_copy(
                    src_ref=part(px, py), dst_ref=outs[a].at[me],
                    send_sem=send_sems.at[sem], recv_sem=recv_sems.at[sem],
                    device_id=(px, py, pc), device_id_type=MESH)
                rc.start()
                remote.append(pltpu.make_async_remote_copy(
                    src_ref=part(px, py), dst_ref=outs[a].at[4 * px + 2 * py + pc],
                    send_sem=send_sems.at[sem], recv_sem=recv_sems.at[sem],
                    device_id=(px, py, pc), device_id_type=MESH))
        for cp in local:
            cp.wait()
        for rc in remote:
            rc.wait()

    hbm = pl.BlockSpec(memory_space=pl.ANY)
    shapes = [a.shape[1:] if per_chip else a.shape for a in arrs]
    return pl.pallas_call(
        body, name="exchange_per_chip" if per_chip else "exchange_all",
        in_specs=[hbm] * n, out_specs=[hbm] * n,
        out_shape=[jax.ShapeDtypeStruct((N_DEV,) + s, a.dtype) for s, a in zip(shapes, arrs)],
        scratch_shapes=[pltpu.SemaphoreType.DMA((7 * n,)), pltpu.SemaphoreType.DMA((7 * n,)),
                        pltpu.SemaphoreType.DMA((n,))],
    )(*arrs)


def adamw(parts, w, m, v, name):
    k_n, n_l, r, c = parts.shape
    tr = r
    for cand in (512, 256, 128, 64, 32, 16, 8):
        if r % cand == 0:
            tr = cand
            break

    def body(p_ref, w_ref, m_ref, v_ref, g_ref, d_ref, mo_ref, vo_ref):
        g = p_ref[0, 0].astype(F32)
        for k in range(1, k_n):
            g = g + p_ref[k, 0].astype(F32)
        m_new = ADAM_B1 * m_ref[0] + (1.0 - ADAM_B1) * g
        v_new = ADAM_B2 * v_ref[0] + (1.0 - ADAM_B2) * (g * g)
        m_hat = m_new / (1.0 - ADAM_B1 ** ADAM_STEP)
        v_hat = v_new / (1.0 - ADAM_B2 ** ADAM_STEP)
        g_ref[0] = g
        d_ref[0] = -ADAM_LR * (m_hat / (jnp.sqrt(v_hat) + ADAM_EPS) + ADAM_WD * w_ref[0])
        mo_ref[0] = m_new
        vo_ref[0] = v_new

    blk = pl.BlockSpec((1, tr, c), lambda l, i: (l, i, 0))
    return pl.pallas_call(
        body, name=name,
        grid=(n_l, r // tr),
        in_specs=[pl.BlockSpec((k_n, 1, tr, c), lambda l, i: (0, l, i, 0)), blk, blk, blk],
        out_specs=[blk, blk, blk, blk],
        out_shape=[jax.ShapeDtypeStruct((n_l, r, c), F32)] * 4,
        compiler_params=_cparams(),
    )(parts, w, m, v)


def _block_diag(pool_w):
    out = jnp.zeros((POOL_W, POOL_W), pool_w.dtype)
    for g in range(4):
        out = lax.dynamic_update_slice(out, pool_w[g], (64 * g, 64 * g))
    return out


def _pad_rows(a, rows):
    return jnp.pad(a, ((0, rows - a.shape[0]), (0, 0)))


def kernel(x, ln1_g, ln1_b, ffn1_w_gate, ffn1_w_up, ffn1_w_down, mix_w_in, pool_w, pool_scale, sconv_w, cconv_w, cconv_b, cnorm_g, cnorm_b, mix_w_out, ln2_g, ln2_b, ffn2_w_gate, ffn2_w_up, ffn2_w_down, ln3_g, ln3_b, loss_target, m_ln1_g, m_ln1_b, m_ffn1_w_gate, m_ffn1_w_up, m_ffn1_w_down, m_mix_w_in, m_pool_w, m_pool_scale, m_sconv_w, m_cconv_w, m_cconv_b, m_cnorm_g, m_cnorm_b, m_mix_w_out, m_ln2_g, m_ln2_b, m_ffn2_w_gate, m_ffn2_w_up, m_ffn2_w_down, m_ln3_g, m_ln3_b, v_ln1_g, v_ln1_b, v_ffn1_w_gate, v_ffn1_w_up, v_ffn1_w_down, v_mix_w_in, v_pool_w, v_pool_scale, v_sconv_w, v_cconv_w, v_cconv_b, v_cnorm_g, v_cnorm_b, v_mix_w_out, v_ln2_g, v_ln2_b, v_ffn2_w_gate, v_ffn2_w_up, v_ffn2_w_down, v_ln3_g, v_ln3_b):
    names = ['ln1_g', 'ln1_b', 'ffn1_w_gate', 'ffn1_w_up', 'ffn1_w_down', 'mix_w_in', 'pool_w', 'pool_scale',
             'sconv_w', 'cconv_w', 'cconv_b', 'cnorm_g', 'cnorm_b', 'mix_w_out', 'ln2_g', 'ln2_b',
             'ffn2_w_gate', 'ffn2_w_up', 'ffn2_w_down', 'ln3_g', 'ln3_b']
    w = dict(zip(names, (ln1_g, ln1_b, ffn1_w_gate, ffn1_w_up, ffn1_w_down, mix_w_in, pool_w, pool_scale, sconv_w,
                         cconv_w, cconv_b, cnorm_g, cnorm_b, mix_w_out, ln2_g, ln2_b, ffn2_w_gate, ffn2_w_up,
                         ffn2_w_down, ln3_g, ln3_b)))
    mom_m = dict(zip(names, (m_ln1_g, m_ln1_b, m_ffn1_w_gate, m_ffn1_w_up, m_ffn1_w_down, m_mix_w_in, m_pool_w,
                             m_pool_scale, m_sconv_w, m_cconv_w, m_cconv_b, m_cnorm_g, m_cnorm_b, m_mix_w_out,
                             m_ln2_g, m_ln2_b, m_ffn2_w_gate, m_ffn2_w_up, m_ffn2_w_down, m_ln3_g, m_ln3_b)))
    mom_v = dict(zip(names, (v_ln1_g, v_ln1_b, v_ffn1_w_gate, v_ffn1_w_up, v_ffn1_w_down, v_mix_w_in, v_pool_w,
                             v_pool_scale, v_sconv_w, v_cconv_w, v_cconv_b, v_cnorm_g, v_cnorm_b, v_mix_w_out,
                             v_ln2_g, v_ln2_b, v_ffn2_w_gate, v_ffn2_w_up, v_ffn2_w_down, v_ln3_g, v_ln3_b)))
    big = ['ffn1_w_gate', 'ffn1_w_up', 'ffn1_w_down', 'mix_w_in', 'mix_w_out',
           'ffn2_w_gate', 'ffn2_w_up', 'ffn2_w_down']
    n_l = ln1_g.shape[0]
    d = x.shape[-1]
    fs = ffn1_w_gate.shape[-1]
    ws_in = mix_w_in.shape[-1]
    cs = sconv_w.shape[-1]
    chip = 2 * lax.axis_index("x") + lax.axis_index("y")

    keys_a = ['ffn2_w_gate', 'ffn2_w_up', 'ffn2_w_down', 'mix_w_in', 'mix_w_out']
    keys_b = ['ffn1_w_gate', 'ffn1_w_up', 'ffn1_w_down']

    def shards_b(l):
        return [w[k][l].astype(BF16) for k in keys_b]

    def shards_a(l):
        conv_loc = jnp.concatenate([sconv_w[l], cconv_w[l]], axis=0)
        return [w[k][l].astype(BF16) for k in keys_a] + [conv_loc]

    def cols(a):
        return jnp.transpose(a, (1, 0, 2)).reshape(a.shape[1], -1)

    def layer_weights(l, got_b, got_a):
        gw = dict(zip(keys_b + keys_a, list(got_b) + list(got_a[:-1])))
        conv_all = jnp.transpose(got_a[-1], (1, 0, 2)).reshape(SCONV_K + CCONV_K, N_CHIPS * cs)
        return dict(
            g1=gw['ffn1_w_gate'], u1=gw['ffn1_w_up'], d1=gw['ffn1_w_down'],
            g2=gw['ffn2_w_gate'], u2=gw['ffn2_w_up'], d2=gw['ffn2_w_down'],
            w_in=cols(gw['mix_w_in']),
            w_out=gw['mix_w_out'].reshape(-1, d),
            pw=_block_diag(pool_w[l]).astype(BF16),
            ps=pool_scale[l][None], sw=_pad_rows(conv_all[:SCONV_K], 8), cw=_pad_rows(conv_all[SCONV_K:], 32),
            cb=cconv_b[l][None], cg=cnorm_g[l][None], cbb=cnorm_b[l][None],
        )

    h = x[0]
    hb = h.astype(BF16)
    target = loss_target[0]
    saved = []
    layer_w = []
    got_b = run_comm(_Comm("gather", shards_b(0)))
    for l in range(n_l):
        (y1, y1b, z1, gs1, us1), got_a = ffn_fwd(h, hb, got_b[0], got_b[1], got_b[2], ln1_g[l][None],
                                                 ln1_b[l][None], _Comm("gather", shards_a(l)))
        lw = layer_weights(l, got_b, got_a)
        layer_w.append(lw)
        proj, ycat, a1, y2, y2b, z2 = mix_fwd(y1, y1b, lw['w_in'], lw['w_out'], ln2_g[l][None], ln2_b[l][None],
                                              lw['pw'], lw['ps'], lw['sw'], lw['cw'], lw['cb'], lw['cg'], lw['cbb'])
        comm = _Comm("gather", shards_b(l + 1)) if l + 1 < n_l else None
        (y3, y3b, z3, gs2, us2), got_b = ffn_fwd(y2, y2b, lw['g2'], lw['u2'], lw['d2'], ln3_g[l][None],
                                                 ln3_b[l][None], comm)
        saved.append(dict(x0b=hb, z1=z1, y1b=y1b, proj=proj, ycat=ycat, a1=a1, z2=z2, y2b=y2b, z3=z3,
                          gs1=gs1, us1=us1, gs2=gs2, us2=us2))
        h, hb = y3, y3b

    dy, loss_blk = loss_and_grad(h, target)
    loss = lax.psum(loss_blk[0, 0], ("x", "y", "c"))

    g_loc = {k: [None] * n_l for k in names if k not in big}
    lands = {k: lax.empty((N_DEV, n_l) + w[k].shape[1:], BF16) for k in big}

    def scatter(keys, arrs, layer):
        return _Comm("scatter", arrs, [lands[k] for k in keys], layer)

    pend_b = None
    parts = None
    for l in reversed(range(n_l)):
        lw, sv = layer_w[l], saved[l]
        dzb, dxr, g_loc['ln3_g'][l], g_loc['ln3_b'][l] = ln_bwd(dy, parts, sv['z3'], ln3_g[l][None])
        comm = None if pend_b is None else scatter(keys_b, pend_b, l + 1)
        (parts, dg2, du2, dd2), landed = ffn_bwd(sv['y2b'], dzb, sv['gs2'], sv['us2'], lw['g2'], lw['u2'],
                                                         lw['d2'], comm)
        if comm is not None:
            lands.update(zip(keys_b, landed))
        dzb, dxr, g_loc['ln2_g'][l], g_loc['ln2_b'][l] = ln_bwd(dxr, parts, sv['z2'], ln2_g[l][None])
        (dproj, dw_in, dw_out, dpw, g_loc['pool_scale'][l], dsw, dcw, g_loc['cconv_b'][l], g_loc['cnorm_g'][l],
         g_loc['cnorm_b'][l]) = mix_bwd(sv['proj'], dzb, lw['w_out'].T, sv['a1'], sv['y1b'], sv['ycat'],
                                        lw['pw'], lw['pw'].T, lw['ps'], lw['sw'], lw['cw'], lw['cg'], lw['cbb'])
        g_loc['pool_w'][l] = jnp.stack([dpw[64 * g:64 * g + 64, 64 * g:64 * g + 64] for g in range(4)])
        g_loc['sconv_w'][l] = dsw[:SCONV_K]
        g_loc['cconv_w'][l] = dcw[:CCONV_K]
        dw_in_c = jnp.transpose(dw_in.reshape(d, N_CHIPS, ws_in), (1, 0, 2)).astype(BF16)
        dw_out_c = dw_out.reshape(N_CHIPS, -1, d).astype(BF16)
        dzb, dxr, g_loc['ln1_g'][l], g_loc['ln1_b'][l] = ln_bwd(dxr, None, sv['z1'], ln1_g[l][None],
                                                                 mm=(dproj, lw['w_in'].T))
        comm = scatter(keys_a, [dg2, du2, dd2, dw_in_c, dw_out_c], l)
        (parts, dg1, du1, dd1), landed = ffn_bwd(sv['x0b'], dzb, sv['gs1'], sv['us1'], lw['g1'], lw['u1'],
                                                         lw['d1'], comm)
        lands.update(zip(keys_a, landed))
        pend_b = [dg1, du1, dd1]
        dy = dxr
    grad_x = add_parts(dy, parts)[None]
    lands.update(zip(keys_b, run_comm(scatter(keys_b, pend_b, 0))))
    parts_big = lands

    small = [k for k in names if k not in big]
    small_full = {}
    for k in small:
        a = jnp.stack(g_loc[k])
        small_full[k] = a.reshape(n_l, -1) if a.shape[1] == 1 else a
    flat = jnp.concatenate([small_full[k].reshape(-1) for k in small])
    n_flat = flat.shape[0]
    rows = -(-n_flat // (SMALL_ROWS * 128)) * SMALL_ROWS
    flat = jnp.pad(flat, (0, rows * 128 - n_flat)).reshape(rows, 128)
    parts_small = exchange([flat], per_chip=False)[0]

    out_g, out_d, out_m, out_v = {}, {}, {}, {}
    for k in big:
        out_g[k], out_d[k], out_m[k], out_v[k] = adamw(parts_big[k], w[k], mom_m[k], mom_v[k], name="adamw_" + k)

    zeros = jnp.zeros((1, rows, 128), F32)
    g_sum = adamw(parts_small[:, None], zeros, zeros, zeros, name="sum_small")[0].reshape(-1)
    off = 0
    for k in small:
        full = small_full[k]
        g = g_sum[off:off + full.size].reshape(full.shape)
        off += full.size
        if k in ('sconv_w', 'cconv_w'):
            g = lax.dynamic_slice_in_dim(g, chip * cs, cs, axis=2)
        out_g[k] = g.reshape(w[k].shape)

    def pack(dct):
        f = jnp.concatenate([dct[k].reshape(-1) for k in small])
        r2 = -(-f.shape[0] // (SMALL_ROWS * 128)) * SMALL_ROWS
        return jnp.pad(f, (0, r2 * 128 - f.shape[0])).reshape(r2, 128), f.shape[0]

    gp, n_small = pack(out_g)
    wp, _ = pack(w)
    mp, _ = pack(mom_m)
    vp, _ = pack(mom_v)
    _, dp, mo, vo = adamw(gp[None, None], wp[None], mp[None], vp[None], name="adamw_small")
    off = 0
    for k in small:
        sz = w[k].size
        out_d[k] = dp.reshape(-1)[off:off + sz].reshape(w[k].shape)
        out_m[k] = mo.reshape(-1)[off:off + sz].reshape(w[k].shape)
        out_v[k] = vo.reshape(-1)[off:off + sz].reshape(w[k].shape)
        off += sz

    return (loss, grad_x, *[out_g[k] for k in names], *[out_d[k] for k in names],
            *[out_m[k] for k in names], *[out_v[k] for k in names])
```

```python
import jax
import jax.numpy as jnp
from jax import lax
from jax.experimental import pallas as pl
from jax.experimental.pallas import tpu as pltpu

F32 = jnp.float32
BF16 = jnp.bfloat16

DEPTH = 4
ALPHA = (2.0 * DEPTH) ** 0.25
LN_EPS = 1e-5
POOL_W = 256
CONV_W = 384
SCONV_K = 3
CCONV_K = 31
C_POOL = (0, 256)
C_GB = (256, 640)
C_GC = (640, 1024)
C_V = (1024, 1408)
C_CV = (1408, 1792)
C_CG = (1792, 2176)

ADAM_LR = 0.001
ADAM_B1 = 0.9
ADAM_B2 = 0.999
ADAM_EPS = 1e-08
ADAM_WD = 0.01
ADAM_STEP = 10

N_CHIPS = 4
N_DEV = 8
MESH = pl.DeviceIdType.MESH

TM_FWD = 512
TM_BWD = 512
TM_MIX = 256
HALO = 32
SMALL_ROWS = 256
VMEM_LIMIT = 56 * 1024 * 1024


def _cparams():
    return pltpu.CompilerParams(vmem_limit_bytes=VMEM_LIMIT)


def _sigmoid(v):
    return 1.0 / (1.0 + jnp.exp(-v))


def _dot(a, b):
    return jnp.dot(a, b, preferred_element_type=F32)


def _dot_nt(a, b):
    return lax.dot_general(a, b, (((1,), (1,)), ((), ())), preferred_element_type=F32)


def _dot_tn(a, b):
    return lax.dot_general(a, b, (((0,), (0,)), ((), ())), preferred_element_type=F32)


def _ln_fwd(z, g, b):
    mu = jnp.mean(z, axis=-1, keepdims=True)
    zc = z - mu
    var = jnp.mean(zc * zc, axis=-1, keepdims=True)
    return zc * lax.rsqrt(var + LN_EPS) * g + b


def _ln_bwd(dy, z, g):
    mu = jnp.mean(z, axis=-1, keepdims=True)
    zc = z - mu
    var = jnp.mean(zc * zc, axis=-1, keepdims=True)
    rstd = lax.rsqrt(var + LN_EPS)
    xhat = zc * rstd
    dxh = dy * g
    m1 = jnp.mean(dxh, axis=-1, keepdims=True)
    m2 = jnp.mean(dxh * xhat, axis=-1, keepdims=True)
    return rstd * (dxh - m1 - xhat * m2), xhat


def _tile(t, tm):
    tm = min(tm, t)
    assert t % tm == 0, (t, tm)
    return tm


def ffn_fwd(x, xb, wg, wu, wd, ln_g, ln_b, comm=None):
    t, d = x.shape
    s_n, _, fs = wg.shape
    tm = _tile(t, TM_FWD)
    n_i = t // tm

    def body(*refs):
        ((x_ref, xb_ref, wg_ref, wu_ref, wd_ref, g_ref, b_ref), c_in, (y_ref, yb_ref, z_ref, gs_ref, us_ref), c_out,
         (acc_ref,), c_sem) = _split_refs(refs, 7, 5, 1, comm)
        i = pl.program_id(0)
        s = pl.program_id(1)
        if comm is not None:
            @pl.when(jnp.logical_and(i == 0, s == 0))
            def _():
                comm.start(c_in, c_out, c_sem)

        xb = xb_ref[...]
        g = _dot(xb, wg_ref[0])
        u = _dot(xb, wu_ref[0])
        gs_ref[0] = g.astype(BF16)
        us_ref[0] = u.astype(BF16)
        a = (g * _sigmoid(g) * u).astype(BF16)
        part = _dot(a, wd_ref[0])

        @pl.when(s == 0)
        def _():
            acc_ref[...] = part

        @pl.when(s > 0)
        def _():
            acc_ref[...] += part

        @pl.when(s == s_n - 1)
        def _():
            z = ALPHA * x_ref[...] + 0.5 * acc_ref[...]
            z_ref[...] = z
            y = _ln_fwd(z, g_ref[...], b_ref[...])
            y_ref[...] = y
            yb_ref[...] = y.astype(BF16)

        if comm is not None:
            @pl.when(jnp.logical_and(i == n_i - 1, s == s_n - 1))
            def _():
                comm.wait(c_in, c_out, c_sem)

    tok = lambda i, s: (i, 0)
    one = lambda i, s: (0, 0)
    return _call_with_comm(
        body, "ffn_fwd", (n_i, s_n),
        [pl.BlockSpec((tm, d), tok), pl.BlockSpec((tm, d), tok),
         pl.BlockSpec((1, d, fs), lambda i, s: (s, 0, 0)),
         pl.BlockSpec((1, d, fs), lambda i, s: (s, 0, 0)),
         pl.BlockSpec((1, fs, d), lambda i, s: (s, 0, 0)),
         pl.BlockSpec((1, d), one), pl.BlockSpec((1, d), one)],
        [pl.BlockSpec((tm, d), tok), pl.BlockSpec((tm, d), tok), pl.BlockSpec((tm, d), tok),
         pl.BlockSpec((1, tm, fs), lambda i, s: (s, i, 0)), pl.BlockSpec((1, tm, fs), lambda i, s: (s, i, 0))],
        [jax.ShapeDtypeStruct((t, d), F32), jax.ShapeDtypeStruct((t, d), BF16), jax.ShapeDtypeStruct((t, d), F32),
         jax.ShapeDtypeStruct((s_n, t, fs), BF16), jax.ShapeDtypeStruct((s_n, t, fs), BF16)],
        [pltpu.VMEM((tm, d), F32)],
        [x, xb, wg, wu, wd, ln_g, ln_b], comm)


def ffn_bwd(xb, dzb, gs, us, wg, wu, wd, comm=None):
    t, d = xb.shape
    s_n, _, fs = wg.shape
    tm = _tile(t, TM_BWD)
    n_i = t // tm

    def body(*refs):
        ((x_ref, dzb_ref, gs_ref, us_ref, wg_ref, wu_ref, wd_ref), c_in, (dx_ref, dwg_ref, dwu_ref, dwd_ref), c_out,
         (accg, accu, accd), c_sem) = _split_refs(refs, 7, 4, 3, comm)
        s = pl.program_id(0)
        i = pl.program_id(1)
        if comm is not None:
            @pl.when(jnp.logical_and(i == 0, s == 0))
            def _():
                comm.start(c_in, c_out, c_sem)

        @pl.when(i == 0)
        def _():
            accg[...] = jnp.zeros_like(accg)
            accu[...] = jnp.zeros_like(accu)
            accd[...] = jnp.zeros_like(accd)

        x_v = x_ref[...]
        g = gs_ref[0].astype(F32)
        u = us_ref[0].astype(F32)
        sg = _sigmoid(g)
        si = g * sg
        a = (si * u).astype(BF16)
        dfb = dzb_ref[...] * 0.5
        da = _dot_nt(dfb, wd_ref[0])
        dgate = (da * u * (sg * (1.0 + g * (1.0 - sg)))).astype(BF16)
        dup = (da * si).astype(BF16)
        dx_ref[0] = (_dot_nt(dgate, wg_ref[0]) + _dot_nt(dup, wu_ref[0])).astype(BF16)
        accg[...] += _dot_tn(dgate, x_v)
        accu[...] += _dot_tn(dup, x_v)
        accd[...] += _dot_tn(a, dfb)

        @pl.when(i == n_i - 1)
        def _():
            dwg_ref[0] = accg[...].astype(BF16)
            dwu_ref[0] = accu[...].astype(BF16)
            dwd_ref[0] = accd[...].astype(BF16)

        if comm is not None:
            @pl.when(jnp.logical_and(i == n_i - 1, s == s_n - 1))
            def _():
                comm.wait(c_in, c_out, c_sem)

    tok = lambda s, i: (i, 0)
    shard = lambda s, i: (s, 0, 0)
    return _call_with_comm(
        body, "ffn_bwd", (s_n, n_i),
        [pl.BlockSpec((tm, d), tok), pl.BlockSpec((tm, d), tok),
         pl.BlockSpec((1, tm, fs), lambda s, i: (s, i, 0)), pl.BlockSpec((1, tm, fs), lambda s, i: (s, i, 0)),
         pl.BlockSpec((1, d, fs), shard), pl.BlockSpec((1, d, fs), shard), pl.BlockSpec((1, fs, d), shard)],
        [pl.BlockSpec((1, tm, d), lambda s, i: (s, i, 0)),
         pl.BlockSpec((1, fs, d), shard), pl.BlockSpec((1, fs, d), shard), pl.BlockSpec((1, fs, d), shard)],
        [jax.ShapeDtypeStruct((s_n, t, d), BF16),
         jax.ShapeDtypeStruct((s_n, fs, d), BF16), jax.ShapeDtypeStruct((s_n, fs, d), BF16),
         jax.ShapeDtypeStruct((s_n, fs, d), BF16)],
        [pltpu.VMEM((fs, d), F32), pltpu.VMEM((fs, d), F32), pltpu.VMEM((fs, d), F32)],
        [xb, dzb, gs, us, wg, wu, wd], comm)


def _sum_parts(base_ref, parts_ref):
    v = base_ref[...]
    if parts_ref is not None:
        for p in range(parts_ref.shape[0]):
            v = v + parts_ref[p].astype(F32)
    return v


def ln_bwd(dy, parts, z, ln_g, mm=None):
    t, d = dy.shape
    tm = _tile(t, TM_FWD)

    def body(*refs):
        refs = list(refs)
        dy_ref = refs.pop(0)
        parts_ref = refs.pop(0) if parts is not None else None
        a_ref, w_ref = (refs.pop(0), refs.pop(0)) if mm is not None else (None, None)
        z_ref, g_ref, dzb_ref, dxr_ref, dg_ref, db_ref = refs
        i = pl.program_id(0)
        dy_v = _sum_parts(dy_ref, parts_ref)
        if mm is not None:
            dy_v = dy_v + _dot(a_ref[...], w_ref[...])
        dz, xhat = _ln_bwd(dy_v, z_ref[...], g_ref[...])
        dzb_ref[...] = dz.astype(BF16)
        dxr_ref[...] = ALPHA * dz

        @pl.when(i == 0)
        def _():
            dg_ref[...] = jnp.zeros_like(dg_ref)
            db_ref[...] = jnp.zeros_like(db_ref)

        dg_ref[...] += jnp.sum(dy_v * xhat, axis=0, keepdims=True)
        db_ref[...] += jnp.sum(dy_v, axis=0, keepdims=True)

    tok = lambda i: (i, 0)
    one = lambda i: (0, 0)
    in_specs = [pl.BlockSpec((tm, d), tok)]
    args = [dy]
    name = "ln_bwd"
    if parts is not None:
        in_specs.append(pl.BlockSpec((parts.shape[0], tm, d), lambda i: (0, i, 0)))
        args.append(parts)
        name += "_parts"
    if mm is not None:
        in_specs += [pl.BlockSpec((tm, mm[0].shape[1]), tok), pl.BlockSpec(mm[1].shape, one)]
        args += list(mm)
        name += "_mm"
    return pl.pallas_call(
        body, name=name,
        grid=(t // tm,),
        in_specs=in_specs + [pl.BlockSpec((tm, d), tok), pl.BlockSpec((1, d), one)],
        out_specs=[pl.BlockSpec((tm, d), tok), pl.BlockSpec((tm, d), tok),
                   pl.BlockSpec((1, d), one), pl.BlockSpec((1, d), one)],
        out_shape=[jax.ShapeDtypeStruct((t, d), BF16), jax.ShapeDtypeStruct((t, d), F32),
                   jax.ShapeDtypeStruct((1, d), F32), jax.ShapeDtypeStruct((1, d), F32)],
        compiler_params=_cparams(),
    )(*args, z, ln_g)


def add_parts(base, parts):
    t, d = base.shape
    tm = _tile(t, TM_FWD)

    def body(b_ref, p_ref, o_ref):
        o_ref[...] = _sum_parts(b_ref, p_ref)

    tok = lambda i: (i, 0)
    return pl.pallas_call(
        body, name="add_parts",
        grid=(t // tm,),
        in_specs=[pl.BlockSpec((tm, d), tok), pl.BlockSpec((parts.shape[0], tm, d), lambda i: (0, i, 0))],
        out_specs=pl.BlockSpec((tm, d), tok),
        out_shape=jax.ShapeDtypeStruct((t, d), F32),
        compiler_params=_cparams(),
    )(base, parts)


def loss_and_grad(y, target):
    t, d = y.shape
    tm = _tile(t, TM_FWD)

    def body(y_ref, t_ref, dy_ref, l_ref):
        i = pl.program_id(0)
        e = y_ref[...] - t_ref[...]
        dy_ref[...] = e * (1.0 / d)

        @pl.when(i == 0)
        def _():
            l_ref[...] = jnp.zeros_like(l_ref)

        l_ref[...] += (0.5 / d) * jnp.sum(e * e)

    tok = lambda i: (i, 0)
    return pl.pallas_call(
        body, name="loss",
        grid=(t // tm,),
        in_specs=[pl.BlockSpec((tm, d), tok), pl.BlockSpec((tm, d), tok)],
        out_specs=[pl.BlockSpec((tm, d), tok), pl.BlockSpec((8, 128), lambda i: (0, 0))],
        out_shape=[jax.ShapeDtypeStruct((t, d), F32), jax.ShapeDtypeStruct((8, 128), F32)],
        compiler_params=_cparams(),
    )(y, target)


def _halo_specs(tm, cols, n_rows):
    r = tm // HALO
    last = n_rows // HALO - 1
    return [pl.BlockSpec((HALO, cols), lambda i: (jnp.maximum(i * r - 1, 0), 0)),
            pl.BlockSpec((tm, cols), lambda i: (i, 0)),
            pl.BlockSpec((HALO, cols), lambda i: (jnp.minimum((i + 1) * r, last), 0))]


def _fill_ext(dst, prev_ref, main_ref, next_ref, i, n_i):
    tm = main_ref.shape[0]
    dst[0:HALO, :] = jnp.where(i > 0, prev_ref[...], 0.0)
    dst[HALO:HALO + tm, :] = main_ref[...]
    dst[HALO + tm:HALO + tm + HALO, :] = jnp.where(i < n_i - 1, next_ref[...], 0.0)


def _pool_lane_half():
    lane = lax.broadcasted_iota(jnp.int32, (1, POOL_W), 1)
    return jnp.left_shift(1, lane // 64)


def _pool_inv_count(t0, rows, seq):
    half = _pool_lane_half()
    tpos = t0 + lax.broadcasted_iota(jnp.int32, (rows, 1), 0)
    lo = jnp.maximum(tpos - half, 0)
    hi = jnp.minimum(tpos + half, seq)
    cnt = jnp.maximum(hi - lo, 1)
    return 1.0 / cnt.astype(F32)


def _taps(ext_ref, cols, offsets, tm, tmp_ref):
    e = ext_ref.shape[0]
    width = len(range(*cols.indices(ext_ref.shape[1])))
    by_phase = {}
    for o in offsets:
        by_phase.setdefault((HALO + o) % 8, []).append(o)
    for r, group in by_phase.items():
        if r == 0:
            for o in group:
                yield o, ext_ref[HALO + o:HALO + o + tm, cols]
            continue
        tmp_ref[:, 0:width] = ext_ref[r:r + e - 8, cols]
        for o in group:
            start = HALO + o - r
            yield o, tmp_ref[start:start + tm, 0:width]


def _pool_forward(p_ext, tm, t0, seq, tmp_ref):
    half = _pool_lane_half()
    total = jnp.zeros((tm, POOL_W), F32)
    for o, win in _taps(p_ext, slice(C_POOL[0], C_POOL[1]), range(-8, 8), tm, tmp_ref):
        m = ((o >= -half) & (o < half)).astype(F32)
        total = total + m * win
    u = p_ext[HALO:HALO + tm, C_POOL[0]:C_POOL[1]]
    return total * _pool_inv_count(t0, tm, seq) - u


CONV_ROWS = 64


def _fill_phases(ph_ref, ext_ref):
    e = ext_ref.shape[0]
    for r in range(8):
        ph_ref[r] = ext_ref[r:r + e - 8, :]


def _phase_win(ph_ref, o, row0, rows):
    r = (HALO + o) % 8
    start = HALO + o - r + row0
    return ph_ref[r, start:start + rows, :]


def mix_fwd(y, yb, w_in, w_out, ln_g, ln_b, pw_bd, pool_scale, sconv_w, cconv_w, cconv_b, cnorm_g, cnorm_b):
    t, d = yb.shape
    pc = w_in.shape[1]
    tm = _tile(t, TM_MIX)
    n_i = t // tm
    e = tm + 2 * HALO

    def body(yp_ref, ym_ref, yn_ref, y_ref, win_ref, wout_ref, lg_ref, lb_ref,
             pw_ref, ps_ref, sw_ref, cw_ref, cb_ref, cg_ref, cbb_ref,
             proj_ref, yc_ref, a1_ref, y2_ref, y2b_ref, z2_ref, yb_ext, p_ext, q_ext, a0_ext, tmp, ph, a1_s):
        i = pl.program_id(0)
        _fill_ext(yb_ext, yp_ref, ym_ref, yn_ref, i, n_i)
        p_ext[...] = _dot(yb_ext[...], win_ref[...])
        proj_ref[...] = p_ext[HALO:HALO + tm, :]
        pooled = _pool_forward(p_ext, tm, i * tm, t, tmp)
        y_a = _dot(pooled.astype(BF16), pw_ref[...]) * ps_ref[...]
        yc_ref[:, 0:256] = y_a.astype(BF16)
        q_ext[...] = p_ext[:, C_GC[0]:C_GC[1]] * p_ext[:, C_V[0]:C_V[1]]
        conv = jnp.zeros((tm, CONV_W), F32)
        for k in range(SCONV_K):
            conv = conv + sw_ref[k:k + 1, :] * q_ext[HALO + k - 1:HALO + k - 1 + tm, :]
        y_b = p_ext[HALO:HALO + tm, C_GB[0]:C_GB[1]] * conv
        yc_ref[:, 256:640] = y_b.astype(BF16)
        a0_ext[...] = p_ext[:, C_CV[0]:C_CV[1]] * _sigmoid(p_ext[:, C_CG[0]:C_CG[1]])
        _fill_phases(ph, a0_ext)
        for c0 in range(0, tm, CONV_ROWS):
            acc = jnp.zeros((CONV_ROWS, CONV_W), F32) + cb_ref[...]
            for o in range(-15, 16):
                acc = acc + cw_ref[o + 15:o + 16, :] * _phase_win(ph, o, c0, CONV_ROWS)
            a1_s[c0:c0 + CONV_ROWS, :] = acc
        a1 = a1_s[...]
        a1_ref[...] = a1
        ln = _ln_fwd(a1, cg_ref[...], cbb_ref[...])
        yc_ref[:, 640:1024] = (ln * _sigmoid(ln)).astype(BF16)
        z = ALPHA * y_ref[...] + _dot(yc_ref[...], wout_ref[...])
        z2_ref[...] = z
        y2 = _ln_fwd(z, lg_ref[...], lb_ref[...])
        y2_ref[...] = y2
        y2b_ref[...] = y2.astype(BF16)

    one = lambda i: (0, 0)
    tok = lambda i: (i, 0)
    return pl.pallas_call(
        body, name="mix_fwd",
        grid=(n_i,),
        in_specs=_halo_specs(tm, d, t) + [
            pl.BlockSpec((tm, d), tok), pl.BlockSpec((d, pc), one), pl.BlockSpec(w_out.shape, one),
            pl.BlockSpec((1, d), one), pl.BlockSpec((1, d), one),
            pl.BlockSpec((POOL_W, POOL_W), one), pl.BlockSpec((1, POOL_W), one),
            pl.BlockSpec((8, CONV_W), one), pl.BlockSpec((32, CONV_W), one),
            pl.BlockSpec((1, CONV_W), one), pl.BlockSpec((1, CONV_W), one), pl.BlockSpec((1, CONV_W), one)],
        out_specs=[pl.BlockSpec((tm, pc), tok), pl.BlockSpec((tm, 1024), tok), pl.BlockSpec((tm, CONV_W), tok),
                   pl.BlockSpec((tm, d), tok), pl.BlockSpec((tm, d), tok), pl.BlockSpec((tm, d), tok)],
        out_shape=[jax.ShapeDtypeStruct((t, pc), F32), jax.ShapeDtypeStruct((t, 1024), BF16),
                   jax.ShapeDtypeStruct((t, CONV_W), F32),
                   jax.ShapeDtypeStruct((t, d), F32), jax.ShapeDtypeStruct((t, d), BF16),
                   jax.ShapeDtypeStruct((t, d), F32)],
        scratch_shapes=[pltpu.VMEM((e, d), BF16), pltpu.VMEM((e, pc), F32), pltpu.VMEM((e, CONV_W), F32),
                        pltpu.VMEM((e, CONV_W), F32), pltpu.VMEM((e - 8, CONV_W), F32),
                        pltpu.VMEM((8, e - 8, CONV_W), F32), pltpu.VMEM((tm, CONV_W), F32)],
        compiler_params=_cparams(),
    )(yb, yb, yb, y, w_in, w_out, ln_g, ln_b, pw_bd, pool_scale, sconv_w, cconv_w, cconv_b, cnorm_g, cnorm_b)


def mix_bwd(proj, dzb, w_out_t, a1, yb, ycat, pw_bd, pw_bd_t, pool_scale, sconv_w, cconv_w, cnorm_g, cnorm_b):
    t, pc = proj.shape
    d = yb.shape[1]
    tm = _tile(t, TM_MIX)
    n_i = t // tm
    e = tm + 2 * HALO

    def body(pp_ref, pm_ref, pn_ref, dp_ref, dm_ref, dn_ref, ap_ref, am_ref, an_ref, yb_ref, yc_ref,
             wot_ref, pw_ref, pwt_ref, ps_ref, sw_ref, cw_ref, cg_ref, cbb_ref,
             dproj_ref, dwin_ref, dwout_ref, dpw_ref, dps_ref, dsw_ref, dcw_ref, dcb_ref, dcg_ref, dcbb_ref,
             p_ext, dz_ext, dy_ext, a1_ext, a0_ext, da1_ext, q_ext, dc_ext, dpn_ext, tmp, ph, da0_s):
        i = pl.program_id(0)
        main = slice(HALO, HALO + tm)

        @pl.when(i == 0)
        def _():
            for r in (dwin_ref, dwout_ref, dpw_ref, dps_ref, dsw_ref, dcw_ref, dcb_ref, dcg_ref, dcbb_ref):
                r[...] = jnp.zeros_like(r)

        _fill_ext(p_ext, pp_ref, pm_ref, pn_ref, i, n_i)
        _fill_ext(dz_ext, dp_ref, dm_ref, dn_ref, i, n_i)
        dy_ext[...] = _dot(dz_ext[...], wot_ref[...])
        _fill_ext(a1_ext, ap_ref, am_ref, an_ref, i, n_i)

        sig_cg = _sigmoid(p_ext[:, C_CG[0]:C_CG[1]])
        a0_ext[...] = p_ext[:, C_CV[0]:C_CV[1]] * sig_cg
        a1_v = a1_ext[...]
        mu = jnp.mean(a1_v, axis=-1, keepdims=True)
        zc = a1_v - mu
        var = jnp.mean(zc * zc, axis=-1, keepdims=True)
        rstd = lax.rsqrt(var + LN_EPS)
        xhat = zc * rstd
        ln = xhat * cg_ref[...] + cbb_ref[...]
        sl = _sigmoid(ln)
        dln = dy_ext[:, 640:1024] * (sl * (1.0 + ln * (1.0 - sl)))
        dcg_ref[...] += jnp.sum((dln * xhat)[main], axis=0, keepdims=True)
        dcbb_ref[...] += jnp.sum(dln[main], axis=0, keepdims=True)
        dxh = dln * cg_ref[...]
        m1 = jnp.mean(dxh, axis=-1, keepdims=True)
        m2 = jnp.mean(dxh * xhat, axis=-1, keepdims=True)
        da1 = rstd * (dxh - m1 - xhat * m2)
        da1_ext[...] = da1
        da1_m = da1[main]
        dcb_ref[...] += jnp.sum(da1_m, axis=0, keepdims=True)
        _fill_phases(ph, da1_ext)
        for c0 in range(0, tm, CONV_ROWS):
            acc = jnp.zeros((CONV_ROWS, CONV_W), F32)
            for o in range(-15, 16):
                acc = acc + cw_ref[15 - o:16 - o, :] * _phase_win(ph, o, c0, CONV_ROWS)
            da0_s[c0:c0 + CONV_ROWS, :] = acc
        da0 = da0_s[...]
        _fill_phases(ph, a0_ext)
        for c0 in range(0, tm, CONV_ROWS):
            da1_c = da1_ext[HALO + c0:HALO + c0 + CONV_ROWS, :]
            for o in range(-15, 16):
                dcw_ref[o + 15:o + 16, :] += jnp.sum(da1_c * _phase_win(ph, o, c0, CONV_ROWS), axis=0, keepdims=True)
        sig_m = sig_cg[main]
        cv_m = p_ext[main, C_CV[0]:C_CV[1]]
        dproj_ref[:, C_CV[0]:C_CV[1]] = (da0 * sig_m).astype(BF16)
        dproj_ref[:, C_CG[0]:C_CG[1]] = (da0 * cv_m * sig_m * (1.0 - sig_m)).astype(BF16)

        q_ext[...] = p_ext[:, C_GC[0]:C_GC[1]] * p_ext[:, C_V[0]:C_V[1]]
        dc_ext[...] = dy_ext[:, 256:640] * p_ext[:, C_GB[0]:C_GB[1]]
        dc_m = dc_ext[main, :]
        conv = jnp.zeros((tm, CONV_W), F32)
        dq = jnp.zeros((tm, CONV_W), F32)
        for k in range(SCONV_K):
            q_k = q_ext[HALO + k - 1:HALO + k - 1 + tm, :]
            conv = conv + sw_ref[k:k + 1, :] * q_k
            dq = dq + sw_ref[k:k + 1, :] * dc_ext[HALO - k + 1:HALO - k + 1 + tm, :]
            dsw_ref[k:k + 1, :] += jnp.sum(dc_m * q_k, axis=0, keepdims=True)
        dproj_ref[:, C_GB[0]:C_GB[1]] = (dy_ext[main, 256:640] * conv).astype(BF16)
        dproj_ref[:, C_GC[0]:C_GC[1]] = (dq * p_ext[main, C_V[0]:C_V[1]]).astype(BF16)
        dproj_ref[:, C_V[0]:C_V[1]] = (dq * p_ext[main, C_GC[0]:C_GC[1]]).astype(BF16)

        t0 = i * tm
        dya = dy_ext[:, 0:256] * ps_ref[...]
        dpooled = _dot(dya.astype(BF16), pwt_ref[...])
        dpn_ext[...] = dpooled * _pool_inv_count(t0 - HALO, e, t)
        half = _pool_lane_half()
        du = jnp.zeros((tm, POOL_W), F32)
        for o, win in _taps(dpn_ext, slice(None), range(-7, 9), tm, tmp):
            m = ((o > -half) & (o <= half)).astype(F32)
            du = du + m * win
        dproj_ref[:, C_POOL[0]:C_POOL[1]] = (du - dpooled[main]).astype(BF16)
        pooled = _pool_forward(p_ext, tm, t0, t, tmp)
        pooled_b = pooled.astype(BF16)
        ya_pre = _dot(pooled_b, pw_ref[...])
        dps_ref[...] += jnp.sum(dy_ext[main, 0:256] * ya_pre, axis=0, keepdims=True)
        dpw_ref[...] += _dot_tn(pooled_b, dya[main].astype(BF16))

        dwin_ref[...] += _dot_tn(yb_ref[...], dproj_ref[...])
        dwout_ref[...] += _dot_tn(yc_ref[...], dm_ref[...])

    one = lambda i: (0, 0)
    tok = lambda i: (i, 0)
    small = [((d, pc), F32), ((ycat.shape[1], d), F32),
             ((POOL_W, POOL_W), F32), ((1, POOL_W), F32), ((8, CONV_W), F32), ((32, CONV_W), F32),
             ((1, CONV_W), F32), ((1, CONV_W), F32), ((1, CONV_W), F32)]
    return pl.pallas_call(
        body, name="mix_bwd",
        grid=(n_i,),
        in_specs=_halo_specs(tm, pc, t) + _halo_specs(tm, dzb.shape[1], t) + _halo_specs(tm, CONV_W, t) + [
            pl.BlockSpec((tm, d), tok), pl.BlockSpec((tm, ycat.shape[1]), tok),
            pl.BlockSpec(w_out_t.shape, one),
            pl.BlockSpec((POOL_W, POOL_W), one), pl.BlockSpec((POOL_W, POOL_W), one), pl.BlockSpec((1, POOL_W), one),
            pl.BlockSpec((8, CONV_W), one), pl.BlockSpec((32, CONV_W), one),
            pl.BlockSpec((1, CONV_W), one), pl.BlockSpec((1, CONV_W), one)],
        out_specs=[pl.BlockSpec((tm, pc), lambda i: (i, 0))] + [pl.BlockSpec(s, one) for s, _ in small],
        out_shape=[jax.ShapeDtypeStruct((t, pc), BF16)] + [jax.ShapeDtypeStruct(s, dt) for s, dt in small],
        scratch_shapes=[pltpu.VMEM((e, pc), F32), pltpu.VMEM((e, dzb.shape[1]), BF16), pltpu.VMEM((e, 1024), F32),
                        pltpu.VMEM((e, CONV_W), F32),
                        pltpu.VMEM((e, CONV_W), F32), pltpu.VMEM((e, CONV_W), F32), pltpu.VMEM((e, CONV_W), F32),
                        pltpu.VMEM((e, CONV_W), F32), pltpu.VMEM((e, POOL_W), F32),
                        pltpu.VMEM((e - 8, CONV_W), F32),
                        pltpu.VMEM((8, e - 8, CONV_W), F32), pltpu.VMEM((tm, CONV_W), F32)],
        compiler_params=_cparams(),
    )(proj, proj, proj, dzb, dzb, dzb, a1, a1, a1, yb, ycat,
      w_out_t, pw_bd, pw_bd_t, pool_scale, sconv_w, cconv_w, cnorm_g, cnorm_b)


def _mesh_pos():
    return lax.axis_index("x"), lax.axis_index("y"), lax.axis_index("c")


def _flip(v, f):
    return 1 - v if f else v


class _Comm:
    def __init__(self, kind, arrs, lands=None, layer=0):
        self.kind = kind
        self.arrs = list(arrs)
        self.n = len(self.arrs)
        self.lands = None if lands is None else list(lands)
        self.layer = layer
        if kind == "gather":
            self.flips = [(1, 0, 0), (0, 1, 0), (1, 1, 0)]
            self.out_shape = [jax.ShapeDtypeStruct((N_CHIPS,) + a.shape, a.dtype) for a in self.arrs]
        else:
            self.flips = [(fx, fy, fc) for fx in (0, 1) for fy in (0, 1) for fc in (0, 1)][1:]
            self.out_shape = [jax.ShapeDtypeStruct(b.shape, b.dtype) for b in self.lands]
        self.n_peer = len(self.flips)

    def operands(self):
        return self.arrs + (self.lands or [])

    def scratch(self):
        return [pltpu.SemaphoreType.DMA((self.n * self.n_peer,)), pltpu.SemaphoreType.DMA((self.n * self.n_peer,)),
                pltpu.SemaphoreType.DMA((self.n,))]

    def aliases(self, in_off, out_off):
        if self.lands is None:
            return {}
        return {in_off + self.n + j: out_off + j for j in range(self.n)}

    def _copies(self, ins, outs, sems):
        send_sems, recv_sems, loc_sems = sems
        x, y, c = _mesh_pos()
        local, sends, recvs = [], [], []
        for a in range(self.n):
            def src(px, py):
                return ins[a] if self.kind == "gather" else ins[a].at[2 * px + py]

            def dst(px, py, pc):
                if self.kind == "gather":
                    return outs[a].at[2 * px + py]
                return outs[a].at[4 * px + 2 * py + pc, self.layer]

            local.append(pltpu.make_async_copy(src(x, y), dst(x, y, c), loc_sems.at[a]))
            for k, (fx, fy, fc) in enumerate(self.flips):
                px, py, pc = _flip(x, fx), _flip(y, fy), _flip(c, fc)
                sem = a * self.n_peer + k
                sends.append(pltpu.make_async_remote_copy(
                    src_ref=src(px, py), dst_ref=dst(x, y, c),
                    send_sem=send_sems.at[sem], recv_sem=recv_sems.at[sem],
                    device_id=(px, py, pc), device_id_type=MESH))
                recvs.append(pltpu.make_async_remote_copy(
                    src_ref=src(px, py), dst_ref=dst(px, py, pc),
                    send_sem=send_sems.at[sem], recv_sem=recv_sems.at[sem],
                    device_id=(px, py, pc), device_id_type=MESH))
        return local, sends, recvs

    def start(self, ins, outs, sems):
        local, sends, _ = self._copies(ins, outs, sems)
        for cp in local + sends:
            cp.start()

    def wait(self, ins, outs, sems):
        local, _, recvs = self._copies(ins, outs, sems)
        for cp in local:
            cp.wait()
        for cp in recvs:
            cp.wait()


def _split_refs(refs, n_in, n_out, n_scr, comm):
    c_in = len(comm.operands()) if comm is not None else 0
    c_out = comm.n if comm is not None else 0
    cuts = [n_in, c_in, n_out, c_out, n_scr]
    out, pos = [], 0
    for m in cuts:
        out.append(refs[pos:pos + m])
        pos += m
    out.append(refs[pos:])
    return out


def _call_with_comm(body, name, grid, in_specs, out_specs, out_shape, scratch, args, comm):
    hbm = pl.BlockSpec(memory_space=pl.ANY)
    aliases = {}
    if comm is not None:
        aliases = comm.aliases(len(in_specs), len(out_specs))
        in_specs = in_specs + [hbm] * len(comm.operands())
        out_specs = out_specs + [hbm] * comm.n
        out_shape = out_shape + comm.out_shape
        scratch = scratch + comm.scratch()
        args = args + comm.operands()
        name = name + "_" + comm.kind
    res = pl.pallas_call(
        body, name=name, grid=grid, in_specs=in_specs, out_specs=out_specs, out_shape=out_shape,
        scratch_shapes=scratch, input_output_aliases=aliases, compiler_params=_cparams(),
    )(*args)
    if comm is None:
        return res, None
    return res[:len(res) - comm.n], res[len(res) - comm.n:]


def run_comm(comm):
    def body(*refs):
        _, c_in, _, c_out, _, c_sem = _split_refs(refs, 0, 0, 0, comm)
        comm.start(c_in, c_out, c_sem)
        comm.wait(c_in, c_out, c_sem)

    hbm = pl.BlockSpec(memory_space=pl.ANY)
    return pl.pallas_call(
        body, name="comm_" + comm.kind,
        in_specs=[hbm] * len(comm.operands()), out_specs=[hbm] * comm.n, out_shape=comm.out_shape,
        scratch_shapes=comm.scratch(), input_output_aliases=comm.aliases(0, 0),
    )(*comm.operands())


def exchange(arrs, per_chip):
    n = len(arrs)
    flips = [(fx, fy, fc) for fx in (0, 1) for fy in (0, 1) for fc in (0, 1)][1:]

    def body(*refs):
        ins, outs = refs[:n], refs[n:2 * n]
        send_sems, recv_sems, loc_sems = refs[2 * n:]
        x, y, c = _mesh_pos()
        me = 4 * x + 2 * y + c
        local = []
        remote = []
        for a in range(n):
            def part(px, py):
                return ins[a].at[2 * px + py] if per_chip else ins[a]

            cp = pltpu.make_async_copy(part(x, y), outs[a].at[me], loc_sems.at[a])
            cp.start()
            local.append(cp)
            for k, (fx, fy, fc) in enumerate(flips):
                px, py, pc = _flip(x, fx), _flip(y, fy), _flip(c, fc)
                sem = a * 7 + k
                rc = pltpu.make_async_remote_copy(
                    src_ref=part(px, py), dst_ref=outs[a].at[me],
                    send_sem=send_sems.at[sem], recv_sem=recv_sems.at[sem],
                    device_id=(px, py, pc), device_id_type=MESH)
                rc.start()
                remote.append(pltpu.make_async_remote_copy(
                    src_ref=part(px, py), dst_ref=outs[a].at[4 * px + 2 * py + pc],
                    send_sem=send_sems.at[sem], recv_sem=recv_sems.at[sem],
                    device_id=(px, py, pc), device_id_type=MESH))
        for cp in local:
            cp.wait()
        for rc in remote:
            rc.wait()

    hbm = pl.BlockSpec(memory_space=pl.ANY)
    shapes = [a.shape[1:] if per_chip else a.shape for a in arrs]
    return pl.pallas_call(
        body, name="exchange_per_chip" if per_chip else "exchange_all",
        in_specs=[hbm] * n, out_specs=[hbm] * n,
        out_shape=[jax.ShapeDtypeStruct((N_DEV,) + s, a.dtype) for s, a in zip(shapes, arrs)],
        scratch_shapes=[pltpu.SemaphoreType.DMA((7 * n,)), pltpu.SemaphoreType.DMA((7 * n,)),
                        pltpu.SemaphoreType.DMA((n,))],
    )(*arrs)


def adamw(parts, w, m, v, name):
    k_n, n_l, r, c = parts.shape
    tr = r
    for cand in (512, 256, 128, 64, 32, 16, 8):
        if r % cand == 0:
            tr = cand
            break

    def body(p_ref, w_ref, m_ref, v_ref, g_ref, d_ref, mo_ref, vo_ref):
        g = p_ref[0, 0].astype(F32)
        for k in range(1, k_n):
            g = g + p_ref[k, 0].astype(F32)
        m_new = ADAM_B1 * m_ref[0] + (1.0 - ADAM_B1) * g
        v_new = ADAM_B2 * v_ref[0] + (1.0 - ADAM_B2) * (g * g)
        m_hat = m_new / (1.0 - ADAM_B1 ** ADAM_STEP)
        v_hat = v_new / (1.0 - ADAM_B2 ** ADAM_STEP)
        g_ref[0] = g
        d_ref[0] = -ADAM_LR * (m_hat / (jnp.sqrt(v_hat) + ADAM_EPS) + ADAM_WD * w_ref[0])
        mo_ref[0] = m_new
        vo_ref[0] = v_new

    blk = pl.BlockSpec((1, tr, c), lambda l, i: (l, i, 0))
    return pl.pallas_call(
        body, name=name,
        grid=(n_l, r // tr),
        in_specs=[pl.BlockSpec((k_n, 1, tr, c), lambda l, i: (0, l, i, 0)), blk, blk, blk],
        out_specs=[blk, blk, blk, blk],
        out_shape=[jax.ShapeDtypeStruct((n_l, r, c), F32)] * 4,
        compiler_params=_cparams(),
    )(parts, w, m, v)


def _block_diag(pool_w):
    out = jnp.zeros((POOL_W, POOL_W), pool_w.dtype)
    for g in range(4):
        out = lax.dynamic_update_slice(out, pool_w[g], (64 * g, 64 * g))
    return out


def _pad_rows(a, rows):
    return jnp.pad(a, ((0, rows - a.shape[0]), (0, 0)))


def kernel(x, ln1_g, ln1_b, ffn1_w_gate, ffn1_w_up, ffn1_w_down, mix_w_in, pool_w, pool_scale, sconv_w, cconv_w, cconv_b, cnorm_g, cnorm_b, mix_w_out, ln2_g, ln2_b, ffn2_w_gate, ffn2_w_up, ffn2_w_down, ln3_g, ln3_b, loss_target, m_ln1_g, m_ln1_b, m_ffn1_w_gate, m_ffn1_w_up, m_ffn1_w_down, m_mix_w_in, m_pool_w, m_pool_scale, m_sconv_w, m_cconv_w, m_cconv_b, m_cnorm_g, m_cnorm_b, m_mix_w_out, m_ln2_g, m_ln2_b, m_ffn2_w_gate, m_ffn2_w_up, m_ffn2_w_down, m_ln3_g, m_ln3_b, v_ln1_g, v_ln1_b, v_ffn1_w_gate, v_ffn1_w_up, v_ffn1_w_down, v_mix_w_in, v_pool_w, v_pool_scale, v_sconv_w, v_cconv_w, v_cconv_b, v_cnorm_g, v_cnorm_b, v_mix_w_out, v_ln2_g, v_ln2_b, v_ffn2_w_gate, v_ffn2_w_up, v_ffn2_w_down, v_ln3_g, v_ln3_b):
    names = ['ln1_g', 'ln1_b', 'ffn1_w_gate', 'ffn1_w_up', 'ffn1_w_down', 'mix_w_in', 'pool_w', 'pool_scale',
             'sconv_w', 'cconv_w', 'cconv_b', 'cnorm_g', 'cnorm_b', 'mix_w_out', 'ln2_g', 'ln2_b',
             'ffn2_w_gate', 'ffn2_w_up', 'ffn2_w_down', 'ln3_g', 'ln3_b']
    w = dict(zip(names, (ln1_g, ln1_b, ffn1_w_gate, ffn1_w_up, ffn1_w_down, mix_w_in, pool_w, pool_scale, sconv_w,
                         cconv_w, cconv_b, cnorm_g, cnorm_b, mix_w_out, ln2_g, ln2_b, ffn2_w_gate, ffn2_w_up,
                         ffn2_w_down, ln3_g, ln3_b)))
    mom_m = dict(zip(names, (m_ln1_g, m_ln1_b, m_ffn1_w_gate, m_ffn1_w_up, m_ffn1_w_down, m_mix_w_in, m_pool_w,
                             m_pool_scale, m_sconv_w, m_cconv_w, m_cconv_b, m_cnorm_g, m_cnorm_b, m_mix_w_out,
                             m_ln2_g, m_ln2_b, m_ffn2_w_gate, m_ffn2_w_up, m_ffn2_w_down, m_ln3_g, m_ln3_b)))
    mom_v = dict(zip(names, (v_ln1_g, v_ln1_b, v_ffn1_w_gate, v_ffn1_w_up, v_ffn1_w_down, v_mix_w_in, v_pool_w,
                             v_pool_scale, v_sconv_w, v_cconv_w, v_cconv_b, v_cnorm_g, v_cnorm_b, v_mix_w_out,
                             v_ln2_g, v_ln2_b, v_ffn2_w_gate, v_ffn2_w_up, v_ffn2_w_down, v_ln3_g, v_ln3_b)))
    big = ['ffn1_w_gate', 'ffn1_w_up', 'ffn1_w_down', 'mix_w_in', 'mix_w_out',
           'ffn2_w_gate', 'ffn2_w_up', 'ffn2_w_down']
    n_l = ln1_g.shape[0]
    d = x.shape[-1]
    fs = ffn1_w_gate.shape[-1]
    ws_in = mix_w_in.shape[-1]
    cs = sconv_w.shape[-1]
    chip = 2 * lax.axis_index("x") + lax.axis_index("y")

    keys_a = ['ffn2_w_gate', 'ffn2_w_up', 'ffn2_w_down', 'mix_w_in', 'mix_w_out']
    keys_b = ['ffn1_w_gate', 'ffn1_w_up', 'ffn1_w_down']

    def shards_b(l):
        return [w[k][l].astype(BF16) for k in keys_b]

    def shards_a(l):
        conv_loc = jnp.concatenate([sconv_w[l], cconv_w[l]], axis=0)
        return [w[k][l].astype(BF16) for k in keys_a] + [conv_loc]

    def cols(a):
        return jnp.transpose(a, (1, 0, 2)).reshape(a.shape[1], -1)

    def layer_weights(l, got_b, got_a):
        gw = dict(zip(keys_b + keys_a, list(got_b) + list(got_a[:-1])))
        conv_all = jnp.transpose(got_a[-1], (1, 0, 2)).reshape(SCONV_K + CCONV_K, N_CHIPS * cs)
        return dict(
            g1=gw['ffn1_w_gate'], u1=gw['ffn1_w_up'], d1=gw['ffn1_w_down'],
            g2=gw['ffn2_w_gate'], u2=gw['ffn2_w_up'], d2=gw['ffn2_w_down'],
            w_in=cols(gw['mix_w_in']),
            w_out=gw['mix_w_out'].reshape(-1, d),
            pw=_block_diag(pool_w[l]).astype(BF16),
            ps=pool_scale[l][None], sw=_pad_rows(conv_all[:SCONV_K], 8), cw=_pad_rows(conv_all[SCONV_K:], 32),
            cb=cconv_b[l][None], cg=cnorm_g[l][None], cbb=cnorm_b[l][None],
        )

    h = x[0]
    hb = h.astype(BF16)
    target = loss_target[0]
    saved = []
    layer_w = []
    got_b = run_comm(_Comm("gather", shards_b(0)))
    for l in range(n_l):
        (y1, y1b, z1, gs1, us1), got_a = ffn_fwd(h, hb, got_b[0], got_b[1], got_b[2], ln1_g[l][None],
                                                 ln1_b[l][None], _Comm("gather", shards_a(l)))
        lw = layer_weights(l, got_b, got_a)
        layer_w.append(lw)
        proj, ycat, a1, y2, y2b, z2 = mix_fwd(y1, y1b, lw['w_in'], lw['w_out'], ln2_g[l][None], ln2_b[l][None],
                                              lw['pw'], lw['ps'], lw['sw'], lw['cw'], lw['cb'], lw['cg'], lw['cbb'])
        comm = _Comm("gather", shards_b(l + 1)) if l + 1 < n_l else None
        (y3, y3b, z3, gs2, us2), got_b = ffn_fwd(y2, y2b, lw['g2'], lw['u2'], lw['d2'], ln3_g[l][None],
                                                 ln3_b[l][None], comm)
        saved.append(dict(x0b=hb, z1=z1, y1b=y1b, proj=proj, ycat=ycat, a1=a1, z2=z2, y2b=y2b, z3=z3,
                          gs1=gs1, us1=us1, gs2=gs2, us2=us2))
        h, hb = y3, y3b

    dy, loss_blk = loss_and_grad(h, target)
    loss = lax.psum(loss_blk[0, 0], ("x", "y", "c"))

    g_loc = {k: [None] * n_l for k in names if k not in big}
    transposed = ['ffn1_w_gate', 'ffn1_w_up', 'ffn2_w_gate', 'ffn2_w_up', 'mix_w_in']

    def shard_shape(k):
        shp = w[k].shape[1:]
        return shp[::-1] if k in transposed else shp

    lands = {k: lax.empty((N_DEV, n_l) + shard_shape(k), BF16) for k in big}

    def scatter(keys, arrs, layer):
        return _Comm("scatter", arrs, [lands[k] for k in keys], layer)

    pend_b = None
    parts = None
    for l in reversed(range(n_l)):
        lw, sv = layer_w[l], saved[l]
        dzb, dxr, g_loc['ln3_g'][l], g_loc['ln3_b'][l] = ln_bwd(dy, parts, sv['z3'], ln3_g[l][None])
        comm = None if pend_b is None else scatter(keys_b, pend_b, l + 1)
        (parts, dg2, du2, dd2), landed = ffn_bwd(sv['y2b'], dzb, sv['gs2'], sv['us2'], lw['g2'], lw['u2'],
                                                         lw['d2'], comm)
        if comm is not None:
            lands.update(zip(keys_b, landed))
        dzb, dxr, g_loc['ln2_g'][l], g_loc['ln2_b'][l] = ln_bwd(dxr, parts, sv['z2'], ln2_g[l][None])
        (dproj, dw_in, dw_out, dpw, g_loc['pool_scale'][l], dsw, dcw, g_loc['cconv_b'][l], g_loc['cnorm_g'][l],
         g_loc['cnorm_b'][l]) = mix_bwd(sv['proj'], dzb, lw['w_out'].T, sv['a1'], sv['y1b'], sv['ycat'],
                                        lw['pw'], lw['pw'].T, lw['ps'], lw['sw'], lw['cw'], lw['cg'], lw['cbb'])
        g_loc['pool_w'][l] = jnp.stack([dpw[64 * g:64 * g + 64, 64 * g:64 * g + 64] for g in range(4)])
        g_loc['sconv_w'][l] = dsw[:SCONV_K]
        g_loc['cconv_w'][l] = dcw[:CCONV_K]
        dw_in_c = jnp.transpose(dw_in.reshape(d, N_CHIPS, ws_in), (1, 2, 0)).astype(BF16)
        dw_out_c = dw_out.reshape(N_CHIPS, -1, d).astype(BF16)
        dzb, dxr, g_loc['ln1_g'][l], g_loc['ln1_b'][l] = ln_bwd(dxr, None, sv['z1'], ln1_g[l][None],
                                                                 mm=(dproj, lw['w_in'].T))
        comm = scatter(keys_a, [dg2, du2, dd2, dw_in_c, dw_out_c], l)
        (parts, dg1, du1, dd1), landed = ffn_bwd(sv['x0b'], dzb, sv['gs1'], sv['us1'], lw['g1'], lw['u1'],
                                                         lw['d1'], comm)
        lands.update(zip(keys_a, landed))
        pend_b = [dg1, du1, dd1]
        dy = dxr
    grad_x = add_parts(dy, parts)[None]
    lands.update(zip(keys_b, run_comm(scatter(keys_b, pend_b, 0))))
    parts_big = lands

    small = [k for k in names if k not in big]
    small_full = {}
    for k in small:
        a = jnp.stack(g_loc[k])
        small_full[k] = a.reshape(n_l, -1) if a.shape[1] == 1 else a
    flat = jnp.concatenate([small_full[k].reshape(-1) for k in small])
    n_flat = flat.shape[0]
    rows = -(-n_flat // (SMALL_ROWS * 128)) * SMALL_ROWS
    flat = jnp.pad(flat, (0, rows * 128 - n_flat)).reshape(rows, 128)
    parts_small = exchange([flat], per_chip=False)[0]

    out_g, out_d, out_m, out_v = {}, {}, {}, {}
    for k in big:
        tr = (lambda a: jnp.swapaxes(a, 1, 2)) if k in transposed else (lambda a: a)
        res = adamw(parts_big[k], tr(w[k]), tr(mom_m[k]), tr(mom_v[k]), name="adamw_" + k)
        out_g[k], out_d[k], out_m[k], out_v[k] = [tr(o) for o in res]

    zeros = jnp.zeros((1, rows, 128), F32)
    g_sum = adamw(parts_small[:, None], zeros, zeros, zeros, name="sum_small")[0].reshape(-1)
    off = 0
    for k in small:
        full = small_full[k]
        g = g_sum[off:off + full.size].reshape(full.shape)
        off += full.size
        if k in ('sconv_w', 'cconv_w'):
            g = lax.dynamic_slice_in_dim(g, chip * cs, cs, axis=2)
        out_g[k] = g.reshape(w[k].shape)

    def pack(dct):
        f = jnp.concatenate([dct[k].reshape(-1) for k in small])
        r2 = -(-f.shape[0] // (SMALL_ROWS * 128)) * SMALL_ROWS
        return jnp.pad(f, (0, r2 * 128 - f.shape[0])).reshape(r2, 128), f.shape[0]

    gp, n_small = pack(out_g)
    wp, _ = pack(w)
    mp, _ = pack(mom_m)
    vp, _ = pack(mom_v)
    _, dp, mo, vo = adamw(gp[None, None], wp[None], mp[None], vp[None], name="adamw_small")
    off = 0
    for k in small:
        sz = w[k].size
        out_d[k] = dp.reshape(-1)[off:off + sz].reshape(w[k].shape)
        out_m[k] = mo.reshape(-1)[off:off + sz].reshape(w[k].shape)
        out_v[k] = vo.reshape(-1)[off:off + sz].reshape(w[k].shape)
        off += sz

    return (loss, grad_x, *[out_g[k] for k in names], *[out_d[k] for k in names],
            *[out_m[k] for k in names], *[out_v[k] for k in names])
```

```python
import jax
import jax.numpy as jnp
from jax import lax
from jax.experimental import pallas as pl
from jax.experimental.pallas import tpu as pltpu

F32 = jnp.float32
BF16 = jnp.bfloat16

DEPTH = 4
ALPHA = (2.0 * DEPTH) ** 0.25
LN_EPS = 1e-5
POOL_W = 256
CONV_W = 384
SCONV_K = 3
CCONV_K = 31
C_POOL = (0, 256)
C_GB = (256, 640)
C_GC = (640, 1024)
C_V = (1024, 1408)
C_CV = (1408, 1792)
C_CG = (1792, 2176)

ADAM_LR = 0.001
ADAM_B1 = 0.9
ADAM_B2 = 0.999
ADAM_EPS = 1e-08
ADAM_WD = 0.01
ADAM_STEP = 10

N_CHIPS = 4
N_DEV = 8
MESH = pl.DeviceIdType.MESH

TM_FWD = 512
TM_BWD = 512
TM_MIX = 256
HALO = 32
SMALL_ROWS = 256
VMEM_LIMIT = 56 * 1024 * 1024


def _cparams():
    return pltpu.CompilerParams(vmem_limit_bytes=VMEM_LIMIT)


def _sigmoid(v):
    return 1.0 / (1.0 + jnp.exp(-v))


def _dot(a, b):
    return jnp.dot(a, b, preferred_element_type=F32)


def _dot_nt(a, b):
    return lax.dot_general(a, b, (((1,), (1,)), ((), ())), preferred_element_type=F32)


def _dot_tn(a, b):
    return lax.dot_general(a, b, (((0,), (0,)), ((), ())), preferred_element_type=F32)


def _ln_fwd(z, g, b):
    mu = jnp.mean(z, axis=-1, keepdims=True)
    zc = z - mu
    var = jnp.mean(zc * zc, axis=-1, keepdims=True)
    return zc * lax.rsqrt(var + LN_EPS) * g + b


def _ln_bwd(dy, z, g):
    mu = jnp.mean(z, axis=-1, keepdims=True)
    zc = z - mu
    var = jnp.mean(zc * zc, axis=-1, keepdims=True)
    rstd = lax.rsqrt(var + LN_EPS)
    xhat = zc * rstd
    dxh = dy * g
    m1 = jnp.mean(dxh, axis=-1, keepdims=True)
    m2 = jnp.mean(dxh * xhat, axis=-1, keepdims=True)
    return rstd * (dxh - m1 - xhat * m2), xhat


def _tile(t, tm):
    tm = min(tm, t)
    assert t % tm == 0, (t, tm)
    return tm


def ffn_fwd(x, xb, wg, wu, wd, ln_g, ln_b, comm=None):
    t, d = x.shape
    s_n, _, fs = wg.shape
    tm = _tile(t, TM_FWD)
    n_i = t // tm

    def body(*refs):
        ((x_ref, xb_ref, wg_ref, wu_ref, wd_ref, g_ref, b_ref), c_in, (y_ref, yb_ref, z_ref, gs_ref, us_ref), c_out,
         (acc_ref,), c_sem) = _split_refs(refs, 7, 5, 1, comm)
        i = pl.program_id(0)
        s = pl.program_id(1)
        if comm is not None:
            @pl.when(jnp.logical_and(i == 0, s == 0))
            def _():
                comm.start(c_in, c_out, c_sem)

        xb = xb_ref[...]
        g = _dot(xb, wg_ref[0])
        u = _dot(xb, wu_ref[0])
        gs_ref[0] = g.astype(BF16)
        us_ref[0] = u.astype(BF16)
        a = (g * _sigmoid(g) * u).astype(BF16)
        part = _dot(a, wd_ref[0])

        @pl.when(s == 0)
        def _():
            acc_ref[...] = part

        @pl.when(s > 0)
        def _():
            acc_ref[...] += part

        @pl.when(s == s_n - 1)
        def _():
            z = ALPHA * x_ref[...] + 0.5 * acc_ref[...]
            z_ref[...] = z
            y = _ln_fwd(z, g_ref[...], b_ref[...])
            y_ref[...] = y
            yb_ref[...] = y.astype(BF16)

        if comm is not None:
            @pl.when(jnp.logical_and(i == n_i - 1, s == s_n - 1))
            def _():
                comm.wait(c_in, c_out, c_sem)

    tok = lambda i, s: (i, 0)
    one = lambda i, s: (0, 0)
    return _call_with_comm(
        body, "ffn_fwd", (n_i, s_n),
        [pl.BlockSpec((tm, d), tok), pl.BlockSpec((tm, d), tok),
         pl.BlockSpec((1, d, fs), lambda i, s: (s, 0, 0)),
         pl.BlockSpec((1, d, fs), lambda i, s: (s, 0, 0)),
         pl.BlockSpec((1, fs, d), lambda i, s: (s, 0, 0)),
         pl.BlockSpec((1, d), one), pl.BlockSpec((1, d), one)],
        [pl.BlockSpec((tm, d), tok), pl.BlockSpec((tm, d), tok), pl.BlockSpec((tm, d), tok),
         pl.BlockSpec((1, tm, fs), lambda i, s: (s, i, 0)), pl.BlockSpec((1, tm, fs), lambda i, s: (s, i, 0))],
        [jax.ShapeDtypeStruct((t, d), F32), jax.ShapeDtypeStruct((t, d), BF16), jax.ShapeDtypeStruct((t, d), F32),
         jax.ShapeDtypeStruct((s_n, t, fs), BF16), jax.ShapeDtypeStruct((s_n, t, fs), BF16)],
        [pltpu.VMEM((tm, d), F32)],
        [x, xb, wg, wu, wd, ln_g, ln_b], comm)


def ffn_bwd(xb, dzb, gs, us, wg, wu, wd, comm=None):
    t, d = xb.shape
    s_n, _, fs = wg.shape
    tm = _tile(t, TM_BWD)
    n_i = t // tm

    def body(*refs):
        ((x_ref, dzb_ref, gs_ref, us_ref, wg_ref, wu_ref, wd_ref), c_in, (dx_ref, dwg_ref, dwu_ref, dwd_ref), c_out,
         (accg, accu, accd), c_sem) = _split_refs(refs, 7, 4, 3, comm)
        s = pl.program_id(0)
        i = pl.program_id(1)
        if comm is not None:
            @pl.when(jnp.logical_and(i == 0, s == 0))
            def _():
                comm.start(c_in, c_out, c_sem)

        @pl.when(i == 0)
        def _():
            accg[...] = jnp.zeros_like(accg)
            accu[...] = jnp.zeros_like(accu)
            accd[...] = jnp.zeros_like(accd)

        x_v = x_ref[...]
        g = gs_ref[0].astype(F32)
        u = us_ref[0].astype(F32)
        sg = _sigmoid(g)
        si = g * sg
        a = (si * u).astype(BF16)
        dfb = dzb_ref[...] * 0.5
        da = _dot_nt(dfb, wd_ref[0])
        dgate = (da * u * (sg * (1.0 + g * (1.0 - sg)))).astype(BF16)
        dup = (da * si).astype(BF16)
        dx_ref[0] = (_dot_nt(dgate, wg_ref[0]) + _dot_nt(dup, wu_ref[0])).astype(BF16)
        accg[...] += _dot_tn(dgate, x_v)
        accu[...] += _dot_tn(dup, x_v)
        accd[...] += _dot_tn(a, dfb)

        @pl.when(i == n_i - 1)
        def _():
            dwg_ref[0] = accg[...].astype(BF16)
            dwu_ref[0] = accu[...].astype(BF16)
            dwd_ref[0] = accd[...].astype(BF16)

        if comm is not None:
            @pl.when(jnp.logical_and(i == n_i - 1, s == s_n - 1))
            def _():
                comm.wait(c_in, c_out, c_sem)

    tok = lambda s, i: (i, 0)
    shard = lambda s, i: (s, 0, 0)
    return _call_with_comm(
        body, "ffn_bwd", (s_n, n_i),
        [pl.BlockSpec((tm, d), tok), pl.BlockSpec((tm, d), tok),
         pl.BlockSpec((1, tm, fs), lambda s, i: (s, i, 0)), pl.BlockSpec((1, tm, fs), lambda s, i: (s, i, 0)),
         pl.BlockSpec((1, d, fs), shard), pl.BlockSpec((1, d, fs), shard), pl.BlockSpec((1, fs, d), shard)],
        [pl.BlockSpec((1, tm, d), lambda s, i: (s, i, 0)),
         pl.BlockSpec((1, fs, d), shard), pl.BlockSpec((1, fs, d), shard), pl.BlockSpec((1, fs, d), shard)],
        [jax.ShapeDtypeStruct((s_n, t, d), BF16),
         jax.ShapeDtypeStruct((s_n, fs, d), BF16), jax.ShapeDtypeStruct((s_n, fs, d), BF16),
         jax.ShapeDtypeStruct((s_n, fs, d), BF16)],
        [pltpu.VMEM((fs, d), F32), pltpu.VMEM((fs, d), F32), pltpu.VMEM((fs, d), F32)],
        [xb, dzb, gs, us, wg, wu, wd], comm)


def _sum_parts(base_ref, parts_ref):
    v = base_ref[...]
    if parts_ref is not None:
        for p in range(parts_ref.shape[0]):
            v = v + parts_ref[p].astype(F32)
    return v


def ln_bwd(dy, parts, z, ln_g, mm=None):
    t, d = dy.shape
    tm = _tile(t, TM_FWD)

    def body(*refs):
        refs = list(refs)
        dy_ref = refs.pop(0)
        parts_ref = refs.pop(0) if parts is not None else None
        a_ref, w_ref = (refs.pop(0), refs.pop(0)) if mm is not None else (None, None)
        z_ref, g_ref, dzb_ref, dxr_ref, dg_ref, db_ref = refs
        i = pl.program_id(0)
        dy_v = _sum_parts(dy_ref, parts_ref)
        if mm is not None:
            dy_v = dy_v + _dot(a_ref[...], w_ref[...])
        dz, xhat = _ln_bwd(dy_v, z_ref[...], g_ref[...])
        dzb_ref[...] = dz.astype(BF16)
        dxr_ref[...] = ALPHA * dz

        @pl.when(i == 0)
        def _():
            dg_ref[...] = jnp.zeros_like(dg_ref)
            db_ref[...] = jnp.zeros_like(db_ref)

        dg_ref[...] += jnp.sum(dy_v * xhat, axis=0, keepdims=True)
        db_ref[...] += jnp.sum(dy_v, axis=0, keepdims=True)

    tok = lambda i: (i, 0)
    one = lambda i: (0, 0)
    in_specs = [pl.BlockSpec((tm, d), tok)]
    args = [dy]
    name = "ln_bwd"
    if parts is not None:
        in_specs.append(pl.BlockSpec((parts.shape[0], tm, d), lambda i: (0, i, 0)))
        args.append(parts)
        name += "_parts"
    if mm is not None:
        in_specs += [pl.BlockSpec((tm, mm[0].shape[1]), tok), pl.BlockSpec(mm[1].shape, one)]
        args += list(mm)
        name += "_mm"
    return pl.pallas_call(
        body, name=name,
        grid=(t // tm,),
        in_specs=in_specs + [pl.BlockSpec((tm, d), tok), pl.BlockSpec((1, d), one)],
        out_specs=[pl.BlockSpec((tm, d), tok), pl.BlockSpec((tm, d), tok),
                   pl.BlockSpec((1, d), one), pl.BlockSpec((1, d), one)],
        out_shape=[jax.ShapeDtypeStruct((t, d), BF16), jax.ShapeDtypeStruct((t, d), F32),
                   jax.ShapeDtypeStruct((1, d), F32), jax.ShapeDtypeStruct((1, d), F32)],
        compiler_params=_cparams(),
    )(*args, z, ln_g)


def add_parts(base, parts, comm=None):
    t, d = base.shape
    tm = _tile(t, TM_FWD)
    n_i = t // tm

    def body(*refs):
        (b_ref, p_ref), c_in, (o_ref,), c_out, _, c_sem = _split_refs(refs, 2, 1, 0, comm)
        i = pl.program_id(0)
        if comm is not None:
            @pl.when(i == 0)
            def _():
                comm.start(c_in, c_out, c_sem)

        o_ref[...] = _sum_parts(b_ref, p_ref)

        if comm is not None:
            @pl.when(i == n_i - 1)
            def _():
                comm.wait(c_in, c_out, c_sem)

    tok = lambda i: (i, 0)
    return _call_with_comm(
        body, "add_parts", (n_i,),
        [pl.BlockSpec((tm, d), tok), pl.BlockSpec((parts.shape[0], tm, d), lambda i: (0, i, 0))],
        [pl.BlockSpec((tm, d), tok)], [jax.ShapeDtypeStruct((t, d), F32)], [], [base, parts], comm)


def loss_and_grad(y, target):
    t, d = y.shape
    tm = _tile(t, TM_FWD)

    def body(y_ref, t_ref, dy_ref, l_ref):
        i = pl.program_id(0)
        e = y_ref[...] - t_ref[...]
        dy_ref[...] = e * (1.0 / d)

        @pl.when(i == 0)
        def _():
            l_ref[...] = jnp.zeros_like(l_ref)

        l_ref[...] += (0.5 / d) * jnp.sum(e * e)

    tok = lambda i: (i, 0)
    return pl.pallas_call(
        body, name="loss",
        grid=(t // tm,),
        in_specs=[pl.BlockSpec((tm, d), tok), pl.BlockSpec((tm, d), tok)],
        out_specs=[pl.BlockSpec((tm, d), tok), pl.BlockSpec((8, 128), lambda i: (0, 0))],
        out_shape=[jax.ShapeDtypeStruct((t, d), F32), jax.ShapeDtypeStruct((8, 128), F32)],
        compiler_params=_cparams(),
    )(y, target)


def _halo_specs(tm, cols, n_rows):
    r = tm // HALO
    last = n_rows // HALO - 1
    return [pl.BlockSpec((HALO, cols), lambda i: (jnp.maximum(i * r - 1, 0), 0)),
            pl.BlockSpec((tm, cols), lambda i: (i, 0)),
            pl.BlockSpec((HALO, cols), lambda i: (jnp.minimum((i + 1) * r, last), 0))]


def _fill_ext(dst, prev_ref, main_ref, next_ref, i, n_i):
    tm = main_ref.shape[0]
    dst[0:HALO, :] = jnp.where(i > 0, prev_ref[...], 0.0)
    dst[HALO:HALO + tm, :] = main_ref[...]
    dst[HALO + tm:HALO + tm + HALO, :] = jnp.where(i < n_i - 1, next_ref[...], 0.0)


def _pool_lane_half():
    lane = lax.broadcasted_iota(jnp.int32, (1, POOL_W), 1)
    return jnp.left_shift(1, lane // 64)


def _pool_inv_count(t0, rows, seq):
    half = _pool_lane_half()
    tpos = t0 + lax.broadcasted_iota(jnp.int32, (rows, 1), 0)
    lo = jnp.maximum(tpos - half, 0)
    hi = jnp.minimum(tpos + half, seq)
    cnt = jnp.maximum(hi - lo, 1)
    return 1.0 / cnt.astype(F32)


def _taps(ext_ref, cols, offsets, tm, tmp_ref):
    e = ext_ref.shape[0]
    width = len(range(*cols.indices(ext_ref.shape[1])))
    by_phase = {}
    for o in offsets:
        by_phase.setdefault((HALO + o) % 8, []).append(o)
    for r, group in by_phase.items():
        if r == 0:
            for o in group:
                yield o, ext_ref[HALO + o:HALO + o + tm, cols]
            continue
        tmp_ref[:, 0:width] = ext_ref[r:r + e - 8, cols]
        for o in group:
            start = HALO + o - r
            yield o, tmp_ref[start:start + tm, 0:width]


def _pool_forward(p_ext, tm, t0, seq, tmp_ref):
    half = _pool_lane_half()
    total = jnp.zeros((tm, POOL_W), F32)
    for o, win in _taps(p_ext, slice(C_POOL[0], C_POOL[1]), range(-8, 8), tm, tmp_ref):
        m = ((o >= -half) & (o < half)).astype(F32)
        total = total + m * win
    u = p_ext[HALO:HALO + tm, C_POOL[0]:C_POOL[1]]
    return total * _pool_inv_count(t0, tm, seq) - u


CONV_ROWS = 64


def _fill_phases(ph_ref, ext_ref):
    e = ext_ref.shape[0]
    for r in range(8):
        ph_ref[r] = ext_ref[r:r + e - 8, :]


def _phase_win(ph_ref, o, row0, rows):
    r = (HALO + o) % 8
    start = HALO + o - r + row0
    return ph_ref[r, start:start + rows, :]


def mix_fwd(y, yb, w_in, w_out, ln_g, ln_b, pw_bd, pool_scale, sconv_w, cconv_w, cconv_b, cnorm_g, cnorm_b):
    t, d = yb.shape
    pc = w_in.shape[1]
    tm = _tile(t, TM_MIX)
    n_i = t // tm
    e = tm + 2 * HALO

    def body(yp_ref, ym_ref, yn_ref, y_ref, win_ref, wout_ref, lg_ref, lb_ref,
             pw_ref, ps_ref, sw_ref, cw_ref, cb_ref, cg_ref, cbb_ref,
             proj_ref, yc_ref, a1_ref, y2_ref, y2b_ref, z2_ref, yb_ext, p_ext, q_ext, a0_ext, tmp, ph, a1_s):
        i = pl.program_id(0)
        _fill_ext(yb_ext, yp_ref, ym_ref, yn_ref, i, n_i)
        p_ext[...] = _dot(yb_ext[...], win_ref[...])
        proj_ref[...] = p_ext[HALO:HALO + tm, :]
        pooled = _pool_forward(p_ext, tm, i * tm, t, tmp)
        y_a = _dot(pooled.astype(BF16), pw_ref[...]) * ps_ref[...]
        yc_ref[:, 0:256] = y_a.astype(BF16)
        q_ext[...] = p_ext[:, C_GC[0]:C_GC[1]] * p_ext[:, C_V[0]:C_V[1]]
        conv = jnp.zeros((tm, CONV_W), F32)
        for k in range(SCONV_K):
            conv = conv + sw_ref[k:k + 1, :] * q_ext[HALO + k - 1:HALO + k - 1 + tm, :]
        y_b = p_ext[HALO:HALO + tm, C_GB[0]:C_GB[1]] * conv
        yc_ref[:, 256:640] = y_b.astype(BF16)
        a0_ext[...] = p_ext[:, C_CV[0]:C_CV[1]] * _sigmoid(p_ext[:, C_CG[0]:C_CG[1]])
        _fill_phases(ph, a0_ext)
        for c0 in range(0, tm, CONV_ROWS):
            acc = jnp.zeros((CONV_ROWS, CONV_W), F32) + cb_ref[...]
            for o in range(-15, 16):
                acc = acc + cw_ref[o + 15:o + 16, :] * _phase_win(ph, o, c0, CONV_ROWS)
            a1_s[c0:c0 + CONV_ROWS, :] = acc
        a1 = a1_s[...]
        a1_ref[...] = a1
        ln = _ln_fwd(a1, cg_ref[...], cbb_ref[...])
        yc_ref[:, 640:1024] = (ln * _sigmoid(ln)).astype(BF16)
        z = ALPHA * y_ref[...] + _dot(yc_ref[...], wout_ref[...])
        z2_ref[...] = z
        y2 = _ln_fwd(z, lg_ref[...], lb_ref[...])
        y2_ref[...] = y2
        y2b_ref[...] = y2.astype(BF16)

    one = lambda i: (0, 0)
    tok = lambda i: (i, 0)
    return pl.pallas_call(
        body, name="mix_fwd",
        grid=(n_i,),
        in_specs=_halo_specs(tm, d, t) + [
            pl.BlockSpec((tm, d), tok), pl.BlockSpec((d, pc), one), pl.BlockSpec(w_out.shape, one),
            pl.BlockSpec((1, d), one), pl.BlockSpec((1, d), one),
            pl.BlockSpec((POOL_W, POOL_W), one), pl.BlockSpec((1, POOL_W), one),
            pl.BlockSpec((8, CONV_W), one), pl.BlockSpec((32, CONV_W), one),
            pl.BlockSpec((1, CONV_W), one), pl.BlockSpec((1, CONV_W), one), pl.BlockSpec((1, CONV_W), one)],
        out_specs=[pl.BlockSpec((tm, pc), tok), pl.BlockSpec((tm, 1024), tok), pl.BlockSpec((tm, CONV_W), tok),
                   pl.BlockSpec((tm, d), tok), pl.BlockSpec((tm, d), tok), pl.BlockSpec((tm, d), tok)],
        out_shape=[jax.ShapeDtypeStruct((t, pc), F32), jax.ShapeDtypeStruct((t, 1024), BF16),
                   jax.ShapeDtypeStruct((t, CONV_W), F32),
                   jax.ShapeDtypeStruct((t, d), F32), jax.ShapeDtypeStruct((t, d), BF16),
                   jax.ShapeDtypeStruct((t, d), F32)],
        scratch_shapes=[pltpu.VMEM((e, d), BF16), pltpu.VMEM((e, pc), F32), pltpu.VMEM((e, CONV_W), F32),
                        pltpu.VMEM((e, CONV_W), F32), pltpu.VMEM((e - 8, CONV_W), F32),
                        pltpu.VMEM((8, e - 8, CONV_W), F32), pltpu.VMEM((tm, CONV_W), F32)],
        compiler_params=_cparams(),
    )(yb, yb, yb, y, w_in, w_out, ln_g, ln_b, pw_bd, pool_scale, sconv_w, cconv_w, cconv_b, cnorm_g, cnorm_b)


def mix_bwd(proj, dzb, w_out_t, a1, yb, ycat, pw_bd, pw_bd_t, pool_scale, sconv_w, cconv_w, cnorm_g, cnorm_b):
    t, pc = proj.shape
    d = yb.shape[1]
    tm = _tile(t, TM_MIX)
    n_i = t // tm
    e = tm + 2 * HALO

    def body(pp_ref, pm_ref, pn_ref, dp_ref, dm_ref, dn_ref, ap_ref, am_ref, an_ref, yb_ref, yc_ref,
             wot_ref, pw_ref, pwt_ref, ps_ref, sw_ref, cw_ref, cg_ref, cbb_ref,
             dproj_ref, dwin_ref, dwout_ref, dpw_ref, dps_ref, dsw_ref, dcw_ref, dcb_ref, dcg_ref, dcbb_ref,
             p_ext, dz_ext, dy_ext, a1_ext, a0_ext, da1_ext, q_ext, dc_ext, dpn_ext, tmp, ph, da0_s):
        i = pl.program_id(0)
        main = slice(HALO, HALO + tm)

        @pl.when(i == 0)
        def _():
            for r in (dwin_ref, dwout_ref, dpw_ref, dps_ref, dsw_ref, dcw_ref, dcb_ref, dcg_ref, dcbb_ref):
                r[...] = jnp.zeros_like(r)

        _fill_ext(p_ext, pp_ref, pm_ref, pn_ref, i, n_i)
        _fill_ext(dz_ext, dp_ref, dm_ref, dn_ref, i, n_i)
        dy_ext[...] = _dot(dz_ext[...], wot_ref[...])
        _fill_ext(a1_ext, ap_ref, am_ref, an_ref, i, n_i)

        sig_cg = _sigmoid(p_ext[:, C_CG[0]:C_CG[1]])
        a0_ext[...] = p_ext[:, C_CV[0]:C_CV[1]] * sig_cg
        a1_v = a1_ext[...]
        mu = jnp.mean(a1_v, axis=-1, keepdims=True)
        zc = a1_v - mu
        var = jnp.mean(zc * zc, axis=-1, keepdims=True)
        rstd = lax.rsqrt(var + LN_EPS)
        xhat = zc * rstd
        ln = xhat * cg_ref[...] + cbb_ref[...]
        sl = _sigmoid(ln)
        dln = dy_ext[:, 640:1024] * (sl * (1.0 + ln * (1.0 - sl)))
        dcg_ref[...] += jnp.sum((dln * xhat)[main], axis=0, keepdims=True)
        dcbb_ref[...] += jnp.sum(dln[main], axis=0, keepdims=True)
        dxh = dln * cg_ref[...]
        m1 = jnp.mean(dxh, axis=-1, keepdims=True)
        m2 = jnp.mean(dxh * xhat, axis=-1, keepdims=True)
        da1 = rstd * (dxh - m1 - xhat * m2)
        da1_ext[...] = da1
        da1_m = da1[main]
        dcb_ref[...] += jnp.sum(da1_m, axis=0, keepdims=True)
        _fill_phases(ph, da1_ext)
        for c0 in range(0, tm, CONV_ROWS):
            acc = jnp.zeros((CONV_ROWS, CONV_W), F32)
            for o in range(-15, 16):
                acc = acc + cw_ref[15 - o:16 - o, :] * _phase_win(ph, o, c0, CONV_ROWS)
            da0_s[c0:c0 + CONV_ROWS, :] = acc
        da0 = da0_s[...]
        _fill_phases(ph, a0_ext)
        for c0 in range(0, tm, CONV_ROWS):
            da1_c = da1_ext[HALO + c0:HALO + c0 + CONV_ROWS, :]
            for o in range(-15, 16):
                dcw_ref[o + 15:o + 16, :] += jnp.sum(da1_c * _phase_win(ph, o, c0, CONV_ROWS), axis=0, keepdims=True)
        sig_m = sig_cg[main]
        cv_m = p_ext[main, C_CV[0]:C_CV[1]]
        dproj_ref[:, C_CV[0]:C_CV[1]] = (da0 * sig_m).astype(BF16)
        dproj_ref[:, C_CG[0]:C_CG[1]] = (da0 * cv_m * sig_m * (1.0 - sig_m)).astype(BF16)

        q_ext[...] = p_ext[:, C_GC[0]:C_GC[1]] * p_ext[:, C_V[0]:C_V[1]]
        dc_ext[...] = dy_ext[:, 256:640] * p_ext[:, C_GB[0]:C_GB[1]]
        dc_m = dc_ext[main, :]
        conv = jnp.zeros((tm, CONV_W), F32)
        dq = jnp.zeros((tm, CONV_W), F32)
        for k in range(SCONV_K):
            q_k = q_ext[HALO + k - 1:HALO + k - 1 + tm, :]
            conv = conv + sw_ref[k:k + 1, :] * q_k
            dq = dq + sw_ref[k:k + 1, :] * dc_ext[HALO - k + 1:HALO - k + 1 + tm, :]
            dsw_ref[k:k + 1, :] += jnp.sum(dc_m * q_k, axis=0, keepdims=True)
        dproj_ref[:, C_GB[0]:C_GB[1]] = (dy_ext[main, 256:640] * conv).astype(BF16)
        dproj_ref[:, C_GC[0]:C_GC[1]] = (dq * p_ext[main, C_V[0]:C_V[1]]).astype(BF16)
        dproj_ref[:, C_V[0]:C_V[1]] = (dq * p_ext[main, C_GC[0]:C_GC[1]]).astype(BF16)

        t0 = i * tm
        dya = dy_ext[:, 0:256] * ps_ref[...]
        dpooled = _dot(dya.astype(BF16), pwt_ref[...])
        dpn_ext[...] = dpooled * _pool_inv_count(t0 - HALO, e, t)
        half = _pool_lane_half()
        du = jnp.zeros((tm, POOL_W), F32)
        for o, win in _taps(dpn_ext, slice(None), range(-7, 9), tm, tmp):
            m = ((o > -half) & (o <= half)).astype(F32)
            du = du + m * win
        dproj_ref[:, C_POOL[0]:C_POOL[1]] = (du - dpooled[main]).astype(BF16)
        pooled = _pool_forward(p_ext, tm, t0, t, tmp)
        pooled_b = pooled.astype(BF16)
        ya_pre = _dot(pooled_b, pw_ref[...])
        dps_ref[...] += jnp.sum(dy_ext[main, 0:256] * ya_pre, axis=0, keepdims=True)
        dpw_ref[...] += _dot_tn(pooled_b, dya[main].astype(BF16))

        dwin_ref[...] += _dot_tn(yb_ref[...], dproj_ref[...])
        dwout_ref[...] += _dot_tn(yc_ref[...], dm_ref[...])

    one = lambda i: (0, 0)
    tok = lambda i: (i, 0)
    small = [((d, pc), F32), ((ycat.shape[1], d), F32),
             ((POOL_W, POOL_W), F32), ((1, POOL_W), F32), ((8, CONV_W), F32), ((32, CONV_W), F32),
             ((1, CONV_W), F32), ((1, CONV_W), F32), ((1, CONV_W), F32)]
    return pl.pallas_call(
        body, name="mix_bwd",
        grid=(n_i,),
        in_specs=_halo_specs(tm, pc, t) + _halo_specs(tm, dzb.shape[1], t) + _halo_specs(tm, CONV_W, t) + [
            pl.BlockSpec((tm, d), tok), pl.BlockSpec((tm, ycat.shape[1]), tok),
            pl.BlockSpec(w_out_t.shape, one),
            pl.BlockSpec((POOL_W, POOL_W), one), pl.BlockSpec((POOL_W, POOL_W), one), pl.BlockSpec((1, POOL_W), one),
            pl.BlockSpec((8, CONV_W), one), pl.BlockSpec((32, CONV_W), one),
            pl.BlockSpec((1, CONV_W), one), pl.BlockSpec((1, CONV_W), one)],
        out_specs=[pl.BlockSpec((tm, pc), lambda i: (i, 0))] + [pl.BlockSpec(s, one) for s, _ in small],
        out_shape=[jax.ShapeDtypeStruct((t, pc), BF16)] + [jax.ShapeDtypeStruct(s, dt) for s, dt in small],
        scratch_shapes=[pltpu.VMEM((e, pc), F32), pltpu.VMEM((e, dzb.shape[1]), BF16), pltpu.VMEM((e, 1024), F32),
                        pltpu.VMEM((e, CONV_W), F32),
                        pltpu.VMEM((e, CONV_W), F32), pltpu.VMEM((e, CONV_W), F32), pltpu.VMEM((e, CONV_W), F32),
                        pltpu.VMEM((e, CONV_W), F32), pltpu.VMEM((e, POOL_W), F32),
                        pltpu.VMEM((e - 8, CONV_W), F32),
                        pltpu.VMEM((8, e - 8, CONV_W), F32), pltpu.VMEM((tm, CONV_W), F32)],
        compiler_params=_cparams(),
    )(proj, proj, proj, dzb, dzb, dzb, a1, a1, a1, yb, ycat,
      w_out_t, pw_bd, pw_bd_t, pool_scale, sconv_w, cconv_w, cnorm_g, cnorm_b)


def _mesh_pos():
    return lax.axis_index("x"), lax.axis_index("y"), lax.axis_index("c")


def _flip(v, f):
    return 1 - v if f else v


class _Comm:
    def __init__(self, kind, arrs, lands=None, layer=0):
        self.kind = kind
        self.arrs = list(arrs)
        self.n = len(self.arrs)
        self.lands = None if lands is None else list(lands)
        self.layer = layer
        if kind == "gather":
            self.flips = [(1, 0, 0), (0, 1, 0), (1, 1, 0)]
            self.out_shape = [jax.ShapeDtypeStruct((N_CHIPS,) + a.shape, a.dtype) for a in self.arrs]
        else:
            self.flips = [(fx, fy, fc) for fx in (0, 1) for fy in (0, 1) for fc in (0, 1)][1:]
            self.out_shape = [jax.ShapeDtypeStruct(b.shape, b.dtype) for b in self.lands]
        self.n_peer = len(self.flips)

    def operands(self):
        return self.arrs + (self.lands or [])

    def scratch(self):
        return [pltpu.SemaphoreType.DMA((self.n * self.n_peer,)), pltpu.SemaphoreType.DMA((self.n * self.n_peer,)),
                pltpu.SemaphoreType.DMA((self.n,))]

    def aliases(self, in_off, out_off):
        if self.lands is None:
            return {}
        return {in_off + self.n + j: out_off + j for j in range(self.n)}

    def _copies(self, ins, outs, sems):
        send_sems, recv_sems, loc_sems = sems
        x, y, c = _mesh_pos()
        local, sends, recvs = [], [], []
        for a in range(self.n):
            def src(px, py):
                return ins[a] if self.kind == "gather" else ins[a].at[2 * px + py]

            def dst(px, py, pc):
                if self.kind == "gather":
                    return outs[a].at[2 * px + py]
                return outs[a].at[4 * px + 2 * py + pc, self.layer]

            local.append(pltpu.make_async_copy(src(x, y), dst(x, y, c), loc_sems.at[a]))
            for k, (fx, fy, fc) in enumerate(self.flips):
                px, py, pc = _flip(x, fx), _flip(y, fy), _flip(c, fc)
                sem = a * self.n_peer + k
                sends.append(pltpu.make_async_remote_copy(
                    src_ref=src(px, py), dst_ref=dst(x, y, c),
                    send_sem=send_sems.at[sem], recv_sem=recv_sems.at[sem],
                    device_id=(px, py, pc), device_id_type=MESH))
                recvs.append(pltpu.make_async_remote_copy(
                    src_ref=src(px, py), dst_ref=dst(px, py, pc),
                    send_sem=send_sems.at[sem], recv_sem=recv_sems.at[sem],
                    device_id=(px, py, pc), device_id_type=MESH))
        return local, sends, recvs

    def start(self, ins, outs, sems):
        local, sends, _ = self._copies(ins, outs, sems)
        for cp in local + sends:
            cp.start()

    def wait(self, ins, outs, sems):
        local, _, recvs = self._copies(ins, outs, sems)
        for cp in local:
            cp.wait()
        for cp in recvs:
            cp.wait()


def _split_refs(refs, n_in, n_out, n_scr, comm):
    c_in = len(comm.operands()) if comm is not None else 0
    c_out = comm.n if comm is not None else 0
    cuts = [n_in, c_in, n_out, c_out, n_scr]
    out, pos = [], 0
    for m in cuts:
        out.append(refs[pos:pos + m])
        pos += m
    out.append(refs[pos:])
    return out


def _call_with_comm(body, name, grid, in_specs, out_specs, out_shape, scratch, args, comm):
    hbm = pl.BlockSpec(memory_space=pl.ANY)
    aliases = {}
    if comm is not None:
        aliases = comm.aliases(len(in_specs), len(out_specs))
        in_specs = in_specs + [hbm] * len(comm.operands())
        out_specs = out_specs + [hbm] * comm.n
        out_shape = out_shape + comm.out_shape
        scratch = scratch + comm.scratch()
        args = args + comm.operands()
        name = name + "_" + comm.kind
    res = pl.pallas_call(
        body, name=name, grid=grid, in_specs=in_specs, out_specs=out_specs, out_shape=out_shape,
        scratch_shapes=scratch, input_output_aliases=aliases, compiler_params=_cparams(),
    )(*args)
    if comm is None:
        return res, None
    return res[:len(res) - comm.n], res[len(res) - comm.n:]


def run_comm(comm):
    def body(*refs):
        _, c_in, _, c_out, _, c_sem = _split_refs(refs, 0, 0, 0, comm)
        comm.start(c_in, c_out, c_sem)
        comm.wait(c_in, c_out, c_sem)

    hbm = pl.BlockSpec(memory_space=pl.ANY)
    return pl.pallas_call(
        body, name="comm_" + comm.kind,
        in_specs=[hbm] * len(comm.operands()), out_specs=[hbm] * comm.n, out_shape=comm.out_shape,
        scratch_shapes=comm.scratch(), input_output_aliases=comm.aliases(0, 0),
    )(*comm.operands())


def exchange(arrs, per_chip):
    n = len(arrs)
    flips = [(fx, fy, fc) for fx in (0, 1) for fy in (0, 1) for fc in (0, 1)][1:]

    def body(*refs):
        ins, outs = refs[:n], refs[n:2 * n]
        send_sems, recv_sems, loc_sems = refs[2 * n:]
        x, y, c = _mesh_pos()
        me = 4 * x + 2 * y + c
        local = []
        remote = []
        for a in range(n):
            def part(px, py):
                return ins[a].at[2 * px + py] if per_chip else ins[a]

            cp = pltpu.make_async_copy(part(x, y), outs[a].at[me], loc_sems.at[a])
            cp.start()
            local.append(cp)
            for k, (fx, fy, fc) in enumerate(flips):
                px, py, pc = _flip(x, fx), _flip(y, fy), _flip(c, fc)
                sem = a * 7 + k
                rc = pltpu.make_async_remote_copy(
                    src_ref=part(px, py), dst_ref=outs[a].at[me],
                    send_sem=send_sems.at[sem], recv_sem=recv_sems.at[sem],
                    device_id=(px, py, pc), device_id_type=MESH)
                rc.start()
                remote.append(pltpu.make_async_remote_copy(
                    src_ref=part(px, py), dst_ref=outs[a].at[4 * px + 2 * py + pc],
                    send_sem=send_sems.at[sem], recv_sem=recv_sems.at[sem],
                    device_id=(px, py, pc), device_id_type=MESH))
        for cp in local:
            cp.wait()
        for rc in remote:
            rc.wait()

    hbm = pl.BlockSpec(memory_space=pl.ANY)
    shapes = [a.shape[1:] if per_chip else a.shape for a in arrs]
    return pl.pallas_call(
        body, name="exchange_per_chip" if per_chip else "exchange_all",
        in_specs=[hbm] * n, out_specs=[hbm] * n,
        out_shape=[jax.ShapeDtypeStruct((N_DEV,) + s, a.dtype) for s, a in zip(shapes, arrs)],
        scratch_shapes=[pltpu.SemaphoreType.DMA((7 * n,)), pltpu.SemaphoreType.DMA((7 * n,)),
                        pltpu.SemaphoreType.DMA((n,))],
    )(*arrs)


def adamw(parts, w, m, v, name):
    k_n, n_l, r, c = parts.shape
    tr = r
    for cand in (512, 256, 128, 64, 32, 16, 8):
        if r % cand == 0:
            tr = cand
            break

    def body(p_ref, w_ref, m_ref, v_ref, g_ref, d_ref, mo_ref, vo_ref):
        g = p_ref[0, 0].astype(F32)
        for k in range(1, k_n):
            g = g + p_ref[k, 0].astype(F32)
        m_new = ADAM_B1 * m_ref[0] + (1.0 - ADAM_B1) * g
        v_new = ADAM_B2 * v_ref[0] + (1.0 - ADAM_B2) * (g * g)
        m_hat = m_new / (1.0 - ADAM_B1 ** ADAM_STEP)
        v_hat = v_new / (1.0 - ADAM_B2 ** ADAM_STEP)
        g_ref[0] = g
        d_ref[0] = -ADAM_LR * (m_hat / (jnp.sqrt(v_hat) + ADAM_EPS) + ADAM_WD * w_ref[0])
        mo_ref[0] = m_new
        vo_ref[0] = v_new

    blk = pl.BlockSpec((1, tr, c), lambda l, i: (l, i, 0))
    return pl.pallas_call(
        body, name=name,
        grid=(n_l, r // tr),
        in_specs=[pl.BlockSpec((k_n, 1, tr, c), lambda l, i: (0, l, i, 0)), blk, blk, blk],
        out_specs=[blk, blk, blk, blk],
        out_shape=[jax.ShapeDtypeStruct((n_l, r, c), F32)] * 4,
        compiler_params=_cparams(),
    )(parts, w, m, v)


def _block_diag(pool_w):
    out = jnp.zeros((POOL_W, POOL_W), pool_w.dtype)
    for g in range(4):
        out = lax.dynamic_update_slice(out, pool_w[g], (64 * g, 64 * g))
    return out


def _pad_rows(a, rows):
    return jnp.pad(a, ((0, rows - a.shape[0]), (0, 0)))


def kernel(x, ln1_g, ln1_b, ffn1_w_gate, ffn1_w_up, ffn1_w_down, mix_w_in, pool_w, pool_scale, sconv_w, cconv_w, cconv_b, cnorm_g, cnorm_b, mix_w_out, ln2_g, ln2_b, ffn2_w_gate, ffn2_w_up, ffn2_w_down, ln3_g, ln3_b, loss_target, m_ln1_g, m_ln1_b, m_ffn1_w_gate, m_ffn1_w_up, m_ffn1_w_down, m_mix_w_in, m_pool_w, m_pool_scale, m_sconv_w, m_cconv_w, m_cconv_b, m_cnorm_g, m_cnorm_b, m_mix_w_out, m_ln2_g, m_ln2_b, m_ffn2_w_gate, m_ffn2_w_up, m_ffn2_w_down, m_ln3_g, m_ln3_b, v_ln1_g, v_ln1_b, v_ffn1_w_gate, v_ffn1_w_up, v_ffn1_w_down, v_mix_w_in, v_pool_w, v_pool_scale, v_sconv_w, v_cconv_w, v_cconv_b, v_cnorm_g, v_cnorm_b, v_mix_w_out, v_ln2_g, v_ln2_b, v_ffn2_w_gate, v_ffn2_w_up, v_ffn2_w_down, v_ln3_g, v_ln3_b):
    names = ['ln1_g', 'ln1_b', 'ffn1_w_gate', 'ffn1_w_up', 'ffn1_w_down', 'mix_w_in', 'pool_w', 'pool_scale',
             'sconv_w', 'cconv_w', 'cconv_b', 'cnorm_g', 'cnorm_b', 'mix_w_out', 'ln2_g', 'ln2_b',
             'ffn2_w_gate', 'ffn2_w_up', 'ffn2_w_down', 'ln3_g', 'ln3_b']
    w = dict(zip(names, (ln1_g, ln1_b, ffn1_w_gate, ffn1_w_up, ffn1_w_down, mix_w_in, pool_w, pool_scale, sconv_w,
                         cconv_w, cconv_b, cnorm_g, cnorm_b, mix_w_out, ln2_g, ln2_b, ffn2_w_gate, ffn2_w_up,
                         ffn2_w_down, ln3_g, ln3_b)))
    mom_m = dict(zip(names, (m_ln1_g, m_ln1_b, m_ffn1_w_gate, m_ffn1_w_up, m_ffn1_w_down, m_mix_w_in, m_pool_w,
                             m_pool_scale, m_sconv_w, m_cconv_w, m_cconv_b, m_cnorm_g, m_cnorm_b, m_mix_w_out,
                             m_ln2_g, m_ln2_b, m_ffn2_w_gate, m_ffn2_w_up, m_ffn2_w_down, m_ln3_g, m_ln3_b)))
    mom_v = dict(zip(names, (v_ln1_g, v_ln1_b, v_ffn1_w_gate, v_ffn1_w_up, v_ffn1_w_down, v_mix_w_in, v_pool_w,
                             v_pool_scale, v_sconv_w, v_cconv_w, v_cconv_b, v_cnorm_g, v_cnorm_b, v_mix_w_out,
                             v_ln2_g, v_ln2_b, v_ffn2_w_gate, v_ffn2_w_up, v_ffn2_w_down, v_ln3_g, v_ln3_b)))
    big = ['ffn1_w_gate', 'ffn1_w_up', 'ffn1_w_down', 'mix_w_in', 'mix_w_out',
           'ffn2_w_gate', 'ffn2_w_up', 'ffn2_w_down']
    n_l = ln1_g.shape[0]
    d = x.shape[-1]
    fs = ffn1_w_gate.shape[-1]
    ws_in = mix_w_in.shape[-1]
    cs = sconv_w.shape[-1]
    chip = 2 * lax.axis_index("x") + lax.axis_index("y")

    keys_a = ['ffn2_w_gate', 'ffn2_w_up', 'ffn2_w_down', 'mix_w_in', 'mix_w_out']
    keys_b = ['ffn1_w_gate', 'ffn1_w_up', 'ffn1_w_down']

    def shards_b(l):
        return [w[k][l].astype(BF16) for k in keys_b]

    def shards_a(l):
        conv_loc = jnp.concatenate([sconv_w[l], cconv_w[l]], axis=0)
        return [w[k][l].astype(BF16) for k in keys_a] + [conv_loc]

    def cols(a):
        return jnp.transpose(a, (1, 0, 2)).reshape(a.shape[1], -1)

    def layer_weights(l, got_b, got_a):
        gw = dict(zip(keys_b + keys_a, list(got_b) + list(got_a[:-1])))
        conv_all = jnp.transpose(got_a[-1], (1, 0, 2)).reshape(SCONV_K + CCONV_K, N_CHIPS * cs)
        return dict(
            g1=gw['ffn1_w_gate'], u1=gw['ffn1_w_up'], d1=gw['ffn1_w_down'],
            g2=gw['ffn2_w_gate'], u2=gw['ffn2_w_up'], d2=gw['ffn2_w_down'],
            w_in=cols(gw['mix_w_in']),
            w_out=gw['mix_w_out'].reshape(-1, d),
            pw=_block_diag(pool_w[l]).astype(BF16),
            ps=pool_scale[l][None], sw=_pad_rows(conv_all[:SCONV_K], 8), cw=_pad_rows(conv_all[SCONV_K:], 32),
            cb=cconv_b[l][None], cg=cnorm_g[l][None], cbb=cnorm_b[l][None],
        )

    h = x[0]
    hb = h.astype(BF16)
    target = loss_target[0]
    saved = []
    layer_w = []
    got_b = run_comm(_Comm("gather", shards_b(0)))
    for l in range(n_l):
        (y1, y1b, z1, gs1, us1), got_a = ffn_fwd(h, hb, got_b[0], got_b[1], got_b[2], ln1_g[l][None],
                                                 ln1_b[l][None], _Comm("gather", shards_a(l)))
        lw = layer_weights(l, got_b, got_a)
        layer_w.append(lw)
        proj, ycat, a1, y2, y2b, z2 = mix_fwd(y1, y1b, lw['w_in'], lw['w_out'], ln2_g[l][None], ln2_b[l][None],
                                              lw['pw'], lw['ps'], lw['sw'], lw['cw'], lw['cb'], lw['cg'], lw['cbb'])
        comm = _Comm("gather", shards_b(l + 1)) if l + 1 < n_l else None
        (y3, y3b, z3, gs2, us2), got_b = ffn_fwd(y2, y2b, lw['g2'], lw['u2'], lw['d2'], ln3_g[l][None],
                                                 ln3_b[l][None], comm)
        saved.append(dict(x0b=hb, z1=z1, y1b=y1b, proj=proj, ycat=ycat, a1=a1, z2=z2, y2b=y2b, z3=z3,
                          gs1=gs1, us1=us1, gs2=gs2, us2=us2))
        h, hb = y3, y3b

    dy, loss_blk = loss_and_grad(h, target)
    loss = lax.psum(loss_blk[0, 0], ("x", "y", "c"))

    g_loc = {k: [None] * n_l for k in names if k not in big}
    transposed = ['ffn1_w_gate', 'ffn1_w_up', 'ffn2_w_gate', 'ffn2_w_up', 'mix_w_in']

    def shard_shape(k):
        shp = w[k].shape[1:]
        return shp[::-1] if k in transposed else shp

    lands = {k: lax.empty((N_DEV, n_l) + shard_shape(k), BF16) for k in big}

    def scatter(keys, arrs, layer):
        return _Comm("scatter", arrs, [lands[k] for k in keys], layer)

    pend_b = None
    parts = None
    for l in reversed(range(n_l)):
        lw, sv = layer_w[l], saved[l]
        dzb, dxr, g_loc['ln3_g'][l], g_loc['ln3_b'][l] = ln_bwd(dy, parts, sv['z3'], ln3_g[l][None])
        comm = None if pend_b is None else scatter(keys_b, pend_b, l + 1)
        (parts, dg2, du2, dd2), landed = ffn_bwd(sv['y2b'], dzb, sv['gs2'], sv['us2'], lw['g2'], lw['u2'],
                                                         lw['d2'], comm)
        if comm is not None:
            lands.update(zip(keys_b, landed))
        dzb, dxr, g_loc['ln2_g'][l], g_loc['ln2_b'][l] = ln_bwd(dxr, parts, sv['z2'], ln2_g[l][None])
        (dproj, dw_in, dw_out, dpw, g_loc['pool_scale'][l], dsw, dcw, g_loc['cconv_b'][l], g_loc['cnorm_g'][l],
         g_loc['cnorm_b'][l]) = mix_bwd(sv['proj'], dzb, lw['w_out'].T, sv['a1'], sv['y1b'], sv['ycat'],
                                        lw['pw'], lw['pw'].T, lw['ps'], lw['sw'], lw['cw'], lw['cg'], lw['cbb'])
        g_loc['pool_w'][l] = jnp.stack([dpw[64 * g:64 * g + 64, 64 * g:64 * g + 64] for g in range(4)])
        g_loc['sconv_w'][l] = dsw[:SCONV_K]
        g_loc['cconv_w'][l] = dcw[:CCONV_K]
        dw_in_c = jnp.transpose(dw_in.reshape(d, N_CHIPS, ws_in), (1, 2, 0)).astype(BF16)
        dw_out_c = dw_out.reshape(N_CHIPS, -1, d).astype(BF16)
        dzb, dxr, g_loc['ln1_g'][l], g_loc['ln1_b'][l] = ln_bwd(dxr, None, sv['z1'], ln1_g[l][None],
                                                                 mm=(dproj, lw['w_in'].T))
        comm = scatter(keys_a, [dg2, du2, dd2, dw_in_c, dw_out_c], l)
        (parts, dg1, du1, dd1), landed = ffn_bwd(sv['x0b'], dzb, sv['gs1'], sv['us1'], lw['g1'], lw['u1'],
                                                         lw['d1'], comm)
        lands.update(zip(keys_a, landed))
        pend_b = [dg1, du1, dd1]
        dy = dxr
    (grad_x,), landed = add_parts(dy, parts, scatter(keys_b, pend_b, 0))
    grad_x = grad_x[None]
    lands.update(zip(keys_b, landed))
    parts_big = lands

    small = [k for k in names if k not in big]
    small_full = {}
    for k in small:
        a = jnp.stack(g_loc[k])
        small_full[k] = a.reshape(n_l, -1) if a.shape[1] == 1 else a
    flat = jnp.concatenate([small_full[k].reshape(-1) for k in small])
    n_flat = flat.shape[0]
    rows = -(-n_flat // (SMALL_ROWS * 128)) * SMALL_ROWS
    flat = jnp.pad(flat, (0, rows * 128 - n_flat)).reshape(rows, 128)
    parts_small = exchange([flat], per_chip=False)[0]

    out_g, out_d, out_m, out_v = {}, {}, {}, {}
    for k in big:
        tr = (lambda a: jnp.swapaxes(a, 1, 2)) if k in transposed else (lambda a: a)
        res = adamw(parts_big[k], tr(w[k]), tr(mom_m[k]), tr(mom_v[k]), name="adamw_" + k)
        out_g[k], out_d[k], out_m[k], out_v[k] = [tr(o) for o in res]

    zeros = jnp.zeros((1, rows, 128), F32)
    g_sum = adamw(parts_small[:, None], zeros, zeros, zeros, name="sum_small")[0].reshape(-1)
    off = 0
    for k in small:
        full = small_full[k]
        g = g_sum[off:off + full.size].reshape(full.shape)
        off += full.size
        if k in ('sconv_w', 'cconv_w'):
            g = lax.dynamic_slice_in_dim(g, chip * cs, cs, axis=2)
        out_g[k] = g.reshape(w[k].shape)

    def pack(dct):
        f = jnp.concatenate([dct[k].reshape(-1) for k in small])
        r2 = -(-f.shape[0] // (SMALL_ROWS * 128)) * SMALL_ROWS
        return jnp.pad(f, (0, r2 * 128 - f.shape[0])).reshape(r2, 128), f.shape[0]

    gp, n_small = pack(out_g)
    wp, _ = pack(w)
    mp, _ = pack(mom_m)
    vp, _ = pack(mom_v)
    _, dp, mo, vo = adamw(gp[None, None], wp[None], mp[None], vp[None], name="adamw_small")
    off = 0
    for k in small:
        sz = w[k].size
        out_d[k] = dp.reshape(-1)[off:off + sz].reshape(w[k].shape)
        out_m[k] = mo.reshape(-1)[off:off + sz].reshape(w[k].shape)
        out_v[k] = vo.reshape(-1)[off:off + sz].reshape(w[k].shape)
        off += sz

    return (loss, grad_x, *[out_g[k] for k in names], *[out_d[k] for k in names],
            *[out_m[k] for k in names], *[out_v[k] for k in names])
```
